```python
import jax, jax.numpy as jnp
from jax import lax
import numpy as np

D_MODEL = 4096
BATCH = 4
SEQ = 2048
DEPTH = 2

N_MIXERS = 2
HEAD_DIM = 64
RWKV_HEADS = D_MODEL // HEAD_DIM
RWKV_LORA = max(32, int(round(1.8 * D_MODEL ** 0.5 / 32)) * 32)
ATTN_Q_HEADS = D_MODEL // HEAD_DIM
ATTN_KV_HEADS = 8
ATTN_GROUP = ATTN_Q_HEADS // ATTN_KV_HEADS
WINDOW = 128
BLOCK = 128
N_RWKV = (DEPTH + 1) // 2
N_ATTN = DEPTH // 2
RWKV_SHIFTED = 3 * D_MODEL + 2 * RWKV_LORA
RWKV_IN = RWKV_SHIFTED + D_MODEL
KV_WIDTH = ATTN_KV_HEADS * HEAD_DIM
ATTN_IN = 2 * D_MODEL + 2 * KV_WIDTH
NORM_EPS = 1e-5
GN_EPS = HEAD_DIM * 1e-5
L2_EPS = 1e-12
MASK_VALUE = -1e30

kernel_name = "rwkv7_swa_sink_hybrid"


def rms_norm(x, g):
    xf = x.astype(jnp.float32)
    y = xf * lax.rsqrt(jnp.mean(xf * xf, axis=-1, keepdims=True) + NORM_EPS)
    return (y * g.astype(jnp.float32)).astype(x.dtype)


def token_shift(y):
    return jnp.pad(y, ((0, 0), (1, 0), (0, 0)))[:, :-1]


def rwkv7_mixer(h, w_in, mu, w0, w2, a0, a2, k_k, k_a, r_k, gn_w, gn_b, w_out):
    B, T, _ = h.shape
    D, L, H, N = D_MODEL, RWKV_LORA, RWKV_HEADS, HEAD_DIM
    p = (h @ w_in).astype(jnp.float32)
    sh, gate = p[..., :RWKV_SHIFTED], p[..., RWKV_SHIFTED:]
    sh = sh + (token_shift(sh) - sh) * mu
    r, k, v, xw, xa = jnp.split(sh, [D, 2 * D, 3 * D, 3 * D + L], axis=-1)
    w = -jax.nn.softplus(-(w0 + jnp.tanh(xw) @ w2)) - 0.5
    decay = jnp.exp(-jnp.exp(w))
    a = jax.nn.sigmoid(a0 + xa @ a2)
    kk = (k * k_k).reshape(B, T, H, N)
    kk = kk / jnp.maximum(jnp.sqrt(jnp.sum(kk * kk, axis=-1, keepdims=True)), L2_EPS)
    k = k * (1.0 + (a - 1.0) * k_a)
    heads = lambda t: t.reshape(B, T, H, N)
    r, decay, k, v, a = heads(r), heads(decay), heads(k), heads(v), heads(a)
    b = kk * a

    def step(S, inp):
        r_t, w_t, k_t, v_t, kk_t, b_t = inp
        sa = jnp.einsum('bhvk,bhk->bhv', S, kk_t)
        S = S * w_t[:, :, None, :] - sa[..., None] * b_t[:, :, None, :] + v_t[..., None] * k_t[:, :, None, :]
        return S, jnp.einsum('bhvk,bhk->bhv', S, r_t)

    tmaj = lambda t: jnp.moveaxis(t, 1, 0)
    S0 = jnp.zeros((B, H, N, N), jnp.float32)
    _, y = lax.scan(step, S0, (tmaj(r), tmaj(decay), tmaj(k), tmaj(v), tmaj(kk), tmaj(b)))
    y = jnp.moveaxis(y, 0, 1)
    mean = jnp.mean(y, axis=-1, keepdims=True)
    var = jnp.mean(jnp.square(y - mean), axis=-1, keepdims=True)
    y = ((y - mean) * lax.rsqrt(var + GN_EPS)).reshape(B, T, D) * gn_w + gn_b
    y = y + (jnp.sum(r * k * r_k, axis=-1, keepdims=True) * v).reshape(B, T, D)
    y = y * jax.nn.silu(gate)
    return y.astype(h.dtype) @ w_out


def swa_sink_mixer(h, w_in, sinks, w_out):
    B, T, _ = h.shape
    Hk, G, N = ATTN_KV_HEADS, ATTN_GROUP, HEAD_DIM
    nb = T // BLOCK
    p = h @ w_in
    q, k, v, gate = jnp.split(p, [D_MODEL, D_MODEL + KV_WIDTH, D_MODEL + 2 * KV_WIDTH], axis=-1)
    q = q.reshape(B, nb, BLOCK, Hk, G, N)

    def band(t):
        cur = t.reshape(B, nb, BLOCK, Hk, N)
        prev = jnp.pad(cur, ((0, 0), (1, 0), (0, 0), (0, 0), (0, 0)))[:, :-1]
        return jnp.concatenate([prev, cur], axis=2)

    kb, vb = band(k), band(v)
    s = jnp.einsum('bnqhgd,bnkhd->bnhgqk', q, kb).astype(jnp.float32) * (N ** -0.5)
    qi = jnp.arange(BLOCK)[:, None]
    kj = jnp.arange(2 * BLOCK)[None, :]
    delta = BLOCK + qi - kj
    in_win = (delta >= 0) & (delta < WINDOW)
    kpos = jnp.arange(nb)[:, None, None] * BLOCK - BLOCK + kj[None]
    mask = in_win[None] & (kpos >= 0)
    s = jnp.where(mask[None, :, None, None], s, MASK_VALUE)
    sink = jnp.broadcast_to(sinks.astype(jnp.float32).reshape(1, 1, Hk, G, 1, 1), s.shape[:-1] + (1,))
    prob = jax.nn.softmax(jnp.concatenate([s, sink], axis=-1), axis=-1)[..., :-1]
    o = jnp.einsum('bnhgqk,bnkhd->bnqhgd', prob.astype(vb.dtype), vb).reshape(B, T, D_MODEL)
    o = o * jax.nn.silu(gate)
    return o @ w_out


def setup_inputs(seed: int = 0) -> dict:
    key = jax.random.key(seed)
    ks = jax.random.split(key, 20)
    D, L, H, N = D_MODEL, RWKV_LORA, RWKV_HEADS, HEAD_DIM
    nrm = lambda k, shape, s: jax.random.normal(k, shape, jnp.float32) * s
    return {
        "x": nrm(ks[0], (BATCH, SEQ, D), 1.0),
        "norm_w": 1.0 + nrm(ks[1], (DEPTH, D), 0.05),
        "final_norm_w": 1.0 + nrm(ks[2], (D,), 0.05),
        "rwkv_w_in": nrm(ks[3], (N_RWKV, D, RWKV_IN), D ** -0.5),
        "rwkv_mu": jax.random.uniform(ks[4], (N_RWKV, RWKV_SHIFTED), jnp.float32, 0.0, 1.0),
        "rwkv_w0": jax.random.uniform(ks[5], (N_RWKV, D), jnp.float32, -3.0, 1.0),
        "rwkv_w2": nrm(ks[6], (N_RWKV, L, D), 0.5 * L ** -0.5),
        "rwkv_a0": nrm(ks[7], (N_RWKV, D), 0.5),
        "rwkv_a2": nrm(ks[8], (N_RWKV, L, D), 0.5 * L ** -0.5),
        "rwkv_k_k": 0.85 + nrm(ks[9], (N_RWKV, D), 0.05),
        "rwkv_k_a": 1.0 + nrm(ks[10], (N_RWKV, D), 0.05),
        "rwkv_r_k": nrm(ks[11], (N_RWKV, H, N), 0.1),
        "rwkv_gn_w": 1.0 + nrm(ks[12], (N_RWKV, D), 0.05),
        "rwkv_gn_b": nrm(ks[13], (N_RWKV, D), 0.02),
        "rwkv_w_out": nrm(ks[14], (N_RWKV, D, D), D ** -0.5),
        "attn_w_in": nrm(ks[15], (N_ATTN, D, ATTN_IN), D ** -0.5),
        "attn_sinks": nrm(ks[16], (N_ATTN, ATTN_Q_HEADS), 0.5),
        "attn_w_out": nrm(ks[17], (N_ATTN, D, D), D ** -0.5),
    }


def reference(x, norm_w, final_norm_w, rwkv_w_in, rwkv_mu, rwkv_w0, rwkv_w2, rwkv_a0, rwkv_a2,
              rwkv_k_k, rwkv_k_a, rwkv_r_k, rwkv_gn_w, rwkv_gn_b, rwkv_w_out,
              attn_w_in, attn_sinks, attn_w_out):
    h = x
    for i in range(DEPTH):
        hn = rms_norm(h, norm_w[i])
        j = i // N_MIXERS
        if i % N_MIXERS == 0:
            h = h + rwkv7_mixer(hn, rwkv_w_in[j], rwkv_mu[j], rwkv_w0[j], rwkv_w2[j], rwkv_a0[j],
                                rwkv_a2[j], rwkv_k_k[j], rwkv_k_a[j], rwkv_r_k[j], rwkv_gn_w[j],
                                rwkv_gn_b[j], rwkv_w_out[j]).astype(h.dtype)
        else:
            h = h + swa_sink_mixer(hn, attn_w_in[j], attn_sinks[j], attn_w_out[j]).astype(h.dtype)
    return rms_norm(h, final_norm_w)
```

```python
import functools

import jax
import jax.numpy as jnp
from jax import lax
from jax.experimental import pallas as pl
from jax.experimental.pallas import tpu as pltpu

HEAD_DIM = 64
LORA = 128
KV_HEADS = 8
GROUP = 8
BLOCK = 128
NORM_EPS = 1e-5
GN_EPS = HEAD_DIM * 1e-5
L2_EPS = 1e-12
MASK_VALUE = -1e30

VMEM_LIMIT_BYTES = 56 * 1024 * 1024
SCAN_TB = 16


def _params(*sem):
    return pltpu.CompilerParams(dimension_semantics=sem, vmem_limit_bytes=VMEM_LIMIT_BYTES)


def _rms_kernel(x_ref, g_ref, o_ref):
    x = x_ref[...]
    ms = jnp.mean(x * x, axis=-1, keepdims=True)
    o_ref[...] = (x * lax.rsqrt(ms + NORM_EPS) * g_ref[...]).astype(o_ref.dtype)


def _rms_norm(x2d, g, out_dtype, tm=256):
    m, d = x2d.shape
    return pl.pallas_call(
        _rms_kernel,
        grid=(m // tm,),
        in_specs=[pl.BlockSpec((tm, d), lambda i: (i, 0)),
                  pl.BlockSpec((1, d), lambda i: (0, 0))],
        out_specs=pl.BlockSpec((tm, d), lambda i: (i, 0)),
        out_shape=jax.ShapeDtypeStruct((m, d), out_dtype),
        compiler_params=_params("parallel"),
        name="rmsnorm",
    )(x2d, g.reshape(1, d))


def _mm_kernel(a_ref, b_ref, o_ref):
    o_ref[...] = jnp.dot(a_ref[...], b_ref[...], preferred_element_type=jnp.float32)


def _mm_res_kernel(a_ref, b_ref, r_ref, o_ref):
    o_ref[...] = r_ref[...] + jnp.dot(a_ref[...], b_ref[...],
                                      preferred_element_type=jnp.float32)


def _matmul(a, b, residual=None, tm=512, tn=1024, name="matmul"):
    m, k = a.shape
    _, n = b.shape
    tm = min(tm, m)
    grid = (n // tn, m // tm)
    in_specs = [pl.BlockSpec((tm, k), lambda j, i: (i, 0)),
                pl.BlockSpec((k, tn), lambda j, i: (0, j))]
    args = [a, b]
    kern = _mm_kernel
    if residual is not None:
        in_specs.append(pl.BlockSpec((tm, tn), lambda j, i: (i, j)))
        args.append(residual)
        kern = _mm_res_kernel
    return pl.pallas_call(
        kern,
        grid=grid,
        in_specs=in_specs,
        out_specs=pl.BlockSpec((tm, tn), lambda j, i: (i, j)),
        out_shape=jax.ShapeDtypeStruct((m, n), jnp.float32),
        compiler_params=_params("parallel", "parallel"),
        name=name,
    )(*args)


def _lora_kernel(x_ref, mu_ref, w2_ref, a2_ref, w0_ref, a0_ref, wl_ref, al_ref):
    x = x_ref[...]
    row = lax.broadcasted_iota(jnp.int32, x.shape, 0)
    prev = jnp.where(row == 0, 0.0, pltpu.roll(x, 1, axis=0))
    x = x + (prev - x) * mu_ref[...]
    xw = jnp.tanh(x[:, :LORA]).astype(jnp.bfloat16)
    xa = x[:, LORA:].astype(jnp.bfloat16)
    wl_ref[...] = w0_ref[...] + jnp.dot(xw, w2_ref[...], preferred_element_type=jnp.float32)
    al_ref[...] = a0_ref[...] + jnp.dot(xa, a2_ref[...], preferred_element_type=jnp.float32)


def _lora(p, batch, seq, col_block, mu_x, w2, a2, w0, a0, tn=512):
    d = w2.shape[1]
    m = batch * seq
    out = jax.ShapeDtypeStruct((m, d), jnp.float32)
    return pl.pallas_call(
        _lora_kernel,
        grid=(batch, d // tn),
        in_specs=[pl.BlockSpec((seq, 2 * LORA), lambda b, j: (b, col_block)),
                  pl.BlockSpec((1, 2 * LORA), lambda b, j: (0, 0)),
                  pl.BlockSpec((LORA, tn), lambda b, j: (0, j)),
                  pl.BlockSpec((LORA, tn), lambda b, j: (0, j)),
                  pl.BlockSpec((1, tn), lambda b, j: (0, j)),
                  pl.BlockSpec((1, tn), lambda b, j: (0, j))],
        out_specs=[pl.BlockSpec((seq, tn), lambda b, j: (b, j)),
                   pl.BlockSpec((seq, tn), lambda b, j: (b, j))],
        out_shape=[out, out],
        compiler_params=_params("parallel", "parallel"),
        name="rwkv_lora",
    )(p, mu_x.reshape(1, -1), w2, a2, w0.reshape(1, -1), a0.reshape(1, -1))


def _scan_kernel(r_ref, k_ref, v_ref, rp_ref, kp_ref, vp_ref, wl_ref, al_ref, g_ref,
                 mur_ref, muk_ref, muv_ref, kkp_ref, kap_ref, rkp_ref, gnw_ref, gnb_ref,
                 o_ref,
                 s_ref, r_s, w_s, k_s, v_s, kk_s, b_s, y_s):
    tb, n, _ = r_s.shape
    first = pl.program_id(0) == 0

    @pl.when(first)
    def _():
        s_ref[...] = jnp.zeros_like(s_ref)

    def lerp(cur_ref, prev_ref, mu_ref, t):
        cur = cur_ref[t]
        before = jnp.where(t == 0,
                           jnp.where(first, 0.0, prev_ref[0]),
                           cur_ref[jnp.maximum(t - 1, 0)])
        return cur + (before - cur) * mu_ref[...]

    def prep(t, carry):
        r = lerp(r_ref, rp_ref, mur_ref, t)
        k = lerp(k_ref, kp_ref, muk_ref, t)
        v = lerp(v_ref, vp_ref, muv_ref, t)
        z = -wl_ref[t]
        softplus = jnp.maximum(z, 0.0) + jnp.log(1.0 + jnp.exp(-jnp.abs(z)))
        w = -softplus - 0.5
        a = 1.0 / (1.0 + jnp.exp(-al_ref[t]))
        kkr = k * kkp_ref[...]
        nrm = jnp.sqrt(jnp.sum(kkr * kkr, axis=0, keepdims=True))
        kk = kkr / jnp.maximum(nrm, L2_EPS)
        r_s[t] = r
        w_s[t] = jnp.exp(-jnp.exp(w))
        k_s[t] = k * (1.0 + (a - 1.0) * kap_ref[...])
        v_s[t] = v
        kk_s[t] = kk
        b_s[t] = kk * a
        return carry

    lax.fori_loop(0, tb, prep, 0)

    def step(t, carry):
        def row(i, c):
            s_old = s_ref[i]
            sa = jnp.sum(s_old * kk_s[t], axis=0, keepdims=True)
            vv = v_s[t, pl.ds(i, 1), :]
            s_new = s_old * w_s[t] - sa * b_s[t] + vv * k_s[t]
            s_ref[i] = s_new
            y_s[t, pl.ds(i, 1), :] = jnp.sum(s_new * r_s[t], axis=0, keepdims=True)
            return c
        return lax.fori_loop(0, n, row, carry, unroll=2)

    lax.fori_loop(0, tb, step, 0)

    def post(t, carry):
        y = y_s[t]
        mean = jnp.mean(y, axis=0, keepdims=True)
        yc = y - mean
        var = jnp.mean(yc * yc, axis=0, keepdims=True)
        y = yc * lax.rsqrt(var + GN_EPS) * gnw_ref[...] + gnb_ref[...]
        bonus = jnp.sum(r_s[t] * k_s[t] * rkp_ref[...], axis=0, keepdims=True)
        y = y + bonus * v_s[t]
        g = g_ref[t]
        o_ref[t] = y * (g / (1.0 + jnp.exp(-g)))
        return carry

    lax.fori_loop(0, tb, post, 0)


def _scan(rz, kz, vz, wlz, alz, gz, head_params, tb=SCAN_TB):
    seq, n, lanes = rz.shape
    tb = min(tb, seq)
    blk = pl.BlockSpec((tb, n, lanes), lambda i: (i, 0, 0))
    prev = pl.BlockSpec((1, n, lanes), lambda i: (jnp.maximum(i * tb - 1, 0), 0, 0))
    par = pl.BlockSpec((n, lanes), lambda i: (0, 0))
    scratch = [pltpu.VMEM((n, n, lanes), jnp.float32)]
    scratch += [pltpu.VMEM((tb, n, lanes), jnp.float32) for _ in range(7)]
    return pl.pallas_call(
        _scan_kernel,
        grid=(seq // tb,),
        in_specs=[blk, blk, blk, prev, prev, prev, blk, blk, blk] + [par] * 8,
        out_specs=blk,
        out_shape=jax.ShapeDtypeStruct((seq, n, lanes), jnp.float32),
        scratch_shapes=scratch,
        compiler_params=_params("arbitrary"),
        name="rwkv_scan",
    )(rz, kz, vz, rz, kz, vz, wlz, alz, gz, *head_params)


def _to_lanes(x2d, batch, seq):
    heads = x2d.shape[1] // HEAD_DIM
    x = x2d.reshape(batch, seq, heads, HEAD_DIM)
    return x.transpose(1, 3, 0, 2).reshape(seq, HEAD_DIM, batch * heads)


def _from_lanes(xz, batch):
    seq, n, lanes = xz.shape
    heads = lanes // batch
    x = xz.reshape(seq, n, batch, heads).transpose(2, 0, 3, 1)
    return x.reshape(batch * seq, heads * n)


def _param_to_lanes(p, batch):
    pt = p.reshape(-1, HEAD_DIM).T
    return jnp.tile(pt, (1, batch))


def _rwkv_layer(h2d, batch, seq, norm_w, w_in, mu, w0, w2, a0, a2, k_k, k_a, r_k, gn_w,
                gn_b, w_out):
    d = h2d.shape[1]
    bf = jnp.bfloat16
    hn = _rms_norm(h2d, norm_w, bf)
    p = _matmul(hn, w_in.astype(bf), tn=1280, name="rwkv_in_proj")
    wl, al = _lora(p, batch, seq, (3 * d) // (2 * LORA), mu[3 * d:], w2.astype(bf),
                   a2.astype(bf), w0, a0)
    lanes = functools.partial(_to_lanes, batch=batch, seq=seq)
    rz, kz, vz = (lanes(p[:, i * d:(i + 1) * d]) for i in range(3))
    gz = lanes(p[:, 3 * d + 2 * LORA:])
    head_params = [_param_to_lanes(x, batch) for x in
                   (mu[:d], mu[d:2 * d], mu[2 * d:3 * d], k_k, k_a, r_k.reshape(-1),
                    gn_w, gn_b)]
    yz = _scan(rz, kz, vz, lanes(wl), lanes(al), gz, head_params)
    y = _from_lanes(yz, batch).astype(bf)
    return _matmul(y, w_out.astype(bf), residual=h2d, name="rwkv_out_proj")


def _attn_kernel(sinks_ref, q_ref, kc_ref, kp_ref, vc_ref, vp_ref, g0_ref, g1_ref, g2_ref,
                 g3_ref, o_ref):
    bf = jnp.bfloat16
    n = HEAD_DIM
    not_first = pl.program_id(1) > 0
    kb = jnp.concatenate([kp_ref[...], kc_ref[...]], axis=0).astype(bf)
    vb = jnp.concatenate([vp_ref[...], vc_ref[...]], axis=0).astype(bf)
    qi = lax.broadcasted_iota(jnp.int32, (BLOCK, 2 * BLOCK), 0)
    kj = lax.broadcasted_iota(jnp.int32, (BLOCK, 2 * BLOCK), 1)
    delta = BLOCK + qi - kj
    mask = (delta >= 0) & (delta < BLOCK) & ((kj >= BLOCK) | not_first)
    gate_refs = (g0_ref, g1_ref, g2_ref, g3_ref)
    gate_w = g0_ref.shape[1]
    for hk in range(KV_HEADS):
        k_h = kb[:, hk * n:(hk + 1) * n]
        v_h = vb[:, hk * n:(hk + 1) * n]
        for pair in range(GROUP // 2):
            outs = []
            for g in (2 * pair, 2 * pair + 1):
                h = hk * GROUP + g
                q_h = q_ref[:, h * n:(h + 1) * n].astype(bf)
                s = lax.dot_general(q_h, k_h, (((1,), (1,)), ((), ())),
                                    preferred_element_type=jnp.float32) * (n ** -0.5)
                s = jnp.where(mask, s, MASK_VALUE)
                sink = sinks_ref[h]
                m = jnp.maximum(jnp.max(s, axis=-1, keepdims=True), sink)
                e = jnp.exp(s - m)
                denom = jnp.sum(e, axis=-1, keepdims=True) + jnp.exp(sink - m)
                prob = (e / denom).astype(bf)
                outs.append(jnp.dot(prob, v_h, preferred_element_type=jnp.float32))
            col = (hk * GROUP + 2 * pair) * n
            g_ref = gate_refs[col // gate_w]
            gate = g_ref[:, col % gate_w:col % gate_w + 2 * n]
            o = jnp.concatenate(outs, axis=1) * (gate / (1.0 + jnp.exp(-gate)))
            o_ref[:, col:col + 2 * n] = o.astype(o_ref.dtype)


def _attention(p, batch, seq, d, sinks):
    nb = seq // BLOCK
    kvw = KV_HEADS * HEAD_DIM
    gate_w = 2 * kvw
    row = lambda b, n: b * nb + n
    prow = lambda b, n: b * nb + jnp.maximum(n - 1, 0)
    kblk, vblk = d // kvw, d // kvw + 1
    gblk = (d + 2 * kvw) // gate_w
    in_specs = [pl.BlockSpec(memory_space=pltpu.SMEM),
                pl.BlockSpec((BLOCK, d), lambda b, n: (row(b, n), 0)),
                pl.BlockSpec((BLOCK, kvw), lambda b, n: (row(b, n), kblk)),
                pl.BlockSpec((BLOCK, kvw), lambda b, n: (prow(b, n), kblk)),
                pl.BlockSpec((BLOCK, kvw), lambda b, n: (row(b, n), vblk)),
                pl.BlockSpec((BLOCK, kvw), lambda b, n: (prow(b, n), vblk))]
    in_specs += [pl.BlockSpec((BLOCK, gate_w), lambda b, n, j=j: (row(b, n), gblk + j))
                 for j in range(4)]
    return pl.pallas_call(
        _attn_kernel,
        grid=(batch, nb),
        in_specs=in_specs,
        out_specs=pl.BlockSpec((BLOCK, d), lambda b, n: (row(b, n), 0)),
        out_shape=jax.ShapeDtypeStruct((batch * seq, d), jnp.bfloat16),
        compiler_params=_params("parallel", "parallel"),
        name="swa_attention",
    )(sinks, p, p, p, p, p, p, p, p, p)


def _attn_layer(h2d, batch, seq, norm_w, w_in, sinks, w_out):
    d = h2d.shape[1]
    bf = jnp.bfloat16
    hn = _rms_norm(h2d, norm_w, bf)
    p = _matmul(hn, w_in.astype(bf), tn=1024, name="attn_in_proj")
    o = _attention(p, batch, seq, d, sinks)
    return _matmul(o, w_out.astype(bf), residual=h2d, name="attn_out_proj")


def kernel(x, norm_w, final_norm_w, rwkv_w_in, rwkv_mu, rwkv_w0, rwkv_w2, rwkv_a0, rwkv_a2,
           rwkv_k_k, rwkv_k_a, rwkv_r_k, rwkv_gn_w, rwkv_gn_b, rwkv_w_out,
           attn_w_in, attn_sinks, attn_w_out):
    batch, seq, d = x.shape
    h = x.reshape(batch * seq, d)
    depth = norm_w.shape[0]
    for i in range(depth):
        j = i // 2
        if i % 2 == 0:
            h = _rwkv_layer(h, batch, seq, norm_w[i], rwkv_w_in[j], rwkv_mu[j], rwkv_w0[j],
                            rwkv_w2[j], rwkv_a0[j], rwkv_a2[j], rwkv_k_k[j], rwkv_k_a[j],
                            rwkv_r_k[j], rwkv_gn_w[j], rwkv_gn_b[j], rwkv_w_out[j])
        else:
            h = _attn_layer(h, batch, seq, norm_w[i], attn_w_in[j], attn_sinks[j],
                            attn_w_out[j])
    return _rms_norm(h, final_norm_w, jnp.float32).reshape(batch, seq, d)
```

```python
import functools

import jax
import jax.numpy as jnp
from jax import lax
from jax.experimental import pallas as pl
from jax.experimental.pallas import tpu as pltpu

HEAD_DIM = 64
LORA = 128
KV_HEADS = 8
GROUP = 8
BLOCK = 128
NORM_EPS = 1e-5
GN_EPS = HEAD_DIM * 1e-5
L2_EPS = 1e-12
MASK_VALUE = -1e30

LANE = 128
SUBLANE = 8
VMEM_LIMIT_BYTES = 56 * 1024 * 1024
SCAN_TB = 16
PITCH = HEAD_DIM + SUBLANE


def _params(*sem):
    return pltpu.CompilerParams(dimension_semantics=sem, vmem_limit_bytes=VMEM_LIMIT_BYTES)


def _rms_kernel(x_ref, g_ref, o_ref):
    x = x_ref[...]
    ms = jnp.mean(x * x, axis=-1, keepdims=True)
    o_ref[...] = (x * lax.rsqrt(ms + NORM_EPS) * g_ref[...]).astype(o_ref.dtype)


def _rms_norm(x2d, g, out_dtype, tm=256):
    m, d = x2d.shape
    return pl.pallas_call(
        _rms_kernel,
        grid=(m // tm,),
        in_specs=[pl.BlockSpec((tm, d), lambda i: (i, 0)),
                  pl.BlockSpec((1, d), lambda i: (0, 0))],
        out_specs=pl.BlockSpec((tm, d), lambda i: (i, 0)),
        out_shape=jax.ShapeDtypeStruct((m, d), out_dtype),
        compiler_params=_params("parallel"),
        name="rmsnorm",
    )(x2d, g.reshape(1, d))


def _mm_kernel(a_ref, b_ref, o_ref):
    o_ref[...] = jnp.dot(a_ref[...], b_ref[...], preferred_element_type=jnp.float32)


def _mm_res_kernel(a_ref, b_ref, r_ref, o_ref):
    o_ref[...] = r_ref[...] + jnp.dot(a_ref[...], b_ref[...],
                                      preferred_element_type=jnp.float32)


def _matmul(a, b, residual=None, tm=512, tn=1024, name="matmul"):
    m, k = a.shape
    _, n = b.shape
    tm = min(tm, m)
    grid = (n // tn, m // tm)
    in_specs = [pl.BlockSpec((tm, k), lambda j, i: (i, 0)),
                pl.BlockSpec((k, tn), lambda j, i: (0, j))]
    args = [a, b]
    kern = _mm_kernel
    if residual is not None:
        in_specs.append(pl.BlockSpec((tm, tn), lambda j, i: (i, j)))
        args.append(residual)
        kern = _mm_res_kernel
    return pl.pallas_call(
        kern,
        grid=grid,
        in_specs=in_specs,
        out_specs=pl.BlockSpec((tm, tn), lambda j, i: (i, j)),
        out_shape=jax.ShapeDtypeStruct((m, n), jnp.float32),
        compiler_params=_params("parallel", "parallel"),
        name=name,
    )(*args)


def _store_lanes(o_ref, lead, res, tm):
    low = lax.broadcasted_iota(jnp.int32, (tm, LANE), 1) < HEAD_DIM
    for bp in range(2):
        ra = res[(2 * bp) * tm:(2 * bp + 1) * tm]
        rb = res[(2 * bp + 1) * tm:(2 * bp + 2) * tm]
        for c in range(res.shape[1] // LANE):
            a = ra[:, c * LANE:(c + 1) * LANE]
            b = rb[:, c * LANE:(c + 1) * LANE]
            even = jnp.where(low, a, pltpu.roll(b, HEAD_DIM, axis=1))
            odd = jnp.where(low, pltpu.roll(a, HEAD_DIM, axis=1), b)
            lanes = pl.ds(bp * LANE, LANE)
            shape = (tm // SUBLANE, SUBLANE, LANE)
            o_ref[lead + (slice(None), 2 * c, slice(None), lanes)] = even.reshape(shape)
            o_ref[lead + (slice(None), 2 * c + 1, slice(None), lanes)] = odd.reshape(shape)


def _inproj_lanes_kernel(x_ref, w_ref, o_ref):
    nb, tm, d = x_ref.shape
    res = jnp.dot(x_ref[...].reshape(nb * tm, d), w_ref[...],
                  preferred_element_type=jnp.float32)
    _store_lanes(o_ref, (0,), res, tm)


def _inproj_lanes(hn3, w_perm, tm=256, tn=512):
    batch, seq, d = hn3.shape
    assert batch == 4 and d == 64 * HEAD_DIM
    segs = w_perm.shape[1] // d
    nv = tn // HEAD_DIM
    per_seg = d // tn
    return pl.pallas_call(
        _inproj_lanes_kernel,
        grid=(seq // tm, segs * per_seg),
        in_specs=[pl.BlockSpec((batch, tm, d), lambda i, j: (0, i, 0)),
                  pl.BlockSpec((d, tn), lambda i, j: (0, j))],
        out_specs=pl.BlockSpec((1, tm // SUBLANE, nv, SUBLANE, 2 * LANE),
                               lambda i, j: (j // per_seg, i, j % per_seg, 0, 0)),
        out_shape=jax.ShapeDtypeStruct((segs, seq // SUBLANE, HEAD_DIM, SUBLANE, 2 * LANE),
                                       jnp.float32),
        compiler_params=_params("parallel", "arbitrary"),
        name="rwkv_in_proj",
    )(hn3, w_perm)


def _lora_kernel(x_ref, xp_ref, mu_ref, w2_ref, a2_ref, w0_ref, a0_ref, dec_ref, a_ref):
    nb, tm, _ = x_ref.shape
    first = pl.program_id(0) == 0
    row0 = lax.broadcasted_iota(jnp.int32, (tm, 2 * LORA), 0) == 0
    xw, xa = [], []
    for b in range(nb):
        x = x_ref[b]
        last = jnp.where(first, 0.0, xp_ref[b][SUBLANE - 1:SUBLANE])
        before = jnp.where(row0, last, pltpu.roll(x, 1, axis=0))
        x = x + (before - x) * mu_ref[...]
        xw.append(jnp.tanh(x[:, :LORA]).astype(jnp.bfloat16))
        xa.append(x[:, LORA:].astype(jnp.bfloat16))
    wl = w0_ref[...] + jnp.dot(jnp.concatenate(xw, axis=0), w2_ref[...],
                               preferred_element_type=jnp.float32)
    al = a0_ref[...] + jnp.dot(jnp.concatenate(xa, axis=0), a2_ref[...],
                               preferred_element_type=jnp.float32)
    z = -wl
    softplus = jnp.maximum(z, 0.0) + jnp.log(1.0 + jnp.exp(-jnp.abs(z)))
    _store_lanes(dec_ref, (), jnp.exp(-jnp.exp(-softplus - 0.5)), tm)
    _store_lanes(a_ref, (), 1.0 / (1.0 + jnp.exp(-al)), tm)


def _lora_lanes(x3, mu_x, w2p, a2p, w0p, a0p, tm=256, tn=512):
    batch, seq, _ = x3.shape
    d = w2p.shape[1]
    nv = tn // HEAD_DIM
    out = jax.ShapeDtypeStruct((seq // SUBLANE, HEAD_DIM, SUBLANE, 2 * LANE), jnp.float32)
    ospec = pl.BlockSpec((tm // SUBLANE, nv, SUBLANE, 2 * LANE), lambda i, j: (i, j, 0, 0))
    tq = tm // SUBLANE
    return pl.pallas_call(
        _lora_kernel,
        grid=(seq // tm, d // tn),
        in_specs=[pl.BlockSpec((batch, tm, 2 * LORA), lambda i, j: (0, i, 0)),
                  pl.BlockSpec((batch, SUBLANE, 2 * LORA),
                               lambda i, j: (0, jnp.maximum(i * tq - 1, 0), 0)),
                  pl.BlockSpec((1, 2 * LORA), lambda i, j: (0, 0)),
                  pl.BlockSpec((LORA, tn), lambda i, j: (0, j)),
                  pl.BlockSpec((LORA, tn), lambda i, j: (0, j)),
                  pl.BlockSpec((1, tn), lambda i, j: (0, j)),
                  pl.BlockSpec((1, tn), lambda i, j: (0, j))],
        out_specs=[ospec, ospec],
        out_shape=[out, out],
        compiler_params=_params("parallel", "arbitrary"),
        name="rwkv_lora",
    )(x3, x3, mu_x.reshape(1, -1), w2p, a2p, w0p.reshape(1, -1), a0p.reshape(1, -1))


def _scan_kernel(r_ref, k_ref, v_ref, g_ref, w_ref, a_ref,
                 mur_ref, muk_ref, muv_ref, kkp_ref, kap_ref, rkp_ref, gnw_ref, gnb_ref,
                 o_ref,
                 s_ref, last_ref, bv_s, r_s, w_s, k_s, v_s, kk_s, b_s, y_s):
    tq = w_ref.shape[0]
    n = HEAD_DIM
    first = pl.program_id(0) == 0

    @pl.when(first)
    def _():
        s_ref[...] = jnp.zeros_like(s_ref)
        last_ref[...] = jnp.zeros_like(last_ref)

    sub0 = lax.broadcasted_iota(jnp.int32, (n, SUBLANE, 2 * LANE), 1) == 0

    def lerp(x_ref, idx, mu_ref, q):
        cur = x_ref[0, q]
        prev_tile = last_ref[idx] if q == 0 else x_ref[0, q - 1]
        before = jnp.where(sub0, pltpu.roll(prev_tile, 1, axis=1), pltpu.roll(cur, 1, axis=1))
        return cur + (before - cur) * mu_ref[...]

    def to_steps(dst, q, val):
        for c in range(n):
            for slab in range(2):
                dst[slab, pl.ds(q * SUBLANE * PITCH + c, SUBLANE, stride=PITCH), :] = (
                    val[c, :, slab * LANE:(slab + 1) * LANE])

    for q in range(tq):
        r = lerp(r_ref, 0, mur_ref, q)
        k = lerp(k_ref, 1, muk_ref, q)
        v = lerp(v_ref, 2, muv_ref, q)
        a = a_ref[q]
        kkr = k * kkp_ref[...]
        nrm = jnp.sqrt(jnp.sum(kkr * kkr, axis=0, keepdims=True))
        kk = kkr / jnp.maximum(nrm, L2_EPS)
        k2 = k * (1.0 + (a - 1.0) * kap_ref[...])
        bv_s[q] = jnp.sum(r * k2 * rkp_ref[...], axis=0, keepdims=True) * v
        to_steps(r_s, q, r)
        to_steps(w_s, q, w_ref[q])
        to_steps(k_s, q, k2)
        to_steps(v_s, q, v)
        to_steps(kk_s, q, kk)
        to_steps(b_s, q, kk * a)
    for idx, x_ref in enumerate((r_ref, k_ref, v_ref)):
        last_ref[idx] = x_ref[0, tq - 1]

    def step(t, carry):
        base = pl.multiple_of(t * PITCH, SUBLANE)

        def row(i, c):
            for slab in range(2):
                rows = pl.ds(base, n)
                s_old = s_ref[slab, i]
                sa = jnp.sum(s_old * kk_s[slab, rows, :], axis=0, keepdims=True)
                vv = v_s[slab, pl.ds(base + i, 1), :]
                s_new = (s_old * w_s[slab, rows, :] - sa * b_s[slab, rows, :]
                         + vv * k_s[slab, rows, :])
                s_ref[slab, i] = s_new
                y_s[slab, pl.ds(base + i, 1), :] = jnp.sum(s_new * r_s[slab, rows, :],
                                                           axis=0, keepdims=True)
            return c
        return lax.fori_loop(0, n, row, carry, unroll=2)

    lax.fori_loop(0, tq * SUBLANE, step, 0)

    for q in range(tq):
        for slab in range(2):
            lanes = pl.ds(slab * LANE, LANE)
            y = jnp.stack([y_s[slab, pl.ds(q * SUBLANE * PITCH + c, SUBLANE, stride=PITCH), :]
                           for c in range(n)], axis=0)
            mean = jnp.mean(y, axis=0, keepdims=True)
            yc = y - mean
            var = jnp.mean(yc * yc, axis=0, keepdims=True)
            y = yc * lax.rsqrt(var + GN_EPS) * gnw_ref[:, :, lanes] + gnb_ref[:, :, lanes]
            y = y + bv_s[q, :, :, lanes]
            g = g_ref[0, q, :, :, lanes]
            o_ref[q, :, :, lanes] = y * (g / (1.0 + jnp.exp(-g)))


def _scan(p4, dec4, a4, head_params, tb=SCAN_TB):
    _, tqs, n, _, lanes = p4.shape
    tq = min(tb // SUBLANE, tqs)
    tile = (n, SUBLANE, lanes)
    seg = lambda s: pl.BlockSpec((1, tq) + tile, lambda i, s=s: (s, i, 0, 0, 0))
    blk = pl.BlockSpec((tq,) + tile, lambda i: (i, 0, 0, 0))
    par = pl.BlockSpec(tile, lambda i: (0, 0, 0))
    steps = (2, tq * SUBLANE * PITCH, LANE)
    scratch = [pltpu.VMEM((2, n, n, LANE), jnp.float32),
               pltpu.VMEM((3,) + tile, jnp.float32),
               pltpu.VMEM((tq,) + tile, jnp.float32)]
    scratch += [pltpu.VMEM(steps, jnp.float32) for _ in range(7)]
    return pl.pallas_call(
        _scan_kernel,
        grid=(tqs // tq,),
        in_specs=[seg(0), seg(1), seg(2), seg(3), blk, blk] + [par] * 8,
        out_specs=blk,
        out_shape=jax.ShapeDtypeStruct((tqs,) + tile, jnp.float32),
        scratch_shapes=scratch,
        compiler_params=_params("arbitrary"),
        name="rwkv_scan",
    )(p4, p4, p4, p4, dec4, a4, *head_params)


def _outproj_lanes_kernel(y_ref, w_ref, r_ref, o_ref, lhs_ref):
    tq = y_ref.shape[0]
    tm = tq * SUBLANE
    nb, _, d = lhs_ref.shape

    @pl.when(pl.program_id(1) == 0)
    def _():
        low = lax.broadcasted_iota(jnp.int32, (tm, LANE), 1) < HEAD_DIM
        for c in range(HEAD_DIM // 2):
            cols = pl.ds(c * LANE, LANE)
            for bp in range(2):
                lanes = pl.ds(bp * LANE, LANE)
                e = y_ref[:, 2 * c, :, lanes].reshape(tm, LANE)
                o = y_ref[:, 2 * c + 1, :, lanes].reshape(tm, LANE)
                lhs_ref[2 * bp, :, cols] = jnp.where(
                    low, e, pltpu.roll(o, HEAD_DIM, axis=1)).astype(lhs_ref.dtype)
                lhs_ref[2 * bp + 1, :, cols] = jnp.where(
                    low, pltpu.roll(e, HEAD_DIM, axis=1), o).astype(lhs_ref.dtype)

    res = jnp.dot(lhs_ref[...].reshape(nb * tm, d), w_ref[...],
                  preferred_element_type=jnp.float32)
    o_ref[...] = r_ref[...] + res.reshape(o_ref.shape)


def _outproj_lanes(y4, w_perm, residual3, tm=128, tn=512):
    batch, seq, d = residual3.shape
    tq = tm // SUBLANE
    ospec = pl.BlockSpec((batch, tm, tn), lambda i, j: (0, i, j))
    return pl.pallas_call(
        _outproj_lanes_kernel,
        grid=(seq // tm, d // tn),
        in_specs=[pl.BlockSpec((tq, HEAD_DIM, SUBLANE, 2 * LANE), lambda i, j: (i, 0, 0, 0)),
                  pl.BlockSpec((d, tn), lambda i, j: (0, j)),
                  ospec],
        out_specs=ospec,
        out_shape=jax.ShapeDtypeStruct((batch, seq, d), jnp.float32),
        scratch_shapes=[pltpu.VMEM((batch, tm, d), jnp.bfloat16)],
        compiler_params=_params("parallel", "arbitrary"),
        name="rwkv_out_proj",
    )(y4, w_perm, residual3)


def _cols_nh(w):
    lead = w.shape[:-1]
    return w.reshape(lead + (-1, HEAD_DIM)).swapaxes(-1, -2).reshape(w.shape)


def _param_lanes(p, batch):
    pt = jnp.tile(p.reshape(-1, HEAD_DIM).T, (1, batch))
    return jnp.broadcast_to(pt[:, None, :], (HEAD_DIM, SUBLANE, pt.shape[1]))


def _rwkv_layer(h2d, batch, seq, norm_w, w_in, mu, w0, w2, a0, a2, k_k, k_a, r_k, gn_w,
                gn_b, w_out):
    d = h2d.shape[1]
    bf = jnp.bfloat16
    hn = _rms_norm(h2d, norm_w, bf)
    x_cols = slice(3 * d, 3 * d + 2 * LORA)
    w_main = jnp.concatenate([_cols_nh(w_in[:, i * d:(i + 1) * d]) for i in range(3)]
                             + [_cols_nh(w_in[:, 3 * d + 2 * LORA:])], axis=1).astype(bf)
    p4 = _inproj_lanes(hn.reshape(batch, seq, d), w_main)
    x = _matmul(hn, w_in[:, x_cols].astype(bf), tn=2 * LORA, name="rwkv_x_proj")
    dec4, a4 = _lora_lanes(x.reshape(batch, seq, 2 * LORA), mu[x_cols],
                           _cols_nh(w2).astype(bf), _cols_nh(a2).astype(bf),
                           _cols_nh(w0), _cols_nh(a0))
    head_params = [_param_lanes(x, batch) for x in
                   (mu[:d], mu[d:2 * d], mu[2 * d:3 * d], k_k, k_a, r_k.reshape(-1),
                    gn_w, gn_b)]
    y4 = _scan(p4, dec4, a4, head_params)
    w_out_perm = w_out.reshape(-1, HEAD_DIM, d).swapaxes(0, 1).reshape(d, d).astype(bf)
    out = _outproj_lanes(y4, w_out_perm, h2d.reshape(batch, seq, d))
    return out.reshape(batch * seq, d)


def _attn_kernel(sinks_ref, q_ref, kc_ref, kp_ref, vc_ref, vp_ref, g0_ref, g1_ref, g2_ref,
                 g3_ref, o_ref):
    bf = jnp.bfloat16
    n = HEAD_DIM
    not_first = pl.program_id(1) > 0
    kb = jnp.concatenate([kp_ref[...], kc_ref[...]], axis=0).astype(bf)
    vb = jnp.concatenate([vp_ref[...], vc_ref[...]], axis=0).astype(bf)
    qi = lax.broadcasted_iota(jnp.int32, (BLOCK, 2 * BLOCK), 0)
    kj = lax.broadcasted_iota(jnp.int32, (BLOCK, 2 * BLOCK), 1)
    delta = BLOCK + qi - kj
    mask = (delta >= 0) & (delta < BLOCK) & ((kj >= BLOCK) | not_first)
    gate_refs = (g0_ref, g1_ref, g2_ref, g3_ref)
    gate_w = g0_ref.shape[1]
    for hk in range(KV_HEADS):
        k_h = kb[:, hk * n:(hk + 1) * n]
        v_h = vb[:, hk * n:(hk + 1) * n]
        for pair in range(GROUP // 2):
            outs = []
            for g in (2 * pair, 2 * pair + 1):
                h = hk * GROUP + g
                q_h = q_ref[:, h * n:(h + 1) * n].astype(bf)
                s = lax.dot_general(q_h, k_h, (((1,), (1,)), ((), ())),
                                    preferred_element_type=jnp.float32) * (n ** -0.5)
                s = jnp.where(mask, s, MASK_VALUE)
                sink = sinks_ref[h]
                m = jnp.maximum(jnp.max(s, axis=-1, keepdims=True), sink)
                e = jnp.exp(s - m)
                denom = jnp.sum(e, axis=-1, keepdims=True) + jnp.exp(sink - m)
                prob = (e / denom).astype(bf)
                outs.append(jnp.dot(prob, v_h, preferred_element_type=jnp.float32))
            col = (hk * GROUP + 2 * pair) * n
            g_ref = gate_refs[col // gate_w]
            gate = g_ref[:, col % gate_w:col % gate_w + 2 * n]
            o = jnp.concatenate(outs, axis=1) * (gate / (1.0 + jnp.exp(-gate)))
            o_ref[:, col:col + 2 * n] = o.astype(o_ref.dtype)


def _attention(p, batch, seq, d, sinks):
    nb = seq // BLOCK
    kvw = KV_HEADS * HEAD_DIM
    gate_w = 2 * kvw
    row = lambda b, n: b * nb + n
    prow = lambda b, n: b * nb + jnp.maximum(n - 1, 0)
    kblk, vblk = d // kvw, d // kvw + 1
    gblk = (d + 2 * kvw) // gate_w
    in_specs = [pl.BlockSpec(memory_space=pltpu.SMEM),
                pl.BlockSpec((BLOCK, d), lambda b, n: (row(b, n), 0)),
                pl.BlockSpec((BLOCK, kvw), lambda b, n: (row(b, n), kblk)),
                pl.BlockSpec((BLOCK, kvw), lambda b, n: (prow(b, n), kblk)),
                pl.BlockSpec((BLOCK, kvw), lambda b, n: (row(b, n), vblk)),
                pl.BlockSpec((BLOCK, kvw), lambda b, n: (prow(b, n), vblk))]
    in_specs += [pl.BlockSpec((BLOCK, gate_w), lambda b, n, j=j: (row(b, n), gblk + j))
                 for j in range(4)]
    return pl.pallas_call(
        _attn_kernel,
        grid=(batch, nb),
        in_specs=in_specs,
        out_specs=pl.BlockSpec((BLOCK, d), lambda b, n: (row(b, n), 0)),
        out_shape=jax.ShapeDtypeStruct((batch * seq, d), jnp.bfloat16),
        compiler_params=_params("parallel", "parallel"),
        name="swa_attention",
    )(sinks, p, p, p, p, p, p, p, p, p)


def _attn_layer(h2d, batch, seq, norm_w, w_in, sinks, w_out):
    d = h2d.shape[1]
    bf = jnp.bfloat16
    hn = _rms_norm(h2d, norm_w, bf)
    p = _matmul(hn, w_in.astype(bf), tn=1024, name="attn_in_proj")
    o = _attention(p, batch, seq, d, sinks)
    return _matmul(o, w_out.astype(bf), residual=h2d, name="attn_out_proj")


def kernel(x, norm_w, final_norm_w, rwkv_w_in, rwkv_mu, rwkv_w0, rwkv_w2, rwkv_a0, rwkv_a2,
           rwkv_k_k, rwkv_k_a, rwkv_r_k, rwkv_gn_w, rwkv_gn_b, rwkv_w_out,
           attn_w_in, attn_sinks, attn_w_out):
    batch, seq, d = x.shape
    h = x.reshape(batch * seq, d)
    depth = norm_w.shape[0]
    for i in range(depth):
        j = i // 2
        if i % 2 == 0:
            h = _rwkv_layer(h, batch, seq, norm_w[i], rwkv_w_in[j], rwkv_mu[j], rwkv_w0[j],
                            rwkv_w2[j], rwkv_a0[j], rwkv_a2[j], rwkv_k_k[j], rwkv_k_a[j],
                            rwkv_r_k[j], rwkv_gn_w[j], rwkv_gn_b[j], rwkv_w_out[j])
        else:
            h = _attn_layer(h, batch, seq, norm_w[i], attn_w_in[j], attn_sinks[j],
                            attn_w_out[j])
    return _rms_norm(h, final_norm_w, jnp.float32).reshape(batch, seq, d)
```

```python
import functools

import jax
import jax.numpy as jnp
from jax import lax
from jax.experimental import pallas as pl
from jax.experimental.pallas import tpu as pltpu

HEAD_DIM = 64
LORA = 128
KV_HEADS = 8
GROUP = 8
BLOCK = 128
NORM_EPS = 1e-5
GN_EPS = HEAD_DIM * 1e-5
L2_EPS = 1e-12
MASK_VALUE = -1e30

LANE = 128
SUBLANE = 8
VMEM_LIMIT_BYTES = 56 * 1024 * 1024
SCAN_TB = 16
PITCH = HEAD_DIM + SUBLANE


def _params(*sem):
    return pltpu.CompilerParams(dimension_semantics=sem, vmem_limit_bytes=VMEM_LIMIT_BYTES)


def _rms_kernel(x_ref, g_ref, o_ref):
    x = x_ref[...]
    ms = jnp.mean(x * x, axis=-1, keepdims=True)
    o_ref[...] = (x * lax.rsqrt(ms + NORM_EPS) * g_ref[...]).astype(o_ref.dtype)


def _rms_norm(x2d, g, out_dtype, tm=256):
    m, d = x2d.shape
    return pl.pallas_call(
        _rms_kernel,
        grid=(m // tm,),
        in_specs=[pl.BlockSpec((tm, d), lambda i: (i, 0)),
                  pl.BlockSpec((1, d), lambda i: (0, 0))],
        out_specs=pl.BlockSpec((tm, d), lambda i: (i, 0)),
        out_shape=jax.ShapeDtypeStruct((m, d), out_dtype),
        compiler_params=_params("parallel"),
        name="rmsnorm",
    )(x2d, g.reshape(1, d))


def _mm_kernel(a_ref, b_ref, o_ref):
    o_ref[...] = jnp.dot(a_ref[...], b_ref[...], preferred_element_type=jnp.float32)


def _mm_res_kernel(a_ref, b_ref, r_ref, o_ref):
    o_ref[...] = r_ref[...] + jnp.dot(a_ref[...], b_ref[...],
                                      preferred_element_type=jnp.float32)


def _matmul(a, b, residual=None, tm=512, tn=1024, name="matmul"):
    m, k = a.shape
    _, n = b.shape
    tm = min(tm, m)
    grid = (n // tn, m // tm)
    in_specs = [pl.BlockSpec((tm, k), lambda j, i: (i, 0)),
                pl.BlockSpec((k, tn), lambda j, i: (0, j))]
    args = [a, b]
    kern = _mm_kernel
    if residual is not None:
        in_specs.append(pl.BlockSpec((tm, tn), lambda j, i: (i, j)))
        args.append(residual)
        kern = _mm_res_kernel
    return pl.pallas_call(
        kern,
        grid=grid,
        in_specs=in_specs,
        out_specs=pl.BlockSpec((tm, tn), lambda j, i: (i, j)),
        out_shape=jax.ShapeDtypeStruct((m, n), jnp.float32),
        compiler_params=_params("parallel", "parallel"),
        name=name,
    )(*args)


def _store_lanes(o_ref, lead, res, tm):
    low = lax.broadcasted_iota(jnp.int32, (tm, LANE), 1) < HEAD_DIM
    for bp in range(2):
        ra = res[(2 * bp) * tm:(2 * bp + 1) * tm]
        rb = res[(2 * bp + 1) * tm:(2 * bp + 2) * tm]
        for c in range(res.shape[1] // LANE):
            a = ra[:, c * LANE:(c + 1) * LANE]
            b = rb[:, c * LANE:(c + 1) * LANE]
            even = jnp.where(low, a, pltpu.roll(b, HEAD_DIM, axis=1))
            odd = jnp.where(low, pltpu.roll(a, HEAD_DIM, axis=1), b)
            lanes = pl.ds(bp * LANE, LANE)
            shape = (tm // SUBLANE, SUBLANE, LANE)
            o_ref[lead + (slice(None), 2 * c, slice(None), lanes)] = even.reshape(shape)
            o_ref[lead + (slice(None), 2 * c + 1, slice(None), lanes)] = odd.reshape(shape)


def _inproj_lanes_kernel(x_ref, w_ref, o_ref):
    nb, tm, d = x_ref.shape
    res = jnp.dot(x_ref[...].reshape(nb * tm, d), w_ref[...],
                  preferred_element_type=jnp.float32)
    _store_lanes(o_ref, (0,), res, tm)


def _inproj_lanes(hn3, w_perm, tm=256, tn=512):
    batch, seq, d = hn3.shape
    assert batch == 4 and d == 64 * HEAD_DIM
    segs = w_perm.shape[1] // d
    nv = tn // HEAD_DIM
    per_seg = d // tn
    return pl.pallas_call(
        _inproj_lanes_kernel,
        grid=(seq // tm, segs * per_seg),
        in_specs=[pl.BlockSpec((batch, tm, d), lambda i, j: (0, i, 0)),
                  pl.BlockSpec((d, tn), lambda i, j: (0, j))],
        out_specs=pl.BlockSpec((1, tm // SUBLANE, nv, SUBLANE, 2 * LANE),
                               lambda i, j: (j // per_seg, i, j % per_seg, 0, 0)),
        out_shape=jax.ShapeDtypeStruct((segs, seq // SUBLANE, HEAD_DIM, SUBLANE, 2 * LANE),
                                       jnp.float32),
        compiler_params=_params("parallel", "arbitrary"),
        name="rwkv_in_proj",
    )(hn3, w_perm)


def _lora_kernel(x_ref, xp_ref, mu_ref, w2_ref, a2_ref, w0_ref, a0_ref, dec_ref, a_ref):
    nb, tm, _ = x_ref.shape
    first = pl.program_id(0) == 0
    row0 = lax.broadcasted_iota(jnp.int32, (tm, 2 * LORA), 0) == 0
    xw, xa = [], []
    for b in range(nb):
        x = x_ref[b]
        last = jnp.where(first, 0.0, xp_ref[b][SUBLANE - 1:SUBLANE])
        before = jnp.where(row0, last, pltpu.roll(x, 1, axis=0))
        x = x + (before - x) * mu_ref[...]
        xw.append(jnp.tanh(x[:, :LORA]).astype(jnp.bfloat16))
        xa.append(x[:, LORA:].astype(jnp.bfloat16))
    wl = w0_ref[...] + jnp.dot(jnp.concatenate(xw, axis=0), w2_ref[...],
                               preferred_element_type=jnp.float32)
    al = a0_ref[...] + jnp.dot(jnp.concatenate(xa, axis=0), a2_ref[...],
                               preferred_element_type=jnp.float32)
    z = -wl
    softplus = jnp.maximum(z, 0.0) + jnp.log(1.0 + jnp.exp(-jnp.abs(z)))
    _store_lanes(dec_ref, (), jnp.exp(-jnp.exp(-softplus - 0.5)), tm)
    _store_lanes(a_ref, (), 1.0 / (1.0 + jnp.exp(-al)), tm)


def _lora_lanes(x3, mu_x, w2p, a2p, w0p, a0p, tm=256, tn=512):
    batch, seq, _ = x3.shape
    d = w2p.shape[1]
    nv = tn // HEAD_DIM
    out = jax.ShapeDtypeStruct((seq // SUBLANE, HEAD_DIM, SUBLANE, 2 * LANE), jnp.float32)
    ospec = pl.BlockSpec((tm // SUBLANE, nv, SUBLANE, 2 * LANE), lambda i, j: (i, j, 0, 0))
    tq = tm // SUBLANE
    return pl.pallas_call(
        _lora_kernel,
        grid=(seq // tm, d // tn),
        in_specs=[pl.BlockSpec((batch, tm, 2 * LORA), lambda i, j: (0, i, 0)),
                  pl.BlockSpec((batch, SUBLANE, 2 * LORA),
                               lambda i, j: (0, jnp.maximum(i * tq - 1, 0), 0)),
                  pl.BlockSpec((1, 2 * LORA), lambda i, j: (0, 0)),
                  pl.BlockSpec((LORA, tn), lambda i, j: (0, j)),
                  pl.BlockSpec((LORA, tn), lambda i, j: (0, j)),
                  pl.BlockSpec((1, tn), lambda i, j: (0, j)),
                  pl.BlockSpec((1, tn), lambda i, j: (0, j))],
        out_specs=[ospec, ospec],
        out_shape=[out, out],
        compiler_params=_params("parallel", "arbitrary"),
        name="rwkv_lora",
    )(x3, x3, mu_x.reshape(1, -1), w2p, a2p, w0p.reshape(1, -1), a0p.reshape(1, -1))


def _scan_kernel(r_ref, k_ref, v_ref, g_ref, w_ref, a_ref,
                 mur_ref, muk_ref, muv_ref, kkp_ref, kap_ref, rkp_ref, gnw_ref, gnb_ref,
                 o_ref,
                 s_ref, last_ref, bv_s, r_s, w_s, k_s, v_s, kk_s, b_s, y_s):
    tq = w_ref.shape[0]
    n = HEAD_DIM
    first = pl.program_id(0) == 0

    @pl.when(first)
    def _():
        s_ref[...] = jnp.zeros_like(s_ref)
        last_ref[...] = jnp.zeros_like(last_ref)

    sub0 = lax.broadcasted_iota(jnp.int32, (n, SUBLANE, 2 * LANE), 1) == 0

    def lerp(x_ref, idx, mu_ref, q):
        cur = x_ref[0, q]
        prev_tile = last_ref[idx] if q == 0 else x_ref[0, q - 1]
        before = jnp.where(sub0, pltpu.roll(prev_tile, 1, axis=1), pltpu.roll(cur, 1, axis=1))
        return cur + (before - cur) * mu_ref[...]

    def to_steps(dst, q, val):
        for c in range(n):
            for slab in range(2):
                dst[slab, pl.ds(q * SUBLANE * PITCH + c, SUBLANE, stride=PITCH), :] = (
                    val[c, :, slab * LANE:(slab + 1) * LANE])

    for q in range(tq):
        r = lerp(r_ref, 0, mur_ref, q)
        k = lerp(k_ref, 1, muk_ref, q)
        v = lerp(v_ref, 2, muv_ref, q)
        a = a_ref[q]
        kkr = k * kkp_ref[...]
        nrm = jnp.sqrt(jnp.sum(kkr * kkr, axis=0, keepdims=True))
        kk = kkr / jnp.maximum(nrm, L2_EPS)
        k2 = k * (1.0 + (a - 1.0) * kap_ref[...])
        bv_s[q] = jnp.sum(r * k2 * rkp_ref[...], axis=0, keepdims=True) * v
        to_steps(r_s, q, r)
        to_steps(w_s, q, w_ref[q])
        to_steps(k_s, q, k2)
        to_steps(v_s, q, v)
        to_steps(kk_s, q, kk)
        to_steps(b_s, q, kk * a)
    for idx, x_ref in enumerate((r_ref, k_ref, v_ref)):
        last_ref[idx] = x_ref[0, tq - 1]

    tb = tq * SUBLANE
    zero = jnp.zeros((n, LANE), jnp.float32)

    def row(ref, slab, i):
        return ref[slab, pl.ds(i, 1), :]

    for slab in range(2):
        def sa_first(c, acc, slab=slab):
            return acc + s_ref[slab, c] * row(kk_s, slab, c)

        def step(t, sa, slab=slab):
            base = pl.multiple_of(t * PITCH, SUBLANE)
            ahead = pl.multiple_of(jnp.minimum(t + 1, tb - 1) * PITCH, SUBLANE)
            vv = v_s[slab, pl.ds(base, n), :]

            def channel(c, carry):
                y_acc, sa_next = carry
                s_new = (s_ref[slab, c] * row(w_s, slab, base + c) - sa * row(b_s, slab, base + c)
                         + vv * row(k_s, slab, base + c))
                s_ref[slab, c] = s_new
                return (y_acc + s_new * row(r_s, slab, base + c),
                        sa_next + s_new * row(kk_s, slab, ahead + c))

            y, sa_next = lax.fori_loop(0, n, channel, (zero, zero), unroll=8)
            y_s[slab, pl.ds(base, n), :] = y
            return sa_next

        lax.fori_loop(0, tb, step, lax.fori_loop(0, n, sa_first, zero, unroll=8))

    for q in range(tq):
        for slab in range(2):
            lanes = pl.ds(slab * LANE, LANE)
            y = jnp.stack([y_s[slab, pl.ds(q * SUBLANE * PITCH + c, SUBLANE, stride=PITCH), :]
                           for c in range(n)], axis=0)
            mean = jnp.mean(y, axis=0, keepdims=True)
            yc = y - mean
            var = jnp.mean(yc * yc, axis=0, keepdims=True)
            y = yc * lax.rsqrt(var + GN_EPS) * gnw_ref[:, :, lanes] + gnb_ref[:, :, lanes]
            y = y + bv_s[q, :, :, lanes]
            g = g_ref[0, q, :, :, lanes]
            o_ref[q, :, :, lanes] = y * (g / (1.0 + jnp.exp(-g)))


def _scan(p4, dec4, a4, head_params, tb=SCAN_TB):
    _, tqs, n, _, lanes = p4.shape
    tq = min(tb // SUBLANE, tqs)
    tile = (n, SUBLANE, lanes)
    seg = lambda s: pl.BlockSpec((1, tq) + tile, lambda i, s=s: (s, i, 0, 0, 0))
    blk = pl.BlockSpec((tq,) + tile, lambda i: (i, 0, 0, 0))
    par = pl.BlockSpec(tile, lambda i: (0, 0, 0))
    steps = (2, tq * SUBLANE * PITCH, LANE)
    scratch = [pltpu.VMEM((2, n, n, LANE), jnp.float32),
               pltpu.VMEM((3,) + tile, jnp.float32),
               pltpu.VMEM((tq,) + tile, jnp.float32)]
    scratch += [pltpu.VMEM(steps, jnp.float32) for _ in range(7)]
    return pl.pallas_call(
        _scan_kernel,
        grid=(tqs // tq,),
        in_specs=[seg(0), seg(1), seg(2), seg(3), blk, blk] + [par] * 8,
        out_specs=blk,
        out_shape=jax.ShapeDtypeStruct((tqs,) + tile, jnp.float32),
        scratch_shapes=scratch,
        compiler_params=_params("arbitrary"),
        name="rwkv_scan",
    )(p4, p4, p4, p4, dec4, a4, *head_params)


def _outproj_lanes_kernel(y_ref, w_ref, r_ref, o_ref, lhs_ref):
    tq = y_ref.shape[0]
    tm = tq * SUBLANE
    nb, _, d = lhs_ref.shape

    @pl.when(pl.program_id(1) == 0)
    def _():
        low = lax.broadcasted_iota(jnp.int32, (tm, LANE), 1) < HEAD_DIM
        for c in range(HEAD_DIM // 2):
            cols = pl.ds(c * LANE, LANE)
            for bp in range(2):
                lanes = pl.ds(bp * LANE, LANE)
                e = y_ref[:, 2 * c, :, lanes].reshape(tm, LANE)
                o = y_ref[:, 2 * c + 1, :, lanes].reshape(tm, LANE)
                lhs_ref[2 * bp, :, cols] = jnp.where(
                    low, e, pltpu.roll(o, HEAD_DIM, axis=1)).astype(lhs_ref.dtype)
                lhs_ref[2 * bp + 1, :, cols] = jnp.where(
                    low, pltpu.roll(e, HEAD_DIM, axis=1), o).astype(lhs_ref.dtype)

    res = jnp.dot(lhs_ref[...].reshape(nb * tm, d), w_ref[...],
                  preferred_element_type=jnp.float32)
    o_ref[...] = r_ref[...] + res.reshape(o_ref.shape)


def _outproj_lanes(y4, w_perm, residual3, tm=128, tn=512):
    batch, seq, d = residual3.shape
    tq = tm // SUBLANE
    ospec = pl.BlockSpec((batch, tm, tn), lambda i, j: (0, i, j))
    return pl.pallas_call(
        _outproj_lanes_kernel,
        grid=(seq // tm, d // tn),
        in_specs=[pl.BlockSpec((tq, HEAD_DIM, SUBLANE, 2 * LANE), lambda i, j: (i, 0, 0, 0)),
                  pl.BlockSpec((d, tn), lambda i, j: (0, j)),
                  ospec],
        out_specs=ospec,
        out_shape=jax.ShapeDtypeStruct((batch, seq, d), jnp.float32),
        scratch_shapes=[pltpu.VMEM((batch, tm, d), jnp.bfloat16)],
        compiler_params=_params("parallel", "arbitrary"),
        name="rwkv_out_proj",
    )(y4, w_perm, residual3)


def _cols_nh(w):
    lead = w.shape[:-1]
    return w.reshape(lead + (-1, HEAD_DIM)).swapaxes(-1, -2).reshape(w.shape)


def _param_lanes(p, batch):
    pt = jnp.tile(p.reshape(-1, HEAD_DIM).T, (1, batch))
    return jnp.broadcast_to(pt[:, None, :], (HEAD_DIM, SUBLANE, pt.shape[1]))


def _rwkv_layer(h2d, batch, seq, norm_w, w_in, mu, w0, w2, a0, a2, k_k, k_a, r_k, gn_w,
                gn_b, w_out):
    d = h2d.shape[1]
    bf = jnp.bfloat16
    hn = _rms_norm(h2d, norm_w, bf)
    x_cols = slice(3 * d, 3 * d + 2 * LORA)
    w_main = jnp.concatenate([_cols_nh(w_in[:, i * d:(i + 1) * d]) for i in range(3)]
                             + [_cols_nh(w_in[:, 3 * d + 2 * LORA:])], axis=1).astype(bf)
    p4 = _inproj_lanes(hn.reshape(batch, seq, d), w_main)
    x = _matmul(hn, w_in[:, x_cols].astype(bf), tn=2 * LORA, name="rwkv_x_proj")
    dec4, a4 = _lora_lanes(x.reshape(batch, seq, 2 * LORA), mu[x_cols],
                           _cols_nh(w2).astype(bf), _cols_nh(a2).astype(bf),
                           _cols_nh(w0), _cols_nh(a0))
    head_params = [_param_lanes(x, batch) for x in
                   (mu[:d], mu[d:2 * d], mu[2 * d:3 * d], k_k, k_a, r_k.reshape(-1),
                    gn_w, gn_b)]
    y4 = _scan(p4, dec4, a4, head_params)
    w_out_perm = w_out.reshape(-1, HEAD_DIM, d).swapaxes(0, 1).reshape(d, d).astype(bf)
    out = _outproj_lanes(y4, w_out_perm, h2d.reshape(batch, seq, d))
    return out.reshape(batch * seq, d)


def _attn_kernel(sinks_ref, q_ref, kc_ref, kp_ref, vc_ref, vp_ref, g0_ref, g1_ref, g2_ref,
                 g3_ref, o_ref):
    bf = jnp.bfloat16
    n = HEAD_DIM
    not_first = pl.program_id(1) > 0
    kb = jnp.concatenate([kp_ref[...], kc_ref[...]], axis=0).astype(bf)
    vb = jnp.concatenate([vp_ref[...], vc_ref[...]], axis=0).astype(bf)
    qi = lax.broadcasted_iota(jnp.int32, (BLOCK, 2 * BLOCK), 0)
    kj = lax.broadcasted_iota(jnp.int32, (BLOCK, 2 * BLOCK), 1)
    delta = BLOCK + qi - kj
    mask = (delta >= 0) & (delta < BLOCK) & ((kj >= BLOCK) | not_first)
    gate_refs = (g0_ref, g1_ref, g2_ref, g3_ref)
    gate_w = g0_ref.shape[1]
    for hk in range(KV_HEADS):
        k_h = kb[:, hk * n:(hk + 1) * n]
        v_h = vb[:, hk * n:(hk + 1) * n]
        for pair in range(GROUP // 2):
            outs = []
            for g in (2 * pair, 2 * pair + 1):
                h = hk * GROUP + g
                q_h = q_ref[:, h * n:(h + 1) * n].astype(bf)
                s = lax.dot_general(q_h, k_h, (((1,), (1,)), ((), ())),
                                    preferred_element_type=jnp.float32) * (n ** -0.5)
                s = jnp.where(mask, s, MASK_VALUE)
                sink = sinks_ref[h]
                m = jnp.maximum(jnp.max(s, axis=-1, keepdims=True), sink)
                e = jnp.exp(s - m)
                denom = jnp.sum(e, axis=-1, keepdims=True) + jnp.exp(sink - m)
                prob = (e / denom).astype(bf)
                outs.append(jnp.dot(prob, v_h, preferred_element_type=jnp.float32))
            col = (hk * GROUP + 2 * pair) * n
            g_ref = gate_refs[col // gate_w]
            gate = g_ref[:, col % gate_w:col % gate_w + 2 * n]
            o = jnp.concatenate(outs, axis=1) * (gate / (1.0 + jnp.exp(-gate)))
            o_ref[:, col:col + 2 * n] = o.astype(o_ref.dtype)


def _attention(p, batch, seq, d, sinks):
    nb = seq // BLOCK
    kvw = KV_HEADS * HEAD_DIM
    gate_w = 2 * kvw
    row = lambda b, n: b * nb + n
    prow = lambda b, n: b * nb + jnp.maximum(n - 1, 0)
    kblk, vblk = d // kvw, d // kvw + 1
    gblk = (d + 2 * kvw) // gate_w
    in_specs = [pl.BlockSpec(memory_space=pltpu.SMEM),
                pl.BlockSpec((BLOCK, d), lambda b, n: (row(b, n), 0)),
                pl.BlockSpec((BLOCK, kvw), lambda b, n: (row(b, n), kblk)),
                pl.BlockSpec((BLOCK, kvw), lambda b, n: (prow(b, n), kblk)),
                pl.BlockSpec((BLOCK, kvw), lambda b, n: (row(b, n), vblk)),
                pl.BlockSpec((BLOCK, kvw), lambda b, n: (prow(b, n), vblk))]
    in_specs += [pl.BlockSpec((BLOCK, gate_w), lambda b, n, j=j: (row(b, n), gblk + j))
                 for j in range(4)]
    return pl.pallas_call(
        _attn_kernel,
        grid=(batch, nb),
        in_specs=in_specs,
        out_specs=pl.BlockSpec((BLOCK, d), lambda b, n: (row(b, n), 0)),
        out_shape=jax.ShapeDtypeStruct((batch * seq, d), jnp.bfloat16),
        compiler_params=_params("parallel", "parallel"),
        name="swa_attention",
    )(sinks, p, p, p, p, p, p, p, p, p)


def _attn_layer(h2d, batch, seq, norm_w, w_in, sinks, w_out):
    d = h2d.shape[1]
    bf = jnp.bfloat16
    hn = _rms_norm(h2d, norm_w, bf)
    p = _matmul(hn, w_in.astype(bf), tn=1024, name="attn_in_proj")
    o = _attention(p, batch, seq, d, sinks)
    return _matmul(o, w_out.astype(bf), residual=h2d, name="attn_out_proj")


def kernel(x, norm_w, final_norm_w, rwkv_w_in, rwkv_mu, rwkv_w0, rwkv_w2, rwkv_a0, rwkv_a2,
           rwkv_k_k, rwkv_k_a, rwkv_r_k, rwkv_gn_w, rwkv_gn_b, rwkv_w_out,
           attn_w_in, attn_sinks, attn_w_out):
    batch, seq, d = x.shape
    h = x.reshape(batch * seq, d)
    depth = norm_w.shape[0]
    for i in range(depth):
        j = i // 2
        if i % 2 == 0:
            h = _rwkv_layer(h, batch, seq, norm_w[i], rwkv_w_in[j], rwkv_mu[j], rwkv_w0[j],
                            rwkv_w2[j], rwkv_a0[j], rwkv_a2[j], rwkv_k_k[j], rwkv_k_a[j],
                            rwkv_r_k[j], rwkv_gn_w[j], rwkv_gn_b[j], rwkv_w_out[j])
        else:
            h = _attn_layer(h, batch, seq, norm_w[i], attn_w_in[j], attn_sinks[j],
                            attn_w_out[j])
    return _rms_norm(h, final_norm_w, jnp.float32).reshape(batch, seq, d)
```

```python
import functools

import jax
import jax.numpy as jnp
from jax import lax
from jax.experimental import pallas as pl
from jax.experimental.pallas import tpu as pltpu

HEAD_DIM = 64
LORA = 128
KV_HEADS = 8
GROUP = 8
BLOCK = 128
NORM_EPS = 1e-5
GN_EPS = HEAD_DIM * 1e-5
L2_EPS = 1e-12
MASK_VALUE = -1e30

LANE = 128
SUBLANE = 8
VMEM_LIMIT_BYTES = 56 * 1024 * 1024
SCAN_TB = 16
PITCH = HEAD_DIM + SUBLANE
ROW_PITCH = HEAD_DIM + 4


def _params(*sem):
    return pltpu.CompilerParams(dimension_semantics=sem, vmem_limit_bytes=VMEM_LIMIT_BYTES)


def _rms_kernel(x_ref, g_ref, o_ref):
    x = x_ref[...]
    ms = jnp.mean(x * x, axis=-1, keepdims=True)
    o_ref[...] = (x * lax.rsqrt(ms + NORM_EPS) * g_ref[...]).astype(o_ref.dtype)


def _rms_norm(x2d, g, out_dtype, tm=256):
    m, d = x2d.shape
    return pl.pallas_call(
        _rms_kernel,
        grid=(m // tm,),
        in_specs=[pl.BlockSpec((tm, d), lambda i: (i, 0)),
                  pl.BlockSpec((1, d), lambda i: (0, 0))],
        out_specs=pl.BlockSpec((tm, d), lambda i: (i, 0)),
        out_shape=jax.ShapeDtypeStruct((m, d), out_dtype),
        compiler_params=_params("parallel"),
        name="rmsnorm",
    )(x2d, g.reshape(1, d))


def _mm_kernel(a_ref, b_ref, o_ref):
    o_ref[...] = jnp.dot(a_ref[...], b_ref[...], preferred_element_type=jnp.float32)


def _mm_res_kernel(a_ref, b_ref, r_ref, o_ref):
    o_ref[...] = r_ref[...] + jnp.dot(a_ref[...], b_ref[...],
                                      preferred_element_type=jnp.float32)


def _matmul(a, b, residual=None, tm=512, tn=1024, name="matmul"):
    m, k = a.shape
    _, n = b.shape
    tm = min(tm, m)
    grid = (n // tn, m // tm)
    in_specs = [pl.BlockSpec((tm, k), lambda j, i: (i, 0)),
                pl.BlockSpec((k, tn), lambda j, i: (0, j))]
    args = [a, b]
    kern = _mm_kernel
    if residual is not None:
        in_specs.append(pl.BlockSpec((tm, tn), lambda j, i: (i, j)))
        args.append(residual)
        kern = _mm_res_kernel
    return pl.pallas_call(
        kern,
        grid=grid,
        in_specs=in_specs,
        out_specs=pl.BlockSpec((tm, tn), lambda j, i: (i, j)),
        out_shape=jax.ShapeDtypeStruct((m, n), jnp.float32),
        compiler_params=_params("parallel", "parallel"),
        name=name,
    )(*args)


def _store_lanes(o_ref, lead, res, tm):
    low = lax.broadcasted_iota(jnp.int32, (tm, LANE), 1) < HEAD_DIM
    for bp in range(2):
        ra = res[(2 * bp) * tm:(2 * bp + 1) * tm]
        rb = res[(2 * bp + 1) * tm:(2 * bp + 2) * tm]
        for c in range(res.shape[1] // LANE):
            a = ra[:, c * LANE:(c + 1) * LANE]
            b = rb[:, c * LANE:(c + 1) * LANE]
            even = jnp.where(low, a, pltpu.roll(b, HEAD_DIM, axis=1))
            odd = jnp.where(low, pltpu.roll(a, HEAD_DIM, axis=1), b)
            lanes = pl.ds(bp * LANE, LANE)
            shape = (tm // SUBLANE, SUBLANE, LANE)
            o_ref[lead + (slice(None), 2 * c, slice(None), lanes)] = even.reshape(shape)
            o_ref[lead + (slice(None), 2 * c + 1, slice(None), lanes)] = odd.reshape(shape)


def _inproj_lanes_kernel(x_ref, w_ref, o_ref):
    nb, tm, d = x_ref.shape
    res = jnp.dot(x_ref[...].reshape(nb * tm, d), w_ref[...],
                  preferred_element_type=jnp.float32)
    _store_lanes(o_ref, (0,), res, tm)


def _inproj_lanes(hn3, w_perm, tm=256, tn=512):
    batch, seq, d = hn3.shape
    assert batch == 4 and d == 64 * HEAD_DIM
    segs = w_perm.shape[1] // d
    nv = tn // HEAD_DIM
    per_seg = d // tn
    return pl.pallas_call(
        _inproj_lanes_kernel,
        grid=(seq // tm, segs * per_seg),
        in_specs=[pl.BlockSpec((batch, tm, d), lambda i, j: (0, i, 0)),
                  pl.BlockSpec((d, tn), lambda i, j: (0, j))],
        out_specs=pl.BlockSpec((1, tm // SUBLANE, nv, SUBLANE, 2 * LANE),
                               lambda i, j: (j // per_seg, i, j % per_seg, 0, 0)),
        out_shape=jax.ShapeDtypeStruct((segs, seq // SUBLANE, HEAD_DIM, SUBLANE, 2 * LANE),
                                       jnp.float32),
        compiler_params=_params("parallel", "arbitrary"),
        name="rwkv_in_proj",
    )(hn3, w_perm)


def _lora_kernel(x_ref, xp_ref, mu_ref, w2_ref, a2_ref, w0_ref, a0_ref, dec_ref, a_ref):
    nb, tm, _ = x_ref.shape
    first = pl.program_id(0) == 0
    row0 = lax.broadcasted_iota(jnp.int32, (tm, 2 * LORA), 0) == 0
    xw, xa = [], []
    for b in range(nb):
        x = x_ref[b]
        last = jnp.where(first, 0.0, xp_ref[b][SUBLANE - 1:SUBLANE])
        before = jnp.where(row0, last, pltpu.roll(x, 1, axis=0))
        x = x + (before - x) * mu_ref[...]
        xw.append(jnp.tanh(x[:, :LORA]).astype(jnp.bfloat16))
        xa.append(x[:, LORA:].astype(jnp.bfloat16))
    wl = w0_ref[...] + jnp.dot(jnp.concatenate(xw, axis=0), w2_ref[...],
                               preferred_element_type=jnp.float32)
    al = a0_ref[...] + jnp.dot(jnp.concatenate(xa, axis=0), a2_ref[...],
                               preferred_element_type=jnp.float32)
    z = -wl
    softplus = jnp.maximum(z, 0.0) + jnp.log(1.0 + jnp.exp(-jnp.abs(z)))
    _store_lanes(dec_ref, (), jnp.exp(-jnp.exp(-softplus - 0.5)), tm)
    _store_lanes(a_ref, (), 1.0 / (1.0 + jnp.exp(-al)), tm)


def _lora_lanes(x3, mu_x, w2p, a2p, w0p, a0p, tm=256, tn=512):
    batch, seq, _ = x3.shape
    d = w2p.shape[1]
    nv = tn // HEAD_DIM
    out = jax.ShapeDtypeStruct((seq // SUBLANE, HEAD_DIM, SUBLANE, 2 * LANE), jnp.float32)
    ospec = pl.BlockSpec((tm // SUBLANE, nv, SUBLANE, 2 * LANE), lambda i, j: (i, j, 0, 0))
    tq = tm // SUBLANE
    return pl.pallas_call(
        _lora_kernel,
        grid=(seq // tm, d // tn),
        in_specs=[pl.BlockSpec((batch, tm, 2 * LORA), lambda i, j: (0, i, 0)),
                  pl.BlockSpec((batch, SUBLANE, 2 * LORA),
                               lambda i, j: (0, jnp.maximum(i * tq - 1, 0), 0)),
                  pl.BlockSpec((1, 2 * LORA), lambda i, j: (0, 0)),
                  pl.BlockSpec((LORA, tn), lambda i, j: (0, j)),
                  pl.BlockSpec((LORA, tn), lambda i, j: (0, j)),
                  pl.BlockSpec((1, tn), lambda i, j: (0, j)),
                  pl.BlockSpec((1, tn), lambda i, j: (0, j))],
        out_specs=[ospec, ospec],
        out_shape=[out, out],
        compiler_params=_params("parallel", "arbitrary"),
        name="rwkv_lora",
    )(x3, x3, mu_x.reshape(1, -1), w2p, a2p, w0p.reshape(1, -1), a0p.reshape(1, -1))


def _scan_kernel(r_ref, k_ref, v_ref, g_ref, w_ref, a_ref,
                 mur_ref, muk_ref, muv_ref, kkp_ref, kap_ref, rkp_ref, gnw_ref, gnb_ref,
                 o_ref,
                 s_ref, last_ref, gcar_ref, bv_s, r_s, gam_s, k_s, kk_s, b_s, v_s, y_s):
    tq = w_ref.shape[0]
    tb = tq * SUBLANE
    n = HEAD_DIM
    ch = SUBLANE
    first = pl.program_id(0) == 0

    @pl.when(first)
    def _():
        s_ref[...] = jnp.zeros_like(s_ref)
        last_ref[...] = jnp.zeros_like(last_ref)

    sub = lax.broadcasted_iota(jnp.int32, (ch, SUBLANE, 2 * LANE), 1)

    def shifted(x, before_first):
        return jnp.where(sub == 0, before_first, pltpu.roll(x, 1, axis=1))

    def lerp(x_ref, idx, mu_ref, q, cs):
        cur = x_ref[0, q, cs]
        prev_tile = last_ref[idx, cs] if q == 0 else x_ref[0, q - 1, cs]
        before = shifted(cur, pltpu.roll(prev_tile, 1, axis=1))
        return cur + (before - cur) * mu_ref[cs]

    def to_steps(dst, pitch, q, c0, val):
        for c in range(ch):
            for slab in range(2):
                dst[slab, pl.ds(q * SUBLANE * pitch + c0 + c, SUBLANE, stride=pitch), :] = (
                    val[c, :, slab * LANE:(slab + 1) * LANE])

    def k_terms(q, cs):
        k = lerp(k_ref, 1, muk_ref, q, cs)
        a = a_ref[q, cs]
        return k * kkp_ref[cs], k * (1.0 + (a - 1.0) * kap_ref[cs]), a

    for q in range(tq):
        def sums(j, carry, q=q):
            ssq, bon = carry
            cs = pl.ds(pl.multiple_of(j * ch, ch), ch)
            kkr, k2, _ = k_terms(q, cs)
            r = lerp(r_ref, 0, mur_ref, q, cs)
            return ssq + kkr * kkr, bon + r * k2 * rkp_ref[cs]

        zeros = jnp.zeros((ch, SUBLANE, 2 * LANE), jnp.float32)
        ssq, bon = lax.fori_loop(0, n // ch, sums, (zeros, zeros))
        nrm = jnp.sqrt(jnp.sum(ssq, axis=0, keepdims=True))
        inv_nrm = 1.0 / jnp.maximum(nrm, L2_EPS)
        bonus = jnp.sum(bon, axis=0, keepdims=True)

        def emit(j, carry, q=q, inv_nrm=inv_nrm, bonus=bonus):
            c0 = pl.multiple_of(j * ch, ch)
            cs = pl.ds(c0, ch)
            kkr, k2, a = k_terms(q, cs)
            r = lerp(r_ref, 0, mur_ref, q, cs)
            v = lerp(v_ref, 2, muv_ref, q, cs)
            kk = kkr * inv_nrm
            gam = w_ref[q, cs]
            for sh in (1, 2, 4):
                gam = gam * jnp.where(sub >= sh, pltpu.roll(gam, sh, axis=1), 1.0)
            before_tile = 1.0 if q == 0 else gcar_ref[cs]
            gam = gam * before_tile
            gam_prev = shifted(gam, before_tile)
            inv_gam = 1.0 / gam
            if q + 1 < tq:
                gcar_ref[cs] = jnp.broadcast_to(gam[:, SUBLANE - 1:SUBLANE, :], gam.shape)
            bv_s[q, cs] = bonus * v
            to_steps(r_s, ROW_PITCH, q, c0, r * gam)
            to_steps(gam_s, ROW_PITCH, q, c0, gam)
            to_steps(k_s, ROW_PITCH, q, c0, k2 * inv_gam)
            to_steps(kk_s, ROW_PITCH, q, c0, kk * gam_prev)
            to_steps(b_s, ROW_PITCH, q, c0, kk * a * inv_gam)
            to_steps(v_s, PITCH, q, c0, v)
            return carry

        lax.fori_loop(0, n // ch, emit, 0)
    for idx, x_ref in enumerate((r_ref, k_ref, v_ref)):
        last_ref[idx] = x_ref[0, tq - 1]

    zero = jnp.zeros((n, LANE), jnp.float32)

    def row(ref, slab, i):
        return ref[slab, pl.ds(i, 1), :]

    for slab in range(2):
        def sa_first(c, acc, slab=slab):
            return acc + s_ref[slab, c] * row(kk_s, slab, c)

        def step(t, sa, slab=slab):
            base = t * ROW_PITCH
            ahead = jnp.minimum(t + 1, tb - 1) * ROW_PITCH
            tile = pl.ds(pl.multiple_of(t * PITCH, SUBLANE), n)
            vv = v_s[slab, tile, :]

            def channel(c, carry):
                y_acc, sa_next = carry
                s_new = (s_ref[slab, c] - sa * row(b_s, slab, base + c)
                         + vv * row(k_s, slab, base + c))
                s_ref[slab, c] = s_new
                return (y_acc + s_new * row(r_s, slab, base + c),
                        sa_next + s_new * row(kk_s, slab, ahead + c))

            y, sa_next = lax.fori_loop(0, n, channel, (zero, zero), unroll=8)
            y_s[slab, tile, :] = y
            return sa_next

        lax.fori_loop(0, tb, step, lax.fori_loop(0, n, sa_first, zero, unroll=8))

        def rescale(c, carry, slab=slab):
            s_ref[slab, c] = s_ref[slab, c] * row(gam_s, slab, (tb - 1) * ROW_PITCH + c)
            return carry

        lax.fori_loop(0, n, rescale, 0, unroll=8)

    for q in range(tq):
        for slab in range(2):
            lanes = pl.ds(slab * LANE, LANE)
            y = jnp.stack([y_s[slab, pl.ds(q * SUBLANE * PITCH + c, SUBLANE, stride=PITCH), :]
                           for c in range(n)], axis=0)
            mean = jnp.mean(y, axis=0, keepdims=True)
            yc = y - mean
            var = jnp.mean(yc * yc, axis=0, keepdims=True)
            y = yc * lax.rsqrt(var + GN_EPS) * gnw_ref[:, :, lanes] + gnb_ref[:, :, lanes]
            y = y + bv_s[q, :, :, lanes]
            g = g_ref[0, q, :, :, lanes]
            o_ref[q, :, :, lanes] = y * (g / (1.0 + jnp.exp(-g)))


def _scan(p4, dec4, a4, head_params, tb=SCAN_TB):
    _, tqs, n, _, lanes = p4.shape
    tq = min(tb // SUBLANE, tqs)
    tile = (n, SUBLANE, lanes)
    seg = lambda s: pl.BlockSpec((1, tq) + tile, lambda i, s=s: (s, i, 0, 0, 0))
    blk = pl.BlockSpec((tq,) + tile, lambda i: (i, 0, 0, 0))
    par = pl.BlockSpec(tile, lambda i: (0, 0, 0))
    rows = (2, tq * SUBLANE * ROW_PITCH, LANE)
    tiles = (2, tq * SUBLANE * PITCH, LANE)
    scratch = [pltpu.VMEM((2, n, n, LANE), jnp.float32),
               pltpu.VMEM((3,) + tile, jnp.float32),
               pltpu.VMEM(tile, jnp.float32),
               pltpu.VMEM((tq,) + tile, jnp.float32)]
    scratch += [pltpu.VMEM(rows, jnp.float32) for _ in range(5)]
    scratch += [pltpu.VMEM(tiles, jnp.float32) for _ in range(2)]
    return pl.pallas_call(
        _scan_kernel,
        grid=(tqs // tq,),
        in_specs=[seg(0), seg(1), seg(2), seg(3), blk, blk] + [par] * 8,
        out_specs=blk,
        out_shape=jax.ShapeDtypeStruct((tqs,) + tile, jnp.float32),
        scratch_shapes=scratch,
        compiler_params=_params("arbitrary"),
        name="rwkv_scan",
    )(p4, p4, p4, p4, dec4, a4, *head_params)


def _outproj_lanes_kernel(y_ref, w_ref, r_ref, o_ref, lhs_ref):
    tq = y_ref.shape[0]
    tm = tq * SUBLANE
    nb, _, d = lhs_ref.shape

    @pl.when(pl.program_id(1) == 0)
    def _():
        low = lax.broadcasted_iota(jnp.int32, (tm, LANE), 1) < HEAD_DIM
        for c in range(HEAD_DIM // 2):
            cols = pl.ds(c * LANE, LANE)
            for bp in range(2):
                lanes = pl.ds(bp * LANE, LANE)
                e = y_ref[:, 2 * c, :, lanes].reshape(tm, LANE)
                o = y_ref[:, 2 * c + 1, :, lanes].reshape(tm, LANE)
                lhs_ref[2 * bp, :, cols] = jnp.where(
                    low, e, pltpu.roll(o, HEAD_DIM, axis=1)).astype(lhs_ref.dtype)
                lhs_ref[2 * bp + 1, :, cols] = jnp.where(
                    low, pltpu.roll(e, HEAD_DIM, axis=1), o).astype(lhs_ref.dtype)

    res = jnp.dot(lhs_ref[...].reshape(nb * tm, d), w_ref[...],
                  preferred_element_type=jnp.float32)
    o_ref[...] = r_ref[...] + res.reshape(o_ref.shape)


def _outproj_lanes(y4, w_perm, residual3, tm=128, tn=512):
    batch, seq, d = residual3.shape
    tq = tm // SUBLANE
    ospec = pl.BlockSpec((batch, tm, tn), lambda i, j: (0, i, j))
    return pl.pallas_call(
        _outproj_lanes_kernel,
        grid=(seq // tm, d // tn),
        in_specs=[pl.BlockSpec((tq, HEAD_DIM, SUBLANE, 2 * LANE), lambda i, j: (i, 0, 0, 0)),
                  pl.BlockSpec((d, tn), lambda i, j: (0, j)),
                  ospec],
        out_specs=ospec,
        out_shape=jax.ShapeDtypeStruct((batch, seq, d), jnp.float32),
        scratch_shapes=[pltpu.VMEM((batch, tm, d), jnp.bfloat16)],
        compiler_params=_params("parallel", "arbitrary"),
        name="rwkv_out_proj",
    )(y4, w_perm, residual3)


def _cols_nh(w):
    lead = w.shape[:-1]
    return w.reshape(lead + (-1, HEAD_DIM)).swapaxes(-1, -2).reshape(w.shape)


def _param_lanes(p, batch):
    pt = jnp.tile(p.reshape(-1, HEAD_DIM).T, (1, batch))
    return jnp.broadcast_to(pt[:, None, :], (HEAD_DIM, SUBLANE, pt.shape[1]))


def _rwkv_layer(h2d, batch, seq, norm_w, w_in, mu, w0, w2, a0, a2, k_k, k_a, r_k, gn_w,
                gn_b, w_out):
    d = h2d.shape[1]
    bf = jnp.bfloat16
    hn = _rms_norm(h2d, norm_w, bf)
    x_cols = slice(3 * d, 3 * d + 2 * LORA)
    w_main = jnp.concatenate([_cols_nh(w_in[:, i * d:(i + 1) * d]) for i in range(3)]
                             + [_cols_nh(w_in[:, 3 * d + 2 * LORA:])], axis=1).astype(bf)
    p4 = _inproj_lanes(hn.reshape(batch, seq, d), w_main)
    x = _matmul(hn, w_in[:, x_cols].astype(bf), tn=2 * LORA, name="rwkv_x_proj")
    dec4, a4 = _lora_lanes(x.reshape(batch, seq, 2 * LORA), mu[x_cols],
                           _cols_nh(w2).astype(bf), _cols_nh(a2).astype(bf),
                           _cols_nh(w0), _cols_nh(a0))
    head_params = [_param_lanes(x, batch) for x in
                   (mu[:d], mu[d:2 * d], mu[2 * d:3 * d], k_k, k_a, r_k.reshape(-1),
                    gn_w, gn_b)]
    y4 = _scan(p4, dec4, a4, head_params)
    w_out_perm = w_out.reshape(-1, HEAD_DIM, d).swapaxes(0, 1).reshape(d, d).astype(bf)
    out = _outproj_lanes(y4, w_out_perm, h2d.reshape(batch, seq, d))
    return out.reshape(batch * seq, d)


def _attn_kernel(sinks_ref, q_ref, kc_ref, kp_ref, vc_ref, vp_ref, g0_ref, g1_ref, g2_ref,
                 g3_ref, o_ref):
    bf = jnp.bfloat16
    n = HEAD_DIM
    pairs = GROUP // 2
    not_first = pl.program_id(1) > 0
    qi = lax.broadcasted_iota(jnp.int32, (pairs * BLOCK, 2 * BLOCK), 0) & (BLOCK - 1)
    kj = lax.broadcasted_iota(jnp.int32, (pairs * BLOCK, 2 * BLOCK), 1)
    delta = BLOCK + qi - kj
    mask = (delta >= 0) & (delta < BLOCK) & ((kj >= BLOCK) | not_first)
    low_kv = lax.broadcasted_iota(jnp.int32, (2 * BLOCK, LANE), 1) < n
    low_o = lax.broadcasted_iota(jnp.int32, (BLOCK, LANE), 1) < n
    gate_refs = (g0_ref, g1_ref, g2_ref, g3_ref)
    gate_w = g0_ref.shape[1]
    for hk in range(KV_HEADS):
        cols = slice((hk // 2) * LANE, (hk // 2 + 1) * LANE)
        kt = jnp.concatenate([kp_ref[:, cols], kc_ref[:, cols]], axis=0)
        vt = jnp.concatenate([vp_ref[:, cols], vc_ref[:, cols]], axis=0)
        kr = pltpu.roll(kt, n, axis=1)
        vr = pltpu.roll(vt, n, axis=1)
        if hk % 2 == 0:
            k_low, k_high = jnp.where(low_kv, kt, 0.0), jnp.where(low_kv, 0.0, kr)
            v_both = jnp.where(low_kv, vt, vr)
        else:
            k_low, k_high = jnp.where(low_kv, kr, 0.0), jnp.where(low_kv, 0.0, kt)
            v_both = jnp.where(low_kv, vr, vt)
        v_both = v_both.astype(bf)
        qs = jnp.concatenate([q_ref[:, (hk * pairs + p) * LANE:(hk * pairs + p + 1) * LANE]
                              for p in range(pairs)], axis=0)
        qs = (qs * (n ** -0.5)).astype(bf)
        outs = []
        for parity, k_sel in ((0, k_low), (1, k_high)):
            s = lax.dot_general(qs, k_sel.astype(bf), (((1,), (1,)), ((), ())),
                                preferred_element_type=jnp.float32)
            s = jnp.where(mask, s, MASK_VALUE)
            sink = jnp.concatenate(
                [jnp.full((BLOCK, 1), sinks_ref[hk * GROUP + 2 * p + parity], jnp.float32)
                 for p in range(pairs)], axis=0)
            m = jnp.maximum(jnp.max(s, axis=-1, keepdims=True), sink)
            e = jnp.exp(s - m)
            denom = jnp.sum(e, axis=-1, keepdims=True) + jnp.exp(sink - m)
            prob = (e / denom).astype(bf)
            outs.append(jnp.dot(prob, v_both, preferred_element_type=jnp.float32))
        for p in range(pairs):
            col = (hk * pairs + p) * LANE
            rows = slice(p * BLOCK, (p + 1) * BLOCK)
            o = jnp.where(low_o, outs[0][rows], outs[1][rows])
            g_ref = gate_refs[col // gate_w]
            gate = g_ref[:, col % gate_w:col % gate_w + LANE]
            o_ref[:, col:col + LANE] = (o * (gate / (1.0 + jnp.exp(-gate)))).astype(o_ref.dtype)


def _attention(p, batch, seq, d, sinks):
    nb = seq // BLOCK
    kvw = KV_HEADS * HEAD_DIM
    gate_w = 2 * kvw
    row = lambda b, n: b * nb + n
    prow = lambda b, n: b * nb + jnp.maximum(n - 1, 0)
    kblk, vblk = d // kvw, d // kvw + 1
    gblk = (d + 2 * kvw) // gate_w
    in_specs = [pl.BlockSpec(memory_space=pltpu.SMEM),
                pl.BlockSpec((BLOCK, d), lambda b, n: (row(b, n), 0)),
                pl.BlockSpec((BLOCK, kvw), lambda b, n: (row(b, n), kblk)),
                pl.BlockSpec((BLOCK, kvw), lambda b, n: (prow(b, n), kblk)),
                pl.BlockSpec((BLOCK, kvw), lambda b, n: (row(b, n), vblk)),
                pl.BlockSpec((BLOCK, kvw), lambda b, n: (prow(b, n), vblk))]
    in_specs += [pl.BlockSpec((BLOCK, gate_w), lambda b, n, j=j: (row(b, n), gblk + j))
                 for j in range(4)]
    return pl.pallas_call(
        _attn_kernel,
        grid=(batch, nb),
        in_specs=in_specs,
        out_specs=pl.BlockSpec((BLOCK, d), lambda b, n: (row(b, n), 0)),
        out_shape=jax.ShapeDtypeStruct((batch * seq, d), jnp.bfloat16),
        compiler_params=_params("parallel", "parallel"),
        name="swa_attention",
    )(sinks, p, p, p, p, p, p, p, p, p)


def _attn_layer(h2d, batch, seq, norm_w, w_in, sinks, w_out):
    d = h2d.shape[1]
    bf = jnp.bfloat16
    hn = _rms_norm(h2d, norm_w, bf)
    p = _matmul(hn, w_in.astype(bf), tn=1024, name="attn_in_proj")
    o = _attention(p, batch, seq, d, sinks)
    return _matmul(o, w_out.astype(bf), residual=h2d, name="attn_out_proj")


def kernel(x, norm_w, final_norm_w, rwkv_w_in, rwkv_mu, rwkv_w0, rwkv_w2, rwkv_a0, rwkv_a2,
           rwkv_k_k, rwkv_k_a, rwkv_r_k, rwkv_gn_w, rwkv_gn_b, rwkv_w_out,
           attn_w_in, attn_sinks, attn_w_out):
    batch, seq, d = x.shape
    h = x.reshape(batch * seq, d)
    depth = norm_w.shape[0]
    for i in range(depth):
        j = i // 2
        if i % 2 == 0:
            h = _rwkv_layer(h, batch, seq, norm_w[i], rwkv_w_in[j], rwkv_mu[j], rwkv_w0[j],
                            rwkv_w2[j], rwkv_a0[j], rwkv_a2[j], rwkv_k_k[j], rwkv_k_a[j],
                            rwkv_r_k[j], rwkv_gn_w[j], rwkv_gn_b[j], rwkv_w_out[j])
        else:
            h = _attn_layer(h, batch, seq, norm_w[i], attn_w_in[j], attn_sinks[j],
                            attn_w_out[j])
    return _rms_norm(h, final_norm_w, jnp.float32).reshape(batch, seq, d)
```

```python
import functools

import jax
import jax.numpy as jnp
from jax import lax
from jax.experimental import pallas as pl
from jax.experimental.pallas import tpu as pltpu

HEAD_DIM = 64
LORA = 128
KV_HEADS = 8
GROUP = 8
BLOCK = 128
NORM_EPS = 1e-5
GN_EPS = HEAD_DIM * 1e-5
L2_EPS = 1e-12
MASK_VALUE = -1e30

LANE = 128
SUBLANE = 8
VMEM_LIMIT_BYTES = 56 * 1024 * 1024
SCAN_TB = 16
PITCH = HEAD_DIM + SUBLANE
ROW_PITCH = HEAD_DIM + 4


def _params(*sem):
    return pltpu.CompilerParams(dimension_semantics=sem, vmem_limit_bytes=VMEM_LIMIT_BYTES)


def _rms_kernel(x_ref, g_ref, o_ref):
    x = x_ref[...]
    ms = jnp.mean(x * x, axis=-1, keepdims=True)
    o_ref[...] = (x * lax.rsqrt(ms + NORM_EPS) * g_ref[...]).astype(o_ref.dtype)


def _rms_norm(x2d, g, out_dtype, tm=256):
    m, d = x2d.shape
    return pl.pallas_call(
        _rms_kernel,
        grid=(m // tm,),
        in_specs=[pl.BlockSpec((tm, d), lambda i: (i, 0)),
                  pl.BlockSpec((1, d), lambda i: (0, 0))],
        out_specs=pl.BlockSpec((tm, d), lambda i: (i, 0)),
        out_shape=jax.ShapeDtypeStruct((m, d), out_dtype),
        compiler_params=_params("parallel"),
        name="rmsnorm",
    )(x2d, g.reshape(1, d))


def _mm_kernel(a_ref, b_ref, o_ref):
    o_ref[...] = jnp.dot(a_ref[...], b_ref[...], preferred_element_type=jnp.float32)


def _mm_res_kernel(a_ref, b_ref, r_ref, o_ref):
    o_ref[...] = r_ref[...] + jnp.dot(a_ref[...], b_ref[...],
                                      preferred_element_type=jnp.float32)


def _matmul(a, b, residual=None, tm=512, tn=1024, name="matmul"):
    m, k = a.shape
    _, n = b.shape
    tm = min(tm, m)
    grid = (n // tn, m // tm)
    in_specs = [pl.BlockSpec((tm, k), lambda j, i: (i, 0)),
                pl.BlockSpec((k, tn), lambda j, i: (0, j))]
    args = [a, b]
    kern = _mm_kernel
    if residual is not None:
        in_specs.append(pl.BlockSpec((tm, tn), lambda j, i: (i, j)))
        args.append(residual)
        kern = _mm_res_kernel
    return pl.pallas_call(
        kern,
        grid=grid,
        in_specs=in_specs,
        out_specs=pl.BlockSpec((tm, tn), lambda j, i: (i, j)),
        out_shape=jax.ShapeDtypeStruct((m, n), jnp.float32),
        compiler_params=_params("parallel", "parallel"),
        name=name,
    )(*args)


def _store_lanes(o_ref, lead, res, tm):
    low = lax.broadcasted_iota(jnp.int32, (tm, LANE), 1) < HEAD_DIM
    for bp in range(2):
        ra = res[(2 * bp) * tm:(2 * bp + 1) * tm]
        rb = res[(2 * bp + 1) * tm:(2 * bp + 2) * tm]
        for c in range(res.shape[1] // LANE):
            a = ra[:, c * LANE:(c + 1) * LANE]
            b = rb[:, c * LANE:(c + 1) * LANE]
            even = jnp.where(low, a, pltpu.roll(b, HEAD_DIM, axis=1))
            odd = jnp.where(low, pltpu.roll(a, HEAD_DIM, axis=1), b)
            lanes = pl.ds(bp * LANE, LANE)
            shape = (tm // SUBLANE, SUBLANE, LANE)
            o_ref[lead + (slice(None), 2 * c, slice(None), lanes)] = even.reshape(shape)
            o_ref[lead + (slice(None), 2 * c + 1, slice(None), lanes)] = odd.reshape(shape)


def _inproj_lanes_kernel(x_ref, g_ref, wx_ref, mu_ref, ws_ref, wg_ref,
                         xo_ref, ps_ref, pg_ref, hn_ref, carry_ref, *, shifted_blocks):
    nb, tm, d = x_ref.shape
    tn = ws_ref.shape[1]
    i, j = pl.program_id(0), pl.program_id(1)

    @pl.when(j == 0)
    def _():
        for b in range(nb):
            x = x_ref[b]
            ms = jnp.mean(x * x, axis=-1, keepdims=True)
            hn_ref[b * tm:(b + 1) * tm, :] = (
                x * lax.rsqrt(ms + NORM_EPS) * g_ref[...]).astype(hn_ref.dtype)
        xo_ref[...] = jnp.dot(hn_ref[...], wx_ref[...],
                              preferred_element_type=jnp.float32).reshape(xo_ref.shape)

    @pl.when(j < shifted_blocks)
    def _():
        res = jnp.dot(hn_ref[...], ws_ref[...], preferred_element_type=jnp.float32)
        prev = jnp.where(i == 0, 0.0, carry_ref[j])
        row0 = lax.broadcasted_iota(jnp.int32, (tm, tn), 0) == 0
        parts, lasts = [], []
        for b in range(nb):
            rb = res[b * tm:(b + 1) * tm]
            before = jnp.where(row0, prev[b:b + 1], pltpu.roll(rb, 1, axis=0))
            parts.append(rb + (before - rb) * mu_ref[...])
            lasts.append(rb[tm - 1:tm])
        carry_ref[j] = jnp.concatenate(lasts, axis=0)
        _store_lanes(ps_ref, (0,), jnp.concatenate(parts, axis=0), tm)

    @pl.when(j >= shifted_blocks)
    def _():
        res = jnp.dot(hn_ref[...], wg_ref[...], preferred_element_type=jnp.float32)
        _store_lanes(pg_ref, (), res, tm)


def _inproj_lanes(x3, norm_w, w_x, mu_s, w_shift, w_gate, tm=256, tn=512):
    batch, seq, d = x3.shape
    assert batch == 4 and d == 64 * HEAD_DIM
    nv = tn // HEAD_DIM
    per_seg = d // tn
    ns = w_shift.shape[1] // tn
    ng = w_gate.shape[1] // tn
    tile = (HEAD_DIM, SUBLANE, 2 * LANE)
    blk = (tm // SUBLANE, nv, SUBLANE, 2 * LANE)
    js = lambda j: jnp.minimum(j, ns - 1)
    jg = lambda j: jnp.maximum(j - ns, 0)
    return pl.pallas_call(
        functools.partial(_inproj_lanes_kernel, shifted_blocks=ns),
        grid=(seq // tm, ns + ng),
        in_specs=[pl.BlockSpec((batch, tm, d), lambda i, j: (0, i, 0),
                               pipeline_mode=pl.Buffered(1)),
                  pl.BlockSpec((1, d), lambda i, j: (0, 0)),
                  pl.BlockSpec((d, w_x.shape[1]), lambda i, j: (0, 0)),
                  pl.BlockSpec((1, tn), lambda i, j: (0, js(j))),
                  pl.BlockSpec((d, tn), lambda i, j: (0, js(j))),
                  pl.BlockSpec((d, tn), lambda i, j: (0, jg(j)))],
        out_specs=[pl.BlockSpec((batch, tm, w_x.shape[1]), lambda i, j: (0, i, 0)),
                   pl.BlockSpec((1,) + blk,
                                lambda i, j: (js(j) // per_seg, i, js(j) % per_seg, 0, 0)),
                   pl.BlockSpec(blk, lambda i, j: (i, jg(j), 0, 0))],
        out_shape=[jax.ShapeDtypeStruct((batch, seq, w_x.shape[1]), jnp.float32),
                   jax.ShapeDtypeStruct((ns // per_seg, seq // SUBLANE) + tile, jnp.float32),
                   jax.ShapeDtypeStruct((seq // SUBLANE,) + tile, jnp.float32)],
        scratch_shapes=[pltpu.VMEM((batch * tm, d), jnp.bfloat16),
                        pltpu.VMEM((ns, batch, tn), jnp.float32)],
        compiler_params=_params("arbitrary", "arbitrary"),
        name="rwkv_in_proj",
    )(x3, norm_w.reshape(1, d), w_x, mu_s.reshape(1, -1), w_shift, w_gate)


def _lora_kernel(x_ref, xp_ref, mu_ref, w2_ref, a2_ref, w0_ref, a0_ref, dec_ref, a_ref):
    nb, tm, _ = x_ref.shape
    first = pl.program_id(0) == 0
    row0 = lax.broadcasted_iota(jnp.int32, (tm, 2 * LORA), 0) == 0
    xw, xa = [], []
    for b in range(nb):
        x = x_ref[b]
        last = jnp.where(first, 0.0, xp_ref[b][SUBLANE - 1:SUBLANE])
        before = jnp.where(row0, last, pltpu.roll(x, 1, axis=0))
        x = x + (before - x) * mu_ref[...]
        xw.append(jnp.tanh(x[:, :LORA]).astype(jnp.bfloat16))
        xa.append(x[:, LORA:].astype(jnp.bfloat16))
    wl = w0_ref[...] + jnp.dot(jnp.concatenate(xw, axis=0), w2_ref[...],
                               preferred_element_type=jnp.float32)
    al = a0_ref[...] + jnp.dot(jnp.concatenate(xa, axis=0), a2_ref[...],
                               preferred_element_type=jnp.float32)
    z = -wl
    softplus = jnp.maximum(z, 0.0) + jnp.log(1.0 + jnp.exp(-jnp.abs(z)))
    _store_lanes(dec_ref, (), jnp.exp(-jnp.exp(-softplus - 0.5)), tm)
    _store_lanes(a_ref, (), 1.0 / (1.0 + jnp.exp(-al)), tm)


def _lora_lanes(x3, mu_x, w2p, a2p, w0p, a0p, tm=256, tn=512):
    batch, seq, _ = x3.shape
    d = w2p.shape[1]
    nv = tn // HEAD_DIM
    out = jax.ShapeDtypeStruct((seq // SUBLANE, HEAD_DIM, SUBLANE, 2 * LANE), jnp.float32)
    ospec = pl.BlockSpec((tm // SUBLANE, nv, SUBLANE, 2 * LANE), lambda i, j: (i, j, 0, 0))
    tq = tm // SUBLANE
    return pl.pallas_call(
        _lora_kernel,
        grid=(seq // tm, d // tn),
        in_specs=[pl.BlockSpec((batch, tm, 2 * LORA), lambda i, j: (0, i, 0)),
                  pl.BlockSpec((batch, SUBLANE, 2 * LORA),
                               lambda i, j: (0, jnp.maximum(i * tq - 1, 0), 0)),
                  pl.BlockSpec((1, 2 * LORA), lambda i, j: (0, 0)),
                  pl.BlockSpec((LORA, tn), lambda i, j: (0, j)),
                  pl.BlockSpec((LORA, tn), lambda i, j: (0, j)),
                  pl.BlockSpec((1, tn), lambda i, j: (0, j)),
                  pl.BlockSpec((1, tn), lambda i, j: (0, j))],
        out_specs=[ospec, ospec],
        out_shape=[out, out],
        compiler_params=_params("parallel", "arbitrary"),
        name="rwkv_lora",
    )(x3, x3, mu_x.reshape(1, -1), w2p, a2p, w0p.reshape(1, -1), a0p.reshape(1, -1))


def _scan_kernel(r_ref, k_ref, v_ref, g_ref, w_ref, a_ref,
                 kkp_ref, kap_ref, rkp_ref, gnw_ref, gnb_ref,
                 o_ref,
                 s_ref, gcar_ref, bv_s, r_s, gam_s, k_s, kk_s, b_s, v_s, y_s):
    tq = w_ref.shape[0]
    tb = tq * SUBLANE
    n = HEAD_DIM
    ch = SUBLANE

    @pl.when(pl.program_id(0) == 0)
    def _():
        s_ref[...] = jnp.zeros_like(s_ref)

    sub = lax.broadcasted_iota(jnp.int32, (ch, SUBLANE, 2 * LANE), 1)

    def to_steps(dst, pitch, q, c0, val):
        for c in range(ch):
            for slab in range(2):
                dst[slab, pl.ds(q * SUBLANE * pitch + c0 + c, SUBLANE, stride=pitch), :] = (
                    val[c, :, slab * LANE:(slab + 1) * LANE])

    def k_terms(q, cs):
        k = k_ref[0, q, cs]
        a = a_ref[q, cs]
        return k * kkp_ref[cs], k * (1.0 + (a - 1.0) * kap_ref[cs]), a

    for q in range(tq):
        def sums(j, carry, q=q):
            ssq, bon = carry
            cs = pl.ds(pl.multiple_of(j * ch, ch), ch)
            kkr, k2, _ = k_terms(q, cs)
            return ssq + kkr * kkr, bon + r_ref[0, q, cs] * k2 * rkp_ref[cs]

        zeros = jnp.zeros((ch, SUBLANE, 2 * LANE), jnp.float32)
        ssq, bon = lax.fori_loop(0, n // ch, sums, (zeros, zeros))
        nrm = jnp.sqrt(jnp.sum(ssq, axis=0, keepdims=True))
        inv_nrm = 1.0 / jnp.maximum(nrm, L2_EPS)
        bonus = jnp.sum(bon, axis=0, keepdims=True)

        def emit(j, carry, q=q, inv_nrm=inv_nrm, bonus=bonus):
            c0 = pl.multiple_of(j * ch, ch)
            cs = pl.ds(c0, ch)
            kkr, k2, a = k_terms(q, cs)
            v = v_ref[0, q, cs]
            kk = kkr * inv_nrm
            gam = w_ref[q, cs]
            for sh in (1, 2, 4):
                gam = gam * jnp.where(sub >= sh, pltpu.roll(gam, sh, axis=1), 1.0)
            before_tile = 1.0 if q == 0 else gcar_ref[cs]
            gam = gam * before_tile
            gam_prev = jnp.where(sub == 0, before_tile, pltpu.roll(gam, 1, axis=1))
            inv_gam = 1.0 / gam
            if q + 1 < tq:
                gcar_ref[cs] = jnp.broadcast_to(gam[:, SUBLANE - 1:SUBLANE, :], gam.shape)
            bv_s[q, cs] = bonus * v
            to_steps(r_s, ROW_PITCH, q, c0, r_ref[0, q, cs] * gam)
            to_steps(gam_s, ROW_PITCH, q, c0, gam)
            to_steps(k_s, ROW_PITCH, q, c0, k2 * inv_gam)
            to_steps(kk_s, ROW_PITCH, q, c0, kk * gam_prev)
            to_steps(b_s, ROW_PITCH, q, c0, kk * a * inv_gam)
            to_steps(v_s, PITCH, q, c0, v)
            return carry

        lax.fori_loop(0, n // ch, emit, 0)

    zero = jnp.zeros((n, LANE), jnp.float32)

    def row(ref, slab, i):
        return ref[slab, pl.ds(i, 1), :]

    for slab in range(2):
        def sa_first(c, acc, slab=slab):
            return acc + s_ref[slab, c] * row(kk_s, slab, c)

        def step(t, sa, slab=slab):
            base = t * ROW_PITCH
            ahead = jnp.minimum(t + 1, tb - 1) * ROW_PITCH
            tile = pl.ds(pl.multiple_of(t * PITCH, SUBLANE), n)
            vv = v_s[slab, tile, :]

            def channel(c, carry):
                y_acc, sa_next = carry
                s_new = (s_ref[slab, c] - sa * row(b_s, slab, base + c)
                         + vv * row(k_s, slab, base + c))
                s_ref[slab, c] = s_new
                return (y_acc + s_new * row(r_s, slab, base + c),
                        sa_next + s_new * row(kk_s, slab, ahead + c))

            y, sa_next = lax.fori_loop(0, n, channel, (zero, zero), unroll=8)
            y_s[slab, tile, :] = y
            return sa_next

        lax.fori_loop(0, tb, step, lax.fori_loop(0, n, sa_first, zero, unroll=8))

        def rescale(c, carry, slab=slab):
            s_ref[slab, c] = s_ref[slab, c] * row(gam_s, slab, (tb - 1) * ROW_PITCH + c)
            return carry

        lax.fori_loop(0, n, rescale, 0, unroll=8)

    for q in range(tq):
        for slab in range(2):
            lanes = pl.ds(slab * LANE, LANE)
            y = jnp.stack([y_s[slab, pl.ds(q * SUBLANE * PITCH + c, SUBLANE, stride=PITCH), :]
                           for c in range(n)], axis=0)
            mean = jnp.mean(y, axis=0, keepdims=True)
            yc = y - mean
            var = jnp.mean(yc * yc, axis=0, keepdims=True)
            y = yc * lax.rsqrt(var + GN_EPS) * gnw_ref[:, :, lanes] + gnb_ref[:, :, lanes]
            y = y + bv_s[q, :, :, lanes]
            g = g_ref[q, :, :, lanes]
            o_ref[q, :, :, lanes] = y * (g / (1.0 + jnp.exp(-g)))


def _scan(p3, g4, dec4, a4, head_params, tb=SCAN_TB):
    _, tqs, n, _, lanes = p3.shape
    tq = min(tb // SUBLANE, tqs)
    tile = (n, SUBLANE, lanes)
    seg = lambda s: pl.BlockSpec((1, tq) + tile, lambda i, s=s: (s, i, 0, 0, 0))
    blk = pl.BlockSpec((tq,) + tile, lambda i: (i, 0, 0, 0))
    par = pl.BlockSpec(tile, lambda i: (0, 0, 0))
    rows = (2, tq * SUBLANE * ROW_PITCH, LANE)
    tiles = (2, tq * SUBLANE * PITCH, LANE)
    scratch = [pltpu.VMEM((2, n, n, LANE), jnp.float32),
               pltpu.VMEM(tile, jnp.float32),
               pltpu.VMEM((tq,) + tile, jnp.float32)]
    scratch += [pltpu.VMEM(rows, jnp.float32) for _ in range(5)]
    scratch += [pltpu.VMEM(tiles, jnp.float32) for _ in range(2)]
    return pl.pallas_call(
        _scan_kernel,
        grid=(tqs // tq,),
        in_specs=[seg(0), seg(1), seg(2), blk, blk, blk] + [par] * len(head_params),
        out_specs=blk,
        out_shape=jax.ShapeDtypeStruct((tqs,) + tile, jnp.float32),
        scratch_shapes=scratch,
        compiler_params=_params("arbitrary"),
        name="rwkv_scan",
    )(p3, p3, p3, g4, dec4, a4, *head_params)


def _outproj_lanes_kernel(y_ref, w_ref, r_ref, o_ref, lhs_ref):
    tq = y_ref.shape[0]
    tm = tq * SUBLANE
    nb, _, d = lhs_ref.shape

    @pl.when(pl.program_id(1) == 0)
    def _():
        low = lax.broadcasted_iota(jnp.int32, (tm, LANE), 1) < HEAD_DIM
        for c in range(HEAD_DIM // 2):
            cols = pl.ds(c * LANE, LANE)
            for bp in range(2):
                lanes = pl.ds(bp * LANE, LANE)
                e = y_ref[:, 2 * c, :, lanes].reshape(tm, LANE)
                o = y_ref[:, 2 * c + 1, :, lanes].reshape(tm, LANE)
                lhs_ref[2 * bp, :, cols] = jnp.where(
                    low, e, pltpu.roll(o, HEAD_DIM, axis=1)).astype(lhs_ref.dtype)
                lhs_ref[2 * bp + 1, :, cols] = jnp.where(
                    low, pltpu.roll(e, HEAD_DIM, axis=1), o).astype(lhs_ref.dtype)

    res = jnp.dot(lhs_ref[...].reshape(nb * tm, d), w_ref[...],
                  preferred_element_type=jnp.float32)
    o_ref[...] = r_ref[...] + res.reshape(o_ref.shape)


def _outproj_lanes(y4, w_perm, residual3, tm=128, tn=512):
    batch, seq, d = residual3.shape
    tq = tm // SUBLANE
    ospec = pl.BlockSpec((batch, tm, tn), lambda i, j: (0, i, j))
    return pl.pallas_call(
        _outproj_lanes_kernel,
        grid=(seq // tm, d // tn),
        in_specs=[pl.BlockSpec((tq, HEAD_DIM, SUBLANE, 2 * LANE), lambda i, j: (i, 0, 0, 0)),
                  pl.BlockSpec((d, tn), lambda i, j: (0, j)),
                  ospec],
        out_specs=ospec,
        out_shape=jax.ShapeDtypeStruct((batch, seq, d), jnp.float32),
        scratch_shapes=[pltpu.VMEM((batch, tm, d), jnp.bfloat16)],
        compiler_params=_params("parallel", "arbitrary"),
        name="rwkv_out_proj",
    )(y4, w_perm, residual3)


def _cols_nh(w, segs=1):
    lead = w.shape[:-1]
    return w.reshape(lead + (segs, -1, HEAD_DIM)).swapaxes(-1, -2).reshape(w.shape)


def _param_lanes(p, batch):
    pt = jnp.tile(p.reshape(-1, HEAD_DIM).T, (1, batch))
    return jnp.broadcast_to(pt[:, None, :], (HEAD_DIM, SUBLANE, pt.shape[1]))


def _rwkv_layer(h2d, batch, seq, norm_w, w_in, mu, w0, w2, a0, a2, k_k, k_a, r_k, gn_w,
                gn_b, w_out):
    d = h2d.shape[1]
    bf = jnp.bfloat16
    x_cols = slice(3 * d, 3 * d + 2 * LORA)
    x, p3, g4 = _inproj_lanes(h2d.reshape(batch, seq, d), norm_w, w_in[:, x_cols].astype(bf),
                              _cols_nh(mu[:3 * d], 3),
                              _cols_nh(w_in[:, :3 * d], 3).astype(bf),
                              _cols_nh(w_in[:, 3 * d + 2 * LORA:]).astype(bf))
    dec4, a4 = _lora_lanes(x, mu[x_cols],
                           _cols_nh(w2).astype(bf), _cols_nh(a2).astype(bf),
                           _cols_nh(w0), _cols_nh(a0))
    head_params = [_param_lanes(x, batch) for x in (k_k, k_a, r_k.reshape(-1), gn_w, gn_b)]
    y4 = _scan(p3, g4, dec4, a4, head_params)
    w_out_perm = w_out.reshape(-1, HEAD_DIM, d).swapaxes(0, 1).reshape(d, d).astype(bf)
    out = _outproj_lanes(y4, w_out_perm, h2d.reshape(batch, seq, d))
    return out.reshape(batch * seq, d)


def _attn_kernel(sinks_ref, q_ref, kc_ref, kp_ref, vc_ref, vp_ref, g0_ref, g1_ref, g2_ref,
                 g3_ref, o_ref):
    bf = jnp.bfloat16
    n = HEAD_DIM
    pairs = GROUP // 2
    not_first = pl.program_id(1) > 0
    qi = lax.broadcasted_iota(jnp.int32, (pairs * BLOCK, 2 * BLOCK), 0) & (BLOCK - 1)
    kj = lax.broadcasted_iota(jnp.int32, (pairs * BLOCK, 2 * BLOCK), 1)
    delta = BLOCK + qi - kj
    mask = (delta >= 0) & (delta < BLOCK) & ((kj >= BLOCK) | not_first)
    low_kv = lax.broadcasted_iota(jnp.int32, (2 * BLOCK, LANE), 1) < n
    low_o = lax.broadcasted_iota(jnp.int32, (BLOCK, LANE), 1) < n
    gate_refs = (g0_ref, g1_ref, g2_ref, g3_ref)
    gate_w = g0_ref.shape[1]
    for hk in range(KV_HEADS):
        cols = slice((hk // 2) * LANE, (hk // 2 + 1) * LANE)
        kt = jnp.concatenate([kp_ref[:, cols], kc_ref[:, cols]], axis=0)
        vt = jnp.concatenate([vp_ref[:, cols], vc_ref[:, cols]], axis=0)
        kr = pltpu.roll(kt, n, axis=1)
        vr = pltpu.roll(vt, n, axis=1)
        if hk % 2 == 0:
            k_low, k_high = jnp.where(low_kv, kt, 0.0), jnp.where(low_kv, 0.0, kr)
            v_both = jnp.where(low_kv, vt, vr)
        else:
            k_low, k_high = jnp.where(low_kv, kr, 0.0), jnp.where(low_kv, 0.0, kt)
            v_both = jnp.where(low_kv, vr, vt)
        v_both = v_both.astype(bf)
        qs = jnp.concatenate([q_ref[:, (hk * pairs + p) * LANE:(hk * pairs + p + 1) * LANE]
                              for p in range(pairs)], axis=0)
        qs = (qs * (n ** -0.5)).astype(bf)
        outs = []
        for parity, k_sel in ((0, k_low), (1, k_high)):
            s = lax.dot_general(qs, k_sel.astype(bf), (((1,), (1,)), ((), ())),
                                preferred_element_type=jnp.float32)
            s = jnp.where(mask, s, MASK_VALUE)
            sink = jnp.concatenate(
                [jnp.full((BLOCK, 1), sinks_ref[hk * GROUP + 2 * p + parity], jnp.float32)
                 for p in range(pairs)], axis=0)
            m = jnp.maximum(jnp.max(s, axis=-1, keepdims=True), sink)
            e = jnp.exp(s - m)
            denom = jnp.sum(e, axis=-1, keepdims=True) + jnp.exp(sink - m)
            prob = (e / denom).astype(bf)
            outs.append(jnp.dot(prob, v_both, preferred_element_type=jnp.float32))
        for p in range(pairs):
            col = (hk * pairs + p) * LANE
            rows = slice(p * BLOCK, (p + 1) * BLOCK)
            o = jnp.where(low_o, outs[0][rows], outs[1][rows])
            g_ref = gate_refs[col // gate_w]
            gate = g_ref[:, col % gate_w:col % gate_w + LANE]
            o_ref[:, col:col + LANE] = (o * (gate / (1.0 + jnp.exp(-gate)))).astype(o_ref.dtype)


def _attention(p, batch, seq, d, sinks):
    nb = seq // BLOCK
    kvw = KV_HEADS * HEAD_DIM
    gate_w = 2 * kvw
    row = lambda b, n: b * nb + n
    prow = lambda b, n: b * nb + jnp.maximum(n - 1, 0)
    kblk, vblk = d // kvw, d // kvw + 1
    gblk = (d + 2 * kvw) // gate_w
    in_specs = [pl.BlockSpec(memory_space=pltpu.SMEM),
                pl.BlockSpec((BLOCK, d), lambda b, n: (row(b, n), 0)),
                pl.BlockSpec((BLOCK, kvw), lambda b, n: (row(b, n), kblk)),
                pl.BlockSpec((BLOCK, kvw), lambda b, n: (prow(b, n), kblk)),
                pl.BlockSpec((BLOCK, kvw), lambda b, n: (row(b, n), vblk)),
                pl.BlockSpec((BLOCK, kvw), lambda b, n: (prow(b, n), vblk))]
    in_specs += [pl.BlockSpec((BLOCK, gate_w), lambda b, n, j=j: (row(b, n), gblk + j))
                 for j in range(4)]
    return pl.pallas_call(
        _attn_kernel,
        grid=(batch, nb),
        in_specs=in_specs,
        out_specs=pl.BlockSpec((BLOCK, d), lambda b, n: (row(b, n), 0)),
        out_shape=jax.ShapeDtypeStruct((batch * seq, d), jnp.bfloat16),
        compiler_params=_params("parallel", "parallel"),
        name="swa_attention",
    )(sinks, p, p, p, p, p, p, p, p, p)


def _attn_layer(h2d, batch, seq, norm_w, w_in, sinks, w_out):
    d = h2d.shape[1]
    bf = jnp.bfloat16
    hn = _rms_norm(h2d, norm_w, bf)
    p = _matmul(hn, w_in.astype(bf), tn=1024, name="attn_in_proj")
    o = _attention(p, batch, seq, d, sinks)
    return _matmul(o, w_out.astype(bf), residual=h2d, name="attn_out_proj")


def kernel(x, norm_w, final_norm_w, rwkv_w_in, rwkv_mu, rwkv_w0, rwkv_w2, rwkv_a0, rwkv_a2,
           rwkv_k_k, rwkv_k_a, rwkv_r_k, rwkv_gn_w, rwkv_gn_b, rwkv_w_out,
           attn_w_in, attn_sinks, attn_w_out):
    batch, seq, d = x.shape
    h = x.reshape(batch * seq, d)
    depth = norm_w.shape[0]
    for i in range(depth):
        j = i // 2
        if i % 2 == 0:
            h = _rwkv_layer(h, batch, seq, norm_w[i], rwkv_w_in[j], rwkv_mu[j], rwkv_w0[j],
                            rwkv_w2[j], rwkv_a0[j], rwkv_a2[j], rwkv_k_k[j], rwkv_k_a[j],
                            rwkv_r_k[j], rwkv_gn_w[j], rwkv_gn_b[j], rwkv_w_out[j])
        else:
            h = _attn_layer(h, batch, seq, norm_w[i], attn_w_in[j], attn_sinks[j],
                            attn_w_out[j])
    return _rms_norm(h, final_norm_w, jnp.float32).reshape(batch, seq, d)
```

```python
import functools

import jax
import jax.numpy as jnp
import numpy as np
from jax import lax
from jax.experimental import pallas as pl
from jax.experimental.pallas import tpu as pltpu

HEAD_DIM = 64
LORA = 128
KV_HEADS = 8
GROUP = 8
BLOCK = 128
NORM_EPS = 1e-5
GN_EPS = HEAD_DIM * 1e-5
L2_EPS = 1e-12
MASK_VALUE = -1e30

LANE = 128
SUBLANE = 8
VMEM_LIMIT_BYTES = 56 * 1024 * 1024
SCAN_TB = 16
PITCH = HEAD_DIM + SUBLANE
ROW_PITCH = HEAD_DIM + 4


def _params(*sem):
    return pltpu.CompilerParams(dimension_semantics=sem, vmem_limit_bytes=VMEM_LIMIT_BYTES)


def _rms_kernel(x_ref, g_ref, o_ref):
    x = x_ref[...]
    ms = jnp.mean(x * x, axis=-1, keepdims=True)
    o_ref[...] = (x * lax.rsqrt(ms + NORM_EPS) * g_ref[...]).astype(o_ref.dtype)


def _rms_norm(x2d, g, out_dtype, tm=256):
    m, d = x2d.shape
    return pl.pallas_call(
        _rms_kernel,
        grid=(m // tm,),
        in_specs=[pl.BlockSpec((tm, d), lambda i: (i, 0)),
                  pl.BlockSpec((1, d), lambda i: (0, 0))],
        out_specs=pl.BlockSpec((tm, d), lambda i: (i, 0)),
        out_shape=jax.ShapeDtypeStruct((m, d), out_dtype),
        compiler_params=_params("parallel"),
        name="rmsnorm",
    )(x2d, g.reshape(1, d))


def _mm_kernel(a_ref, b_ref, o_ref):
    o_ref[...] = jnp.dot(a_ref[...], b_ref[...], preferred_element_type=jnp.float32)


def _mm_res_kernel(a_ref, b_ref, r_ref, o_ref):
    o_ref[...] = r_ref[...] + jnp.dot(a_ref[...], b_ref[...],
                                      preferred_element_type=jnp.float32)


def _matmul(a, b, residual=None, tm=512, tn=1024, name="matmul"):
    m, k = a.shape
    _, n = b.shape
    tm = min(tm, m)
    grid = (n // tn, m // tm)
    in_specs = [pl.BlockSpec((tm, k), lambda j, i: (i, 0)),
                pl.BlockSpec((k, tn), lambda j, i: (0, j))]
    args = [a, b]
    kern = _mm_kernel
    if residual is not None:
        in_specs.append(pl.BlockSpec((tm, tn), lambda j, i: (i, j)))
        args.append(residual)
        kern = _mm_res_kernel
    return pl.pallas_call(
        kern,
        grid=grid,
        in_specs=in_specs,
        out_specs=pl.BlockSpec((tm, tn), lambda j, i: (i, j)),
        out_shape=jax.ShapeDtypeStruct((m, n), jnp.float32),
        compiler_params=_params("parallel", "parallel"),
        name=name,
    )(*args)


def _store_lanes(o_ref, lead, res, tm):
    low = lax.broadcasted_iota(jnp.int32, (tm, LANE), 1) < HEAD_DIM
    for bp in range(2):
        ra = res[(2 * bp) * tm:(2 * bp + 1) * tm]
        rb = res[(2 * bp + 1) * tm:(2 * bp + 2) * tm]
        for c in range(res.shape[1] // LANE):
            a = ra[:, c * LANE:(c + 1) * LANE]
            b = rb[:, c * LANE:(c + 1) * LANE]
            even = jnp.where(low, a, pltpu.roll(b, HEAD_DIM, axis=1))
            odd = jnp.where(low, pltpu.roll(a, HEAD_DIM, axis=1), b)
            lanes = pl.ds(bp * LANE, LANE)
            shape = (tm // SUBLANE, SUBLANE, LANE)
            o_ref[lead + (slice(None), 2 * c, slice(None), lanes)] = even.reshape(shape)
            o_ref[lead + (slice(None), 2 * c + 1, slice(None), lanes)] = odd.reshape(shape)


def _inproj_lanes_kernel(x_ref, g_ref, wx_ref, mu_ref, ws_ref, wg_ref,
                         xo_ref, ps_ref, pg_ref, hn_ref, carry_ref, *, shifted_blocks):
    nb, tm, d = x_ref.shape
    tn = ws_ref.shape[1]
    i, j = pl.program_id(0), pl.program_id(1)

    @pl.when(j == 0)
    def _():
        for b in range(nb):
            x = x_ref[b]
            ms = jnp.mean(x * x, axis=-1, keepdims=True)
            hn_ref[b * tm:(b + 1) * tm, :] = (
                x * lax.rsqrt(ms + NORM_EPS) * g_ref[...]).astype(hn_ref.dtype)
        xo_ref[...] = jnp.dot(hn_ref[...], wx_ref[...],
                              preferred_element_type=jnp.float32).reshape(xo_ref.shape)

    @pl.when(j < shifted_blocks)
    def _():
        res = jnp.dot(hn_ref[...], ws_ref[...], preferred_element_type=jnp.float32)
        prev = jnp.where(i == 0, 0.0, carry_ref[j])
        row0 = lax.broadcasted_iota(jnp.int32, (tm, tn), 0) == 0
        parts, lasts = [], []
        for b in range(nb):
            rb = res[b * tm:(b + 1) * tm]
            before = jnp.where(row0, prev[b:b + 1], pltpu.roll(rb, 1, axis=0))
            parts.append(rb + (before - rb) * mu_ref[...])
            lasts.append(rb[tm - 1:tm])
        carry_ref[j] = jnp.concatenate(lasts, axis=0)
        _store_lanes(ps_ref, (0,), jnp.concatenate(parts, axis=0), tm)

    @pl.when(j >= shifted_blocks)
    def _():
        res = jnp.dot(hn_ref[...], wg_ref[...], preferred_element_type=jnp.float32)
        _store_lanes(pg_ref, (), res, tm)


def _inproj_lanes(x3, norm_w, w_x, mu_s, w_shift, w_gate, tm=256, tn=512):
    batch, seq, d = x3.shape
    assert batch == 4 and d == 64 * HEAD_DIM
    nv = tn // HEAD_DIM
    per_seg = d // tn
    ns = w_shift.shape[1] // tn
    ng = w_gate.shape[1] // tn
    tile = (HEAD_DIM, SUBLANE, 2 * LANE)
    blk = (tm // SUBLANE, nv, SUBLANE, 2 * LANE)
    js = lambda j: jnp.minimum(j, ns - 1)
    jg = lambda j: jnp.maximum(j - ns, 0)
    return pl.pallas_call(
        functools.partial(_inproj_lanes_kernel, shifted_blocks=ns),
        grid=(seq // tm, ns + ng),
        in_specs=[pl.BlockSpec((batch, tm, d), lambda i, j: (0, i, 0),
                               pipeline_mode=pl.Buffered(1)),
                  pl.BlockSpec((1, d), lambda i, j: (0, 0)),
                  pl.BlockSpec((d, w_x.shape[1]), lambda i, j: (0, 0)),
                  pl.BlockSpec((1, tn), lambda i, j: (0, js(j))),
                  pl.BlockSpec((d, tn), lambda i, j: (0, js(j))),
                  pl.BlockSpec((d, tn), lambda i, j: (0, jg(j)))],
        out_specs=[pl.BlockSpec((batch, tm, w_x.shape[1]), lambda i, j: (0, i, 0)),
                   pl.BlockSpec((1,) + blk,
                                lambda i, j: (js(j) // per_seg, i, js(j) % per_seg, 0, 0)),
                   pl.BlockSpec(blk, lambda i, j: (i, jg(j), 0, 0))],
        out_shape=[jax.ShapeDtypeStruct((batch, seq, w_x.shape[1]), jnp.float32),
                   jax.ShapeDtypeStruct((ns // per_seg, seq // SUBLANE) + tile, jnp.float32),
                   jax.ShapeDtypeStruct((seq // SUBLANE,) + tile, jnp.float32)],
        scratch_shapes=[pltpu.VMEM((batch * tm, d), jnp.bfloat16),
                        pltpu.VMEM((ns, batch, tn), jnp.float32)],
        compiler_params=_params("arbitrary", "arbitrary"),
        name="rwkv_in_proj",
    )(x3, norm_w.reshape(1, d), w_x, mu_s.reshape(1, -1), w_shift, w_gate)


def _lora_kernel(x_ref, xp_ref, mu_ref, w2_ref, a2_ref, w0_ref, a0_ref, dec_ref, a_ref):
    nb, tm, _ = x_ref.shape
    first = pl.program_id(0) == 0
    row0 = lax.broadcasted_iota(jnp.int32, (tm, 2 * LORA), 0) == 0
    xw, xa = [], []
    for b in range(nb):
        x = x_ref[b]
        last = jnp.where(first, 0.0, xp_ref[b][SUBLANE - 1:SUBLANE])
        before = jnp.where(row0, last, pltpu.roll(x, 1, axis=0))
        x = x + (before - x) * mu_ref[...]
        xw.append(jnp.tanh(x[:, :LORA]).astype(jnp.bfloat16))
        xa.append(x[:, LORA:].astype(jnp.bfloat16))
    wl = w0_ref[...] + jnp.dot(jnp.concatenate(xw, axis=0), w2_ref[...],
                               preferred_element_type=jnp.float32)
    al = a0_ref[...] + jnp.dot(jnp.concatenate(xa, axis=0), a2_ref[...],
                               preferred_element_type=jnp.float32)
    sig_w = 1.0 / (1.0 + jnp.exp(-wl))
    _store_lanes(dec_ref, (), jnp.exp(sig_w * (-float(np.exp(-0.5)))), tm)
    _store_lanes(a_ref, (), 1.0 / (1.0 + jnp.exp(-al)), tm)


def _lora_lanes(x3, mu_x, w2p, a2p, w0p, a0p, tm=256, tn=512):
    batch, seq, _ = x3.shape
    d = w2p.shape[1]
    nv = tn // HEAD_DIM
    out = jax.ShapeDtypeStruct((seq // SUBLANE, HEAD_DIM, SUBLANE, 2 * LANE), jnp.float32)
    ospec = pl.BlockSpec((tm // SUBLANE, nv, SUBLANE, 2 * LANE), lambda i, j: (i, j, 0, 0))
    tq = tm // SUBLANE
    return pl.pallas_call(
        _lora_kernel,
        grid=(seq // tm, d // tn),
        in_specs=[pl.BlockSpec((batch, tm, 2 * LORA), lambda i, j: (0, i, 0)),
                  pl.BlockSpec((batch, SUBLANE, 2 * LORA),
                               lambda i, j: (0, jnp.maximum(i * tq - 1, 0), 0)),
                  pl.BlockSpec((1, 2 * LORA), lambda i, j: (0, 0)),
                  pl.BlockSpec((LORA, tn), lambda i, j: (0, j)),
                  pl.BlockSpec((LORA, tn), lambda i, j: (0, j)),
                  pl.BlockSpec((1, tn), lambda i, j: (0, j)),
                  pl.BlockSpec((1, tn), lambda i, j: (0, j))],
        out_specs=[ospec, ospec],
        out_shape=[out, out],
        compiler_params=_params("parallel", "arbitrary"),
        name="rwkv_lora",
    )(x3, x3, mu_x.reshape(1, -1), w2p, a2p, w0p.reshape(1, -1), a0p.reshape(1, -1))


def _scan_kernel(r_ref, k_ref, v_ref, g_ref, w_ref, a_ref,
                 kkp_ref, kap_ref, rkp_ref, gnw_ref, gnb_ref,
                 o_ref,
                 s_ref, gcar_ref, bv_s, r_s, gam_s, k_s, kk_s, b_s, v_s, y_s):
    tq = w_ref.shape[0]
    tb = tq * SUBLANE
    n = HEAD_DIM
    ch = SUBLANE

    @pl.when(pl.program_id(0) == 0)
    def _():
        s_ref[...] = jnp.zeros_like(s_ref)

    sub = lax.broadcasted_iota(jnp.int32, (ch, SUBLANE, 2 * LANE), 1)

    def to_steps(dst, pitch, q, c0, val):
        for c in range(ch):
            for slab in range(2):
                dst[slab, pl.ds(q * SUBLANE * pitch + c0 + c, SUBLANE, stride=pitch), :] = (
                    val[c, :, slab * LANE:(slab + 1) * LANE])

    def k_terms(q, cs):
        k = k_ref[0, q, cs]
        a = a_ref[q, cs]
        return k * kkp_ref[cs], k * (1.0 + (a - 1.0) * kap_ref[cs]), a

    for q in range(tq):
        def sums(j, carry, q=q):
            ssq, bon = carry
            cs = pl.ds(pl.multiple_of(j * ch, ch), ch)
            kkr, k2, _ = k_terms(q, cs)
            return ssq + kkr * kkr, bon + r_ref[0, q, cs] * k2 * rkp_ref[cs]

        zeros = jnp.zeros((ch, SUBLANE, 2 * LANE), jnp.float32)
        ssq, bon = lax.fori_loop(0, n // ch, sums, (zeros, zeros))
        nrm = jnp.sqrt(jnp.sum(ssq, axis=0, keepdims=True))
        inv_nrm = 1.0 / jnp.maximum(nrm, L2_EPS)
        bonus = jnp.sum(bon, axis=0, keepdims=True)

        def emit(j, carry, q=q, inv_nrm=inv_nrm, bonus=bonus):
            c0 = pl.multiple_of(j * ch, ch)
            cs = pl.ds(c0, ch)
            kkr, k2, a = k_terms(q, cs)
            v = v_ref[0, q, cs]
            kk = kkr * inv_nrm
            gam = w_ref[q, cs]
            for sh in (1, 2, 4):
                gam = gam * jnp.where(sub >= sh, pltpu.roll(gam, sh, axis=1), 1.0)
            before_tile = 1.0 if q == 0 else gcar_ref[cs]
            gam = gam * before_tile
            gam_prev = jnp.where(sub == 0, before_tile, pltpu.roll(gam, 1, axis=1))
            inv_gam = 1.0 / gam
            if q + 1 < tq:
                gcar_ref[cs] = jnp.broadcast_to(gam[:, SUBLANE - 1:SUBLANE, :], gam.shape)
            bv_s[q, cs] = bonus * v
            to_steps(r_s, ROW_PITCH, q, c0, r_ref[0, q, cs] * gam)
            to_steps(gam_s, ROW_PITCH, q, c0, gam)
            to_steps(k_s, ROW_PITCH, q, c0, k2 * inv_gam)
            to_steps(kk_s, ROW_PITCH, q, c0, kk * gam_prev)
            to_steps(b_s, ROW_PITCH, q, c0, kk * a * inv_gam)
            to_steps(v_s, PITCH, q, c0, v)
            return carry

        lax.fori_loop(0, n // ch, emit, 0)

    zero = jnp.zeros((n, LANE), jnp.float32)

    def row(ref, slab, i):
        return ref[slab, pl.ds(i, 1), :]

    for slab in range(2):
        def sa_first(c, acc, slab=slab):
            return acc + s_ref[slab, c] * row(kk_s, slab, c)

        def step(t, sa, slab=slab):
            base = t * ROW_PITCH
            ahead = jnp.minimum(t + 1, tb - 1) * ROW_PITCH
            tile = pl.ds(pl.multiple_of(t * PITCH, SUBLANE), n)
            vv = v_s[slab, tile, :]

            def channel(c, carry):
                y_acc, sa_next = carry
                s_new = (s_ref[slab, c] - sa * row(b_s, slab, base + c)
                         + vv * row(k_s, slab, base + c))
                s_ref[slab, c] = s_new
                return (y_acc + s_new * row(r_s, slab, base + c),
                        sa_next + s_new * row(kk_s, slab, ahead + c))

            y, sa_next = lax.fori_loop(0, n, channel, (zero, zero), unroll=True)
            y_s[slab, tile, :] = y
            return sa_next

        lax.fori_loop(0, tb, step, lax.fori_loop(0, n, sa_first, zero, unroll=8))

        def rescale(c, carry, slab=slab):
            s_ref[slab, c] = s_ref[slab, c] * row(gam_s, slab, (tb - 1) * ROW_PITCH + c)
            return carry

        lax.fori_loop(0, n, rescale, 0, unroll=8)

    for q in range(tq):
        for slab in range(2):
            lanes = pl.ds(slab * LANE, LANE)
            y = jnp.stack([y_s[slab, pl.ds(q * SUBLANE * PITCH + c, SUBLANE, stride=PITCH), :]
                           for c in range(n)], axis=0)
            mean = jnp.mean(y, axis=0, keepdims=True)
            yc = y - mean
            var = jnp.mean(yc * yc, axis=0, keepdims=True)
            y = yc * lax.rsqrt(var + GN_EPS) * gnw_ref[:, :, lanes] + gnb_ref[:, :, lanes]
            y = y + bv_s[q, :, :, lanes]
            g = g_ref[q, :, :, lanes]
            o_ref[q, :, :, lanes] = y * (g / (1.0 + jnp.exp(-g)))


def _scan(p3, g4, dec4, a4, head_params, tb=SCAN_TB):
    _, tqs, n, _, lanes = p3.shape
    tq = min(tb // SUBLANE, tqs)
    tile = (n, SUBLANE, lanes)
    seg = lambda s: pl.BlockSpec((1, tq) + tile, lambda i, s=s: (s, i, 0, 0, 0))
    blk = pl.BlockSpec((tq,) + tile, lambda i: (i, 0, 0, 0))
    par = pl.BlockSpec(tile, lambda i: (0, 0, 0))
    rows = (2, tq * SUBLANE * ROW_PITCH, LANE)
    tiles = (2, tq * SUBLANE * PITCH, LANE)
    scratch = [pltpu.VMEM((2, n, n, LANE), jnp.float32),
               pltpu.VMEM(tile, jnp.float32),
               pltpu.VMEM((tq,) + tile, jnp.float32)]
    scratch += [pltpu.VMEM(rows, jnp.float32) for _ in range(5)]
    scratch += [pltpu.VMEM(tiles, jnp.float32) for _ in range(2)]
    return pl.pallas_call(
        _scan_kernel,
        grid=(tqs // tq,),
        in_specs=[seg(0), seg(1), seg(2), blk, blk, blk] + [par] * len(head_params),
        out_specs=blk,
        out_shape=jax.ShapeDtypeStruct((tqs,) + tile, jnp.float32),
        scratch_shapes=scratch,
        compiler_params=_params("arbitrary"),
        name="rwkv_scan",
    )(p3, p3, p3, g4, dec4, a4, *head_params)


def _outproj_lanes_kernel(y_ref, w_ref, r_ref, o_ref, lhs_ref):
    tq = y_ref.shape[0]
    tm = tq * SUBLANE
    nb, _, d = lhs_ref.shape

    @pl.when(pl.program_id(1) == 0)
    def _():
        low = lax.broadcasted_iota(jnp.int32, (tm, LANE), 1) < HEAD_DIM
        for c in range(HEAD_DIM // 2):
            cols = pl.ds(c * LANE, LANE)
            for bp in range(2):
                lanes = pl.ds(bp * LANE, LANE)
                e = y_ref[:, 2 * c, :, lanes].reshape(tm, LANE)
                o = y_ref[:, 2 * c + 1, :, lanes].reshape(tm, LANE)
                lhs_ref[2 * bp, :, cols] = jnp.where(
                    low, e, pltpu.roll(o, HEAD_DIM, axis=1)).astype(lhs_ref.dtype)
                lhs_ref[2 * bp + 1, :, cols] = jnp.where(
                    low, pltpu.roll(e, HEAD_DIM, axis=1), o).astype(lhs_ref.dtype)

    res = jnp.dot(lhs_ref[...].reshape(nb * tm, d), w_ref[...],
                  preferred_element_type=jnp.float32)
    o_ref[...] = r_ref[...] + res.reshape(o_ref.shape)


def _outproj_lanes(y4, w_perm, residual3, tm=128, tn=512):
    batch, seq, d = residual3.shape
    tq = tm // SUBLANE
    ospec = pl.BlockSpec((batch, tm, tn), lambda i, j: (0, i, j))
    return pl.pallas_call(
        _outproj_lanes_kernel,
        grid=(seq // tm, d // tn),
        in_specs=[pl.BlockSpec((tq, HEAD_DIM, SUBLANE, 2 * LANE), lambda i, j: (i, 0, 0, 0)),
                  pl.BlockSpec((d, tn), lambda i, j: (0, j)),
                  ospec],
        out_specs=ospec,
        out_shape=jax.ShapeDtypeStruct((batch, seq, d), jnp.float32),
        scratch_shapes=[pltpu.VMEM((batch, tm, d), jnp.bfloat16)],
        compiler_params=_params("parallel", "arbitrary"),
        name="rwkv_out_proj",
    )(y4, w_perm, residual3)


def _cols_nh(w, segs=1):
    lead = w.shape[:-1]
    return w.reshape(lead + (segs, -1, HEAD_DIM)).swapaxes(-1, -2).reshape(w.shape)


def _param_lanes(p, batch):
    pt = jnp.tile(p.reshape(-1, HEAD_DIM).T, (1, batch))
    return jnp.broadcast_to(pt[:, None, :], (HEAD_DIM, SUBLANE, pt.shape[1]))


def _rwkv_layer(h2d, batch, seq, norm_w, w_in, mu, w0, w2, a0, a2, k_k, k_a, r_k, gn_w,
                gn_b, w_out):
    d = h2d.shape[1]
    bf = jnp.bfloat16
    x_cols = slice(3 * d, 3 * d + 2 * LORA)
    x, p3, g4 = _inproj_lanes(h2d.reshape(batch, seq, d), norm_w, w_in[:, x_cols].astype(bf),
                              _cols_nh(mu[:3 * d], 3),
                              _cols_nh(w_in[:, :3 * d], 3).astype(bf),
                              _cols_nh(w_in[:, 3 * d + 2 * LORA:]).astype(bf))
    dec4, a4 = _lora_lanes(x, mu[x_cols],
                           _cols_nh(w2).astype(bf), _cols_nh(a2).astype(bf),
                           _cols_nh(w0), _cols_nh(a0))
    head_params = [_param_lanes(x, batch) for x in (k_k, k_a, r_k.reshape(-1), gn_w, gn_b)]
    y4 = _scan(p3, g4, dec4, a4, head_params)
    w_out_perm = w_out.reshape(-1, HEAD_DIM, d).swapaxes(0, 1).reshape(d, d).astype(bf)
    out = _outproj_lanes(y4, w_out_perm, h2d.reshape(batch, seq, d))
    return out.reshape(batch * seq, d)


def _attn_kernel(sinks_ref, q_ref, kc_ref, kp_ref, vc_ref, vp_ref, g0_ref, g1_ref, g2_ref,
                 g3_ref, o_ref):
    bf = jnp.bfloat16
    n = HEAD_DIM
    pairs = GROUP // 2
    nq = pairs * BLOCK
    not_first = pl.program_id(1) > 0
    kj = lax.broadcasted_iota(jnp.int32, (2 * BLOCK, nq), 0)
    qi = lax.broadcasted_iota(jnp.int32, (2 * BLOCK, nq), 1) & (BLOCK - 1)
    delta = BLOCK + qi - kj
    mask = (delta >= 0) & (delta < BLOCK) & ((kj >= BLOCK) | not_first)
    sink_slot = kj == 0
    low_kv = lax.broadcasted_iota(jnp.int32, (2 * BLOCK, LANE), 1) < n
    key0_kv = lax.broadcasted_iota(jnp.int32, (2 * BLOCK, LANE), 0) == 0
    gate_refs = (g0_ref, g1_ref, g2_ref, g3_ref)
    gate_w = g0_ref.shape[1]
    for hk in range(KV_HEADS):
        cols = slice((hk // 2) * LANE, (hk // 2 + 1) * LANE)
        kt = jnp.concatenate([kp_ref[:, cols], kc_ref[:, cols]], axis=0)
        vt = jnp.concatenate([vp_ref[:, cols], vc_ref[:, cols]], axis=0)
        kr = pltpu.roll(kt, n, axis=1)
        vr = pltpu.roll(vt, n, axis=1)
        if hk % 2 == 0:
            k_low, k_high = jnp.where(low_kv, kt, 0.0), jnp.where(low_kv, 0.0, kr)
            v_both = jnp.where(low_kv, vt, vr)
        else:
            k_low, k_high = jnp.where(low_kv, kr, 0.0), jnp.where(low_kv, 0.0, kt)
            v_both = jnp.where(low_kv, vr, vt)
        v_t = jnp.where(key0_kv, 0.0, v_both).T.astype(bf)
        qs = jnp.concatenate([q_ref[:, (hk * pairs + p) * LANE:(hk * pairs + p + 1) * LANE]
                              for p in range(pairs)], axis=0)
        qs = (qs * (n ** -0.5)).astype(bf)
        outs = []
        for parity, k_sel in ((0, k_low), (1, k_high)):
            s = lax.dot_general(k_sel.astype(bf), qs, (((1,), (1,)), ((), ())),
                                preferred_element_type=jnp.float32)
            sink = jnp.concatenate(
                [jnp.full((1, BLOCK), sinks_ref[hk * GROUP + 2 * p + parity], jnp.float32)
                 for p in range(pairs)], axis=1)
            s = jnp.where(mask, s, jnp.where(sink_slot, sink, MASK_VALUE))
            m = jnp.max(s, axis=0, keepdims=True)
            e = jnp.exp(s - m)
            inv = 1.0 / jnp.sum(e, axis=0, keepdims=True)
            outs.append(jnp.dot(v_t, (e * inv).astype(bf), preferred_element_type=jnp.float32))
        for p in range(pairs):
            col = (hk * pairs + p) * LANE
            qcols = slice(p * BLOCK, (p + 1) * BLOCK)
            o = jnp.concatenate([outs[0][:n, qcols], outs[1][n:, qcols]], axis=0).T
            g_ref = gate_refs[col // gate_w]
            gate = g_ref[:, col % gate_w:col % gate_w + LANE]
            o_ref[:, col:col + LANE] = (o * (gate / (1.0 + jnp.exp(-gate)))).astype(o_ref.dtype)


def _attention(p, batch, seq, d, sinks):
    nb = seq // BLOCK
    kvw = KV_HEADS * HEAD_DIM
    gate_w = 2 * kvw
    row = lambda b, n: b * nb + n
    prow = lambda b, n: b * nb + jnp.maximum(n - 1, 0)
    kblk, vblk = d // kvw, d // kvw + 1
    gblk = (d + 2 * kvw) // gate_w
    in_specs = [pl.BlockSpec(memory_space=pltpu.SMEM),
                pl.BlockSpec((BLOCK, d), lambda b, n: (row(b, n), 0)),
                pl.BlockSpec((BLOCK, kvw), lambda b, n: (row(b, n), kblk)),
                pl.BlockSpec((BLOCK, kvw), lambda b, n: (prow(b, n), kblk)),
                pl.BlockSpec((BLOCK, kvw), lambda b, n: (row(b, n), vblk)),
                pl.BlockSpec((BLOCK, kvw), lambda b, n: (prow(b, n), vblk))]
    in_specs += [pl.BlockSpec((BLOCK, gate_w), lambda b, n, j=j: (row(b, n), gblk + j))
                 for j in range(4)]
    return pl.pallas_call(
        _attn_kernel,
        grid=(batch, nb),
        in_specs=in_specs,
        out_specs=pl.BlockSpec((BLOCK, d), lambda b, n: (row(b, n), 0)),
        out_shape=jax.ShapeDtypeStruct((batch * seq, d), jnp.bfloat16),
        compiler_params=_params("parallel", "parallel"),
        name="swa_attention",
    )(sinks, p, p, p, p, p, p, p, p, p)


def _attn_layer(h2d, batch, seq, norm_w, w_in, sinks, w_out):
    d = h2d.shape[1]
    bf = jnp.bfloat16
    hn = _rms_norm(h2d, norm_w, bf)
    p = _matmul(hn, w_in.astype(bf), tn=1024, name="attn_in_proj")
    o = _attention(p, batch, seq, d, sinks)
    return _matmul(o, w_out.astype(bf), residual=h2d, name="attn_out_proj")


def kernel(x, norm_w, final_norm_w, rwkv_w_in, rwkv_mu, rwkv_w0, rwkv_w2, rwkv_a0, rwkv_a2,
           rwkv_k_k, rwkv_k_a, rwkv_r_k, rwkv_gn_w, rwkv_gn_b, rwkv_w_out,
           attn_w_in, attn_sinks, attn_w_out):
    batch, seq, d = x.shape
    h = x.reshape(batch * seq, d)
    depth = norm_w.shape[0]
    for i in range(depth):
        j = i // 2
        if i % 2 == 0:
            h = _rwkv_layer(h, batch, seq, norm_w[i], rwkv_w_in[j], rwkv_mu[j], rwkv_w0[j],
                            rwkv_w2[j], rwkv_a0[j], rwkv_a2[j], rwkv_k_k[j], rwkv_k_a[j],
                            rwkv_r_k[j], rwkv_gn_w[j], rwkv_gn_b[j], rwkv_w_out[j])
        else:
            h = _attn_layer(h, batch, seq, norm_w[i], attn_w_in[j], attn_sinks[j],
                            attn_w_out[j])
    return _rms_norm(h, final_norm_w, jnp.float32).reshape(batch, seq, d)
```

```python
import functools

import jax
import jax.numpy as jnp
import numpy as np
from jax import lax
from jax.experimental import pallas as pl
from jax.experimental.pallas import tpu as pltpu

HEAD_DIM = 64
LORA = 128
KV_HEADS = 8
GROUP = 8
BLOCK = 128
NORM_EPS = 1e-5
GN_EPS = HEAD_DIM * 1e-5
L2_EPS = 1e-12
MASK_VALUE = -1e30

LANE = 128
SUBLANE = 8
VMEM_LIMIT_BYTES = 56 * 1024 * 1024
SCAN_TB = 16
PITCH = HEAD_DIM + SUBLANE
ROW_PITCH = HEAD_DIM + 4


def _params(*sem):
    return pltpu.CompilerParams(dimension_semantics=sem, vmem_limit_bytes=VMEM_LIMIT_BYTES)


def _rms_kernel(x_ref, g_ref, o_ref):
    x = x_ref[...]
    ms = jnp.mean(x * x, axis=-1, keepdims=True)
    o_ref[...] = (x * lax.rsqrt(ms + NORM_EPS) * g_ref[...]).astype(o_ref.dtype)


def _rms_norm(x2d, g, out_dtype, tm=256):
    m, d = x2d.shape
    return pl.pallas_call(
        _rms_kernel,
        grid=(m // tm,),
        in_specs=[pl.BlockSpec((tm, d), lambda i: (i, 0)),
                  pl.BlockSpec((1, d), lambda i: (0, 0))],
        out_specs=pl.BlockSpec((tm, d), lambda i: (i, 0)),
        out_shape=jax.ShapeDtypeStruct((m, d), out_dtype),
        compiler_params=_params("parallel"),
        name="rmsnorm",
    )(x2d, g.reshape(1, d))


def _mm_kernel(a_ref, b_ref, o_ref):
    o_ref[...] = jnp.dot(a_ref[...], b_ref[...], preferred_element_type=jnp.float32)


def _mm_res_kernel(a_ref, b_ref, r_ref, o_ref):
    o_ref[...] = r_ref[...] + jnp.dot(a_ref[...], b_ref[...],
                                      preferred_element_type=jnp.float32)


def _matmul(a, b, residual=None, tm=512, tn=1024, name="matmul"):
    m, k = a.shape
    _, n = b.shape
    tm = min(tm, m)
    grid = (n // tn, m // tm)
    in_specs = [pl.BlockSpec((tm, k), lambda j, i: (i, 0)),
                pl.BlockSpec((k, tn), lambda j, i: (0, j))]
    args = [a, b]
    kern = _mm_kernel
    if residual is not None:
        in_specs.append(pl.BlockSpec((tm, tn), lambda j, i: (i, j)))
        args.append(residual)
        kern = _mm_res_kernel
    return pl.pallas_call(
        kern,
        grid=grid,
        in_specs=in_specs,
        out_specs=pl.BlockSpec((tm, tn), lambda j, i: (i, j)),
        out_shape=jax.ShapeDtypeStruct((m, n), jnp.float32),
        compiler_params=_params("parallel", "parallel"),
        name=name,
    )(*args)


def _store_lanes(o_ref, lead, res, tm):
    low = lax.broadcasted_iota(jnp.int32, (tm, LANE), 1) < HEAD_DIM
    for bp in range(2):
        ra = res[(2 * bp) * tm:(2 * bp + 1) * tm]
        rb = res[(2 * bp + 1) * tm:(2 * bp + 2) * tm]
        for c in range(res.shape[1] // LANE):
            a = ra[:, c * LANE:(c + 1) * LANE]
            b = rb[:, c * LANE:(c + 1) * LANE]
            even = jnp.where(low, a, pltpu.roll(b, HEAD_DIM, axis=1))
            odd = jnp.where(low, pltpu.roll(a, HEAD_DIM, axis=1), b)
            lanes = pl.ds(bp * LANE, LANE)
            shape = (tm // SUBLANE, SUBLANE, LANE)
            o_ref[lead + (slice(None), 2 * c, slice(None), lanes)] = even.reshape(shape)
            o_ref[lead + (slice(None), 2 * c + 1, slice(None), lanes)] = odd.reshape(shape)


def _inproj_lanes_kernel(x_ref, g_ref, wx_ref, mu_ref, ws_ref, wg_ref,
                         xo_ref, ps_ref, pg_ref, hn_ref, carry_ref, *, shifted_blocks):
    nb, tm, d = x_ref.shape
    tn = ws_ref.shape[1]
    i, j = pl.program_id(0), pl.program_id(1)

    @pl.when(j == 0)
    def _():
        for b in range(nb):
            x = x_ref[b]
            ms = jnp.mean(x * x, axis=-1, keepdims=True)
            hn_ref[b * tm:(b + 1) * tm, :] = (
                x * lax.rsqrt(ms + NORM_EPS) * g_ref[...]).astype(hn_ref.dtype)
        xo_ref[...] = jnp.dot(hn_ref[...], wx_ref[...],
                              preferred_element_type=jnp.float32).reshape(xo_ref.shape)

    @pl.when(j < shifted_blocks)
    def _():
        res = jnp.dot(hn_ref[...], ws_ref[...], preferred_element_type=jnp.float32)
        prev = jnp.where(i == 0, 0.0, carry_ref[j])
        row0 = lax.broadcasted_iota(jnp.int32, (tm, tn), 0) == 0
        parts, lasts = [], []
        for b in range(nb):
            rb = res[b * tm:(b + 1) * tm]
            before = jnp.where(row0, prev[b:b + 1], pltpu.roll(rb, 1, axis=0))
            parts.append(rb + (before - rb) * mu_ref[...])
            lasts.append(rb[tm - 1:tm])
        carry_ref[j] = jnp.concatenate(lasts, axis=0)
        _store_lanes(ps_ref, (0,), jnp.concatenate(parts, axis=0), tm)

    @pl.when(j >= shifted_blocks)
    def _():
        res = jnp.dot(hn_ref[...], wg_ref[...], preferred_element_type=jnp.float32)
        _store_lanes(pg_ref, (), res, tm)


def _inproj_lanes(x3, norm_w, w_x, mu_s, w_shift, w_gate, tm=256, tn=512):
    batch, seq, d = x3.shape
    assert batch == 4 and d == 64 * HEAD_DIM
    nv = tn // HEAD_DIM
    per_seg = d // tn
    ns = w_shift.shape[1] // tn
    ng = w_gate.shape[1] // tn
    tile = (HEAD_DIM, SUBLANE, 2 * LANE)
    blk = (tm // SUBLANE, nv, SUBLANE, 2 * LANE)
    js = lambda j: jnp.minimum(j, ns - 1)
    jg = lambda j: jnp.maximum(j - ns, 0)
    return pl.pallas_call(
        functools.partial(_inproj_lanes_kernel, shifted_blocks=ns),
        grid=(seq // tm, ns + ng),
        in_specs=[pl.BlockSpec((batch, tm, d), lambda i, j: (0, i, 0),
                               pipeline_mode=pl.Buffered(1)),
                  pl.BlockSpec((1, d), lambda i, j: (0, 0)),
                  pl.BlockSpec((d, w_x.shape[1]), lambda i, j: (0, 0)),
                  pl.BlockSpec((1, tn), lambda i, j: (0, js(j))),
                  pl.BlockSpec((d, tn), lambda i, j: (0, js(j))),
                  pl.BlockSpec((d, tn), lambda i, j: (0, jg(j)))],
        out_specs=[pl.BlockSpec((batch, tm, w_x.shape[1]), lambda i, j: (0, i, 0)),
                   pl.BlockSpec((1,) + blk,
                                lambda i, j: (js(j) // per_seg, i, js(j) % per_seg, 0, 0)),
                   pl.BlockSpec(blk, lambda i, j: (i, jg(j), 0, 0))],
        out_shape=[jax.ShapeDtypeStruct((batch, seq, w_x.shape[1]), jnp.float32),
                   jax.ShapeDtypeStruct((ns // per_seg, seq // SUBLANE) + tile, jnp.float32),
                   jax.ShapeDtypeStruct((seq // SUBLANE,) + tile, jnp.float32)],
        scratch_shapes=[pltpu.VMEM((batch * tm, d), jnp.bfloat16),
                        pltpu.VMEM((ns, batch, tn), jnp.float32)],
        compiler_params=_params("arbitrary", "arbitrary"),
        name="rwkv_in_proj",
    )(x3, norm_w.reshape(1, d), w_x, mu_s.reshape(1, -1), w_shift, w_gate)


def _lora_kernel(x_ref, xp_ref, mu_ref, w2_ref, a2_ref, w0_ref, a0_ref, dec_ref, a_ref):
    nb, tm, _ = x_ref.shape
    first = pl.program_id(0) == 0
    row0 = lax.broadcasted_iota(jnp.int32, (tm, 2 * LORA), 0) == 0
    xw, xa = [], []
    for b in range(nb):
        x = x_ref[b]
        last = jnp.where(first, 0.0, xp_ref[b][SUBLANE - 1:SUBLANE])
        before = jnp.where(row0, last, pltpu.roll(x, 1, axis=0))
        x = x + (before - x) * mu_ref[...]
        xw.append(jnp.tanh(x[:, :LORA]).astype(jnp.bfloat16))
        xa.append(x[:, LORA:].astype(jnp.bfloat16))
    wl = w0_ref[...] + jnp.dot(jnp.concatenate(xw, axis=0), w2_ref[...],
                               preferred_element_type=jnp.float32)
    al = a0_ref[...] + jnp.dot(jnp.concatenate(xa, axis=0), a2_ref[...],
                               preferred_element_type=jnp.float32)
    sig_w = 1.0 / (1.0 + jnp.exp(-wl))
    _store_lanes(dec_ref, (), jnp.exp(sig_w * (-float(np.exp(-0.5)))), tm)
    _store_lanes(a_ref, (), 1.0 / (1.0 + jnp.exp(-al)), tm)


def _lora_lanes(x3, mu_x, w2p, a2p, w0p, a0p, tm=256, tn=512):
    batch, seq, _ = x3.shape
    d = w2p.shape[1]
    nv = tn // HEAD_DIM
    out = jax.ShapeDtypeStruct((seq // SUBLANE, HEAD_DIM, SUBLANE, 2 * LANE), jnp.float32)
    ospec = pl.BlockSpec((tm // SUBLANE, nv, SUBLANE, 2 * LANE), lambda i, j: (i, j, 0, 0))
    tq = tm // SUBLANE
    return pl.pallas_call(
        _lora_kernel,
        grid=(seq // tm, d // tn),
        in_specs=[pl.BlockSpec((batch, tm, 2 * LORA), lambda i, j: (0, i, 0)),
                  pl.BlockSpec((batch, SUBLANE, 2 * LORA),
                               lambda i, j: (0, jnp.maximum(i * tq - 1, 0), 0)),
                  pl.BlockSpec((1, 2 * LORA), lambda i, j: (0, 0)),
                  pl.BlockSpec((LORA, tn), lambda i, j: (0, j)),
                  pl.BlockSpec((LORA, tn), lambda i, j: (0, j)),
                  pl.BlockSpec((1, tn), lambda i, j: (0, j)),
                  pl.BlockSpec((1, tn), lambda i, j: (0, j))],
        out_specs=[ospec, ospec],
        out_shape=[out, out],
        compiler_params=_params("parallel", "arbitrary"),
        name="rwkv_lora",
    )(x3, x3, mu_x.reshape(1, -1), w2p, a2p, w0p.reshape(1, -1), a0p.reshape(1, -1))


def _scan_kernel(*refs, n_casts):
    (r_ref, k_ref, v_ref, g_ref, w_ref, a_ref,
     kkp_ref, kap_ref, rkp_ref, gnw_ref, gnb_ref) = refs[:11]
    cast_in = refs[11:11 + n_casts]
    o_ref = refs[11 + n_casts]
    cast_out = refs[12 + n_casts:12 + 2 * n_casts]
    s_ref, gcar_ref, bv_s, r_s, gam_s, k_s, kk_s, b_s, v_s, y_s = refs[12 + 2 * n_casts:]
    for src, dst in zip(cast_in, cast_out):
        dst[...] = src[...].astype(dst.dtype)
    tq = w_ref.shape[0]
    tb = tq * SUBLANE
    n = HEAD_DIM
    ch = SUBLANE

    @pl.when(pl.program_id(0) == 0)
    def _():
        s_ref[...] = jnp.zeros_like(s_ref)

    sub = lax.broadcasted_iota(jnp.int32, (ch, SUBLANE, 2 * LANE), 1)

    def to_steps(dst, pitch, q, c0, val):
        for c in range(ch):
            for slab in range(2):
                dst[slab, pl.ds(q * SUBLANE * pitch + c0 + c, SUBLANE, stride=pitch), :] = (
                    val[c, :, slab * LANE:(slab + 1) * LANE])

    def k_terms(q, cs):
        k = k_ref[0, q, cs]
        a = a_ref[q, cs]
        return k * kkp_ref[cs], k * (1.0 + (a - 1.0) * kap_ref[cs]), a

    for q in range(tq):
        def sums(j, carry, q=q):
            ssq, bon = carry
            cs = pl.ds(pl.multiple_of(j * ch, ch), ch)
            kkr, k2, _ = k_terms(q, cs)
            return ssq + kkr * kkr, bon + r_ref[0, q, cs] * k2 * rkp_ref[cs]

        zeros = jnp.zeros((ch, SUBLANE, 2 * LANE), jnp.float32)
        ssq, bon = lax.fori_loop(0, n // ch, sums, (zeros, zeros))
        nrm = jnp.sqrt(jnp.sum(ssq, axis=0, keepdims=True))
        inv_nrm = 1.0 / jnp.maximum(nrm, L2_EPS)
        bonus = jnp.sum(bon, axis=0, keepdims=True)

        def emit(j, carry, q=q, inv_nrm=inv_nrm, bonus=bonus):
            c0 = pl.multiple_of(j * ch, ch)
            cs = pl.ds(c0, ch)
            kkr, k2, a = k_terms(q, cs)
            v = v_ref[0, q, cs]
            kk = kkr * inv_nrm
            gam = w_ref[q, cs]
            for sh in (1, 2, 4):
                gam = gam * jnp.where(sub >= sh, pltpu.roll(gam, sh, axis=1), 1.0)
            before_tile = 1.0 if q == 0 else gcar_ref[cs]
            gam = gam * before_tile
            gam_prev = jnp.where(sub == 0, before_tile, pltpu.roll(gam, 1, axis=1))
            inv_gam = 1.0 / gam
            if q + 1 < tq:
                gcar_ref[cs] = jnp.broadcast_to(gam[:, SUBLANE - 1:SUBLANE, :], gam.shape)
            bv_s[q, cs] = bonus * v
            to_steps(r_s, ROW_PITCH, q, c0, r_ref[0, q, cs] * gam)
            to_steps(gam_s, ROW_PITCH, q, c0, gam)
            to_steps(k_s, ROW_PITCH, q, c0, k2 * inv_gam)
            to_steps(kk_s, ROW_PITCH, q, c0, kk * gam_prev)
            to_steps(b_s, ROW_PITCH, q, c0, kk * a * inv_gam)
            to_steps(v_s, PITCH, q, c0, v)
            return carry

        lax.fori_loop(0, n // ch, emit, 0)

    zero = jnp.zeros((n, LANE), jnp.float32)

    def row(ref, slab, i):
        return ref[slab, pl.ds(i, 1), :]

    for slab in range(2):
        def sa_first(c, acc, slab=slab):
            return acc + s_ref[slab, c] * row(kk_s, slab, c)

        def step(t, sa, slab=slab):
            base = t * ROW_PITCH
            ahead = jnp.minimum(t + 1, tb - 1) * ROW_PITCH
            tile = pl.ds(pl.multiple_of(t * PITCH, SUBLANE), n)
            vv = v_s[slab, tile, :]

            def channel(c, carry):
                y_acc, sa_next = carry
                s_new = (s_ref[slab, c] - sa * row(b_s, slab, base + c)
                         + vv * row(k_s, slab, base + c))
                s_ref[slab, c] = s_new
                return (y_acc + s_new * row(r_s, slab, base + c),
                        sa_next + s_new * row(kk_s, slab, ahead + c))

            y, sa_next = lax.fori_loop(0, n, channel, (zero, zero), unroll=True)
            y_s[slab, tile, :] = y
            return sa_next

        lax.fori_loop(0, tb, step, lax.fori_loop(0, n, sa_first, zero, unroll=8))

        def rescale(c, carry, slab=slab):
            s_ref[slab, c] = s_ref[slab, c] * row(gam_s, slab, (tb - 1) * ROW_PITCH + c)
            return carry

        lax.fori_loop(0, n, rescale, 0, unroll=8)

    for q in range(tq):
        for slab in range(2):
            lanes = pl.ds(slab * LANE, LANE)
            y = jnp.stack([y_s[slab, pl.ds(q * SUBLANE * PITCH + c, SUBLANE, stride=PITCH), :]
                           for c in range(n)], axis=0)
            mean = jnp.mean(y, axis=0, keepdims=True)
            yc = y - mean
            var = jnp.mean(yc * yc, axis=0, keepdims=True)
            y = yc * lax.rsqrt(var + GN_EPS) * gnw_ref[:, :, lanes] + gnb_ref[:, :, lanes]
            y = y + bv_s[q, :, :, lanes]
            g = g_ref[q, :, :, lanes]
            o_ref[q, :, :, lanes] = y * (g / (1.0 + jnp.exp(-g)))


def _scan(p3, g4, dec4, a4, head_params, side_casts=(), tb=SCAN_TB):
    _, tqs, n, _, lanes = p3.shape
    tq = min(tb // SUBLANE, tqs)
    steps = tqs // tq
    cast_specs = [pl.BlockSpec((w.shape[0] // steps, w.shape[1]), lambda i: (i, 0))
                  for w in side_casts]
    tile = (n, SUBLANE, lanes)
    seg = lambda s: pl.BlockSpec((1, tq) + tile, lambda i, s=s: (s, i, 0, 0, 0))
    blk = pl.BlockSpec((tq,) + tile, lambda i: (i, 0, 0, 0))
    par = pl.BlockSpec(tile, lambda i: (0, 0, 0))
    rows = (2, tq * SUBLANE * ROW_PITCH, LANE)
    tiles = (2, tq * SUBLANE * PITCH, LANE)
    scratch = [pltpu.VMEM((2, n, n, LANE), jnp.float32),
               pltpu.VMEM(tile, jnp.float32),
               pltpu.VMEM((tq,) + tile, jnp.float32)]
    scratch += [pltpu.VMEM(rows, jnp.float32) for _ in range(5)]
    scratch += [pltpu.VMEM(tiles, jnp.float32) for _ in range(2)]
    outs = pl.pallas_call(
        functools.partial(_scan_kernel, n_casts=len(side_casts)),
        grid=(steps,),
        in_specs=([seg(0), seg(1), seg(2), blk, blk, blk] + [par] * len(head_params)
                  + cast_specs),
        out_specs=[blk] + cast_specs,
        out_shape=[jax.ShapeDtypeStruct((tqs,) + tile, jnp.float32)]
        + [jax.ShapeDtypeStruct(w.shape, jnp.bfloat16) for w in side_casts],
        scratch_shapes=scratch,
        compiler_params=_params("arbitrary"),
        name="rwkv_scan",
    )(p3, p3, p3, g4, dec4, a4, *head_params, *side_casts)
    return outs[0], outs[1:]


def _outproj_lanes_kernel(y_ref, w_ref, r_ref, o_ref, lhs_ref):
    tq = y_ref.shape[0]
    tm = tq * SUBLANE
    nb, _, d = lhs_ref.shape

    @pl.when(pl.program_id(1) == 0)
    def _():
        low = lax.broadcasted_iota(jnp.int32, (tm, LANE), 1) < HEAD_DIM
        for c in range(HEAD_DIM // 2):
            cols = pl.ds(c * LANE, LANE)
            for bp in range(2):
                lanes = pl.ds(bp * LANE, LANE)
                e = y_ref[:, 2 * c, :, lanes].reshape(tm, LANE)
                o = y_ref[:, 2 * c + 1, :, lanes].reshape(tm, LANE)
                lhs_ref[2 * bp, :, cols] = jnp.where(
                    low, e, pltpu.roll(o, HEAD_DIM, axis=1)).astype(lhs_ref.dtype)
                lhs_ref[2 * bp + 1, :, cols] = jnp.where(
                    low, pltpu.roll(e, HEAD_DIM, axis=1), o).astype(lhs_ref.dtype)

    res = jnp.dot(lhs_ref[...].reshape(nb * tm, d), w_ref[...],
                  preferred_element_type=jnp.float32)
    o_ref[...] = r_ref[...] + res.reshape(o_ref.shape)


def _outproj_lanes(y4, w_perm, residual3, tm=256, tn=512):
    batch, seq, d = residual3.shape
    tq = tm // SUBLANE
    ospec = pl.BlockSpec((batch, tm, tn), lambda i, j: (0, i, j))
    return pl.pallas_call(
        _outproj_lanes_kernel,
        grid=(seq // tm, d // tn),
        in_specs=[pl.BlockSpec((tq, HEAD_DIM, SUBLANE, 2 * LANE), lambda i, j: (i, 0, 0, 0),
                               pipeline_mode=pl.Buffered(1)),
                  pl.BlockSpec((d, tn), lambda i, j: (0, j)),
                  ospec],
        out_specs=ospec,
        out_shape=jax.ShapeDtypeStruct((batch, seq, d), jnp.float32),
        scratch_shapes=[pltpu.VMEM((batch, tm, d), jnp.bfloat16)],
        compiler_params=_params("parallel", "arbitrary"),
        name="rwkv_out_proj",
    )(y4, w_perm, residual3)


def _cols_nh(w, segs=1):
    lead = w.shape[:-1]
    return w.reshape(lead + (segs, -1, HEAD_DIM)).swapaxes(-1, -2).reshape(w.shape)


def _param_lanes(p, batch):
    pt = jnp.tile(p.reshape(-1, HEAD_DIM).T, (1, batch))
    return jnp.broadcast_to(pt[:, None, :], (HEAD_DIM, SUBLANE, pt.shape[1]))


def _rwkv_layer(h2d, batch, seq, norm_w, w_in, mu, w0, w2, a0, a2, k_k, k_a, r_k, gn_w,
                gn_b, w_out, side_casts=()):
    d = h2d.shape[1]
    bf = jnp.bfloat16
    x_cols = slice(3 * d, 3 * d + 2 * LORA)
    x, p3, g4 = _inproj_lanes(h2d.reshape(batch, seq, d), norm_w, w_in[:, x_cols].astype(bf),
                              _cols_nh(mu[:3 * d], 3),
                              _cols_nh(w_in[:, :3 * d], 3).astype(bf),
                              _cols_nh(w_in[:, 3 * d + 2 * LORA:]).astype(bf))
    dec4, a4 = _lora_lanes(x, mu[x_cols],
                           _cols_nh(w2).astype(bf), _cols_nh(a2).astype(bf),
                           _cols_nh(w0), _cols_nh(a0))
    head_params = [_param_lanes(x, batch) for x in (k_k, k_a, r_k.reshape(-1), gn_w, gn_b)]
    y4, casted = _scan(p3, g4, dec4, a4, head_params, side_casts)
    w_out_perm = w_out.reshape(-1, HEAD_DIM, d).swapaxes(0, 1).reshape(d, d).astype(bf)
    out = _outproj_lanes(y4, w_out_perm, h2d.reshape(batch, seq, d))
    return out.reshape(batch * seq, d), casted


def _attn_kernel(sinks_ref, q_ref, kc_ref, kp_ref, vc_ref, vp_ref, g0_ref, g1_ref, g2_ref,
                 g3_ref, o_ref):
    bf = jnp.bfloat16
    n = HEAD_DIM
    pairs = GROUP // 2
    nq = pairs * BLOCK
    not_first = pl.program_id(1) > 0
    kj = lax.broadcasted_iota(jnp.int32, (2 * BLOCK, nq), 0)
    qi = lax.broadcasted_iota(jnp.int32, (2 * BLOCK, nq), 1) & (BLOCK - 1)
    delta = BLOCK + qi - kj
    mask = (delta >= 0) & (delta < BLOCK) & ((kj >= BLOCK) | not_first)
    sink_slot = kj == 0
    low_kv = lax.broadcasted_iota(jnp.int32, (2 * BLOCK, LANE), 1) < n
    key0_kv = lax.broadcasted_iota(jnp.int32, (2 * BLOCK, LANE), 0) == 0
    gate_refs = (g0_ref, g1_ref, g2_ref, g3_ref)
    gate_w = g0_ref.shape[1]
    for hk in range(KV_HEADS):
        cols = slice((hk // 2) * LANE, (hk // 2 + 1) * LANE)
        kt = jnp.concatenate([kp_ref[:, cols], kc_ref[:, cols]], axis=0)
        vt = jnp.concatenate([vp_ref[:, cols], vc_ref[:, cols]], axis=0)
        kr = pltpu.roll(kt, n, axis=1)
        vr = pltpu.roll(vt, n, axis=1)
        if hk % 2 == 0:
            k_low, k_high = jnp.where(low_kv, kt, 0.0), jnp.where(low_kv, 0.0, kr)
            v_both = jnp.where(low_kv, vt, vr)
        else:
            k_low, k_high = jnp.where(low_kv, kr, 0.0), jnp.where(low_kv, 0.0, kt)
            v_both = jnp.where(low_kv, vr, vt)
        v_t = jnp.where(key0_kv, 0.0, v_both).T.astype(bf)
        qs = jnp.concatenate([q_ref[:, (hk * pairs + p) * LANE:(hk * pairs + p + 1) * LANE]
                              for p in range(pairs)], axis=0)
        qs = (qs * (n ** -0.5)).astype(bf)
        outs = []
        for parity, k_sel in ((0, k_low), (1, k_high)):
            s = lax.dot_general(k_sel.astype(bf), qs, (((1,), (1,)), ((), ())),
                                preferred_element_type=jnp.float32)
            sink = jnp.concatenate(
                [jnp.full((1, BLOCK), sinks_ref[hk * GROUP + 2 * p + parity], jnp.float32)
                 for p in range(pairs)], axis=1)
            s = jnp.where(mask, s, jnp.where(sink_slot, sink, MASK_VALUE))
            m = jnp.max(s, axis=0, keepdims=True)
            e = jnp.exp(s - m)
            inv = 1.0 / jnp.sum(e, axis=0, keepdims=True)
            outs.append(jnp.dot(v_t, (e * inv).astype(bf), preferred_element_type=jnp.float32))
        for p in range(pairs):
            col = (hk * pairs + p) * LANE
            qcols = slice(p * BLOCK, (p + 1) * BLOCK)
            o = jnp.concatenate([outs[0][:n, qcols], outs[1][n:, qcols]], axis=0).T
            g_ref = gate_refs[col // gate_w]
            gate = g_ref[:, col % gate_w:col % gate_w + LANE]
            o_ref[:, col:col + LANE] = (o * (gate / (1.0 + jnp.exp(-gate)))).astype(o_ref.dtype)


def _attention(p, batch, seq, d, sinks):
    nb = seq // BLOCK
    kvw = KV_HEADS * HEAD_DIM
    gate_w = 2 * kvw
    row = lambda b, n: b * nb + n
    prow = lambda b, n: b * nb + jnp.maximum(n - 1, 0)
    kblk, vblk = d // kvw, d // kvw + 1
    gblk = (d + 2 * kvw) // gate_w
    in_specs = [pl.BlockSpec(memory_space=pltpu.SMEM),
                pl.BlockSpec((BLOCK, d), lambda b, n: (row(b, n), 0)),
                pl.BlockSpec((BLOCK, kvw), lambda b, n: (row(b, n), kblk)),
                pl.BlockSpec((BLOCK, kvw), lambda b, n: (prow(b, n), kblk)),
                pl.BlockSpec((BLOCK, kvw), lambda b, n: (row(b, n), vblk)),
                pl.BlockSpec((BLOCK, kvw), lambda b, n: (prow(b, n), vblk))]
    in_specs += [pl.BlockSpec((BLOCK, gate_w), lambda b, n, j=j: (row(b, n), gblk + j))
                 for j in range(4)]
    return pl.pallas_call(
        _attn_kernel,
        grid=(batch, nb),
        in_specs=in_specs,
        out_specs=pl.BlockSpec((BLOCK, d), lambda b, n: (row(b, n), 0)),
        out_shape=jax.ShapeDtypeStruct((batch * seq, d), jnp.bfloat16),
        compiler_params=_params("parallel", "parallel"),
        name="swa_attention",
    )(sinks, p, p, p, p, p, p, p, p, p)


def _attn_layer(h2d, batch, seq, norm_w, w_in_bf, sinks, w_out_bf):
    d = h2d.shape[1]
    hn = _rms_norm(h2d, norm_w, jnp.bfloat16)
    p = _matmul(hn, w_in_bf, tn=1024, name="attn_in_proj")
    o = _attention(p, batch, seq, d, sinks)
    return _matmul(o, w_out_bf, residual=h2d, name="attn_out_proj")


def kernel(x, norm_w, final_norm_w, rwkv_w_in, rwkv_mu, rwkv_w0, rwkv_w2, rwkv_a0, rwkv_a2,
           rwkv_k_k, rwkv_k_a, rwkv_r_k, rwkv_gn_w, rwkv_gn_b, rwkv_w_out,
           attn_w_in, attn_sinks, attn_w_out):
    batch, seq, d = x.shape
    h = x.reshape(batch * seq, d)
    depth = norm_w.shape[0]
    bf_weights = {}
    for i in range(depth):
        j = i // 2
        if i % 2 == 0:
            ahead = (attn_w_in[j], attn_w_out[j]) if i + 1 < depth else ()
            h, casted = _rwkv_layer(h, batch, seq, norm_w[i], rwkv_w_in[j], rwkv_mu[j],
                                    rwkv_w0[j], rwkv_w2[j], rwkv_a0[j], rwkv_a2[j], rwkv_k_k[j],
                                    rwkv_k_a[j], rwkv_r_k[j], rwkv_gn_w[j], rwkv_gn_b[j],
                                    rwkv_w_out[j], ahead)
            if ahead:
                bf_weights[i + 1] = casted
        else:
            w_in_bf, w_out_bf = bf_weights.get(i) or (attn_w_in[j].astype(jnp.bfloat16),
                                                      attn_w_out[j].astype(jnp.bfloat16))
            h = _attn_layer(h, batch, seq, norm_w[i], w_in_bf, attn_sinks[j], w_out_bf)
    return _rms_norm(h, final_norm_w, jnp.float32).reshape(batch, seq, d)
```

```python
import functools

import jax
import jax.numpy as jnp
import numpy as np
from jax import lax
from jax.experimental import pallas as pl
from jax.experimental.pallas import tpu as pltpu

HEAD_DIM = 64
LORA = 128
KV_HEADS = 8
GROUP = 8
BLOCK = 128
NORM_EPS = 1e-5
GN_EPS = HEAD_DIM * 1e-5
L2_EPS = 1e-12
MASK_VALUE = -1e30

LANE = 128
SUBLANE = 8
VMEM_LIMIT_BYTES = 56 * 1024 * 1024
SCAN_TB = 16
PITCH = HEAD_DIM + SUBLANE
ROW_PITCH = HEAD_DIM + 4


def _params(*sem):
    return pltpu.CompilerParams(dimension_semantics=sem, vmem_limit_bytes=VMEM_LIMIT_BYTES)


def _rms_kernel(x_ref, g_ref, o_ref):
    x = x_ref[...]
    ms = jnp.mean(x * x, axis=-1, keepdims=True)
    o_ref[...] = (x * lax.rsqrt(ms + NORM_EPS) * g_ref[...]).astype(o_ref.dtype)


def _rms_norm(x2d, g, out_dtype, tm=256):
    m, d = x2d.shape
    return pl.pallas_call(
        _rms_kernel,
        grid=(m // tm,),
        in_specs=[pl.BlockSpec((tm, d), lambda i: (i, 0)),
                  pl.BlockSpec((1, d), lambda i: (0, 0))],
        out_specs=pl.BlockSpec((tm, d), lambda i: (i, 0)),
        out_shape=jax.ShapeDtypeStruct((m, d), out_dtype),
        compiler_params=_params("parallel"),
        name="rmsnorm",
    )(x2d, g.reshape(1, d))


def _mm_kernel(a_ref, b_ref, o_ref):
    o_ref[...] = jnp.dot(a_ref[...], b_ref[...], preferred_element_type=jnp.float32)


def _mm_res_kernel(a_ref, b_ref, r_ref, o_ref):
    o_ref[...] = r_ref[...] + jnp.dot(a_ref[...], b_ref[...],
                                      preferred_element_type=jnp.float32)


def _matmul(a, b, residual=None, tm=512, tn=1024, name="matmul"):
    m, k = a.shape
    _, n = b.shape
    tm = min(tm, m)
    grid = (n // tn, m // tm)
    in_specs = [pl.BlockSpec((tm, k), lambda j, i: (i, 0)),
                pl.BlockSpec((k, tn), lambda j, i: (0, j))]
    args = [a, b]
    kern = _mm_kernel
    if residual is not None:
        in_specs.append(pl.BlockSpec((tm, tn), lambda j, i: (i, j)))
        args.append(residual)
        kern = _mm_res_kernel
    return pl.pallas_call(
        kern,
        grid=grid,
        in_specs=in_specs,
        out_specs=pl.BlockSpec((tm, tn), lambda j, i: (i, j)),
        out_shape=jax.ShapeDtypeStruct((m, n), jnp.float32),
        compiler_params=_params("parallel", "parallel"),
        name=name,
    )(*args)


def _store_lanes(o_ref, lead, res, tm):
    low = lax.broadcasted_iota(jnp.int32, (tm, LANE), 1) < HEAD_DIM
    for bp in range(2):
        ra = res[(2 * bp) * tm:(2 * bp + 1) * tm]
        rb = res[(2 * bp + 1) * tm:(2 * bp + 2) * tm]
        for c in range(res.shape[1] // LANE):
            a = ra[:, c * LANE:(c + 1) * LANE]
            b = rb[:, c * LANE:(c + 1) * LANE]
            even = jnp.where(low, a, pltpu.roll(b, HEAD_DIM, axis=1))
            odd = jnp.where(low, pltpu.roll(a, HEAD_DIM, axis=1), b)
            lanes = pl.ds(bp * LANE, LANE)
            shape = (tm // SUBLANE, SUBLANE, LANE)
            o_ref[lead + (slice(None), 2 * c, slice(None), lanes)] = even.reshape(shape)
            o_ref[lead + (slice(None), 2 * c + 1, slice(None), lanes)] = odd.reshape(shape)


def _inproj_lanes_kernel(x_ref, g_ref, wx_ref, mu_ref, ws_ref, wg_ref,
                         xo_ref, ps_ref, pg_ref, hn_ref, carry_ref, *, shifted_blocks):
    nb, tm, d = x_ref.shape
    tn = ws_ref.shape[0]
    i, j = pl.program_id(0), pl.program_id(1)

    @pl.when(j == 0)
    def _():
        for b in range(nb):
            x = x_ref[b]
            ms = jnp.mean(x * x, axis=-1, keepdims=True)
            hn_ref[b * tm:(b + 1) * tm, :] = (
                x * lax.rsqrt(ms + NORM_EPS) * g_ref[...]).astype(hn_ref.dtype)
        xo_ref[...] = jnp.dot(hn_ref[...], wx_ref[...],
                              preferred_element_type=jnp.float32).reshape(xo_ref.shape)

    @pl.when(j < shifted_blocks)
    def _():
        res = lax.dot_general(hn_ref[...], ws_ref[...], (((1,), (1,)), ((), ())),
                              preferred_element_type=jnp.float32)
        prev = jnp.where(i == 0, 0.0, carry_ref[j])
        row0 = lax.broadcasted_iota(jnp.int32, (tm, tn), 0) == 0
        parts, lasts = [], []
        for b in range(nb):
            rb = res[b * tm:(b + 1) * tm]
            before = jnp.where(row0, prev[b:b + 1], pltpu.roll(rb, 1, axis=0))
            parts.append(rb + (before - rb) * mu_ref[...])
            lasts.append(rb[tm - 1:tm])
        carry_ref[j] = jnp.concatenate(lasts, axis=0)
        _store_lanes(ps_ref, (0,), jnp.concatenate(parts, axis=0), tm)

    @pl.when(j >= shifted_blocks)
    def _():
        res = lax.dot_general(hn_ref[...], wg_ref[...], (((1,), (1,)), ((), ())),
                              preferred_element_type=jnp.float32)
        _store_lanes(pg_ref, (), res, tm)


def _inproj_lanes(x3, norm_w, w_x, mu_s, w_shift, w_gate, tm=256, tn=512):
    batch, seq, d = x3.shape
    assert batch == 4 and d == 64 * HEAD_DIM
    nv = tn // HEAD_DIM
    per_seg = d // tn
    ns = w_shift.shape[0] // tn
    ng = w_gate.shape[0] // tn
    tile = (HEAD_DIM, SUBLANE, 2 * LANE)
    blk = (tm // SUBLANE, nv, SUBLANE, 2 * LANE)
    js = lambda j: jnp.minimum(j, ns - 1)
    jg = lambda j: jnp.maximum(j - ns, 0)
    return pl.pallas_call(
        functools.partial(_inproj_lanes_kernel, shifted_blocks=ns),
        grid=(seq // tm, ns + ng),
        in_specs=[pl.BlockSpec((batch, tm, d), lambda i, j: (0, i, 0),
                               pipeline_mode=pl.Buffered(1)),
                  pl.BlockSpec((1, d), lambda i, j: (0, 0)),
                  pl.BlockSpec((d, w_x.shape[1]), lambda i, j: (0, 0)),
                  pl.BlockSpec((1, tn), lambda i, j: (0, js(j))),
                  pl.BlockSpec((tn, d), lambda i, j: (js(j), 0)),
                  pl.BlockSpec((tn, d), lambda i, j: (jg(j), 0))],
        out_specs=[pl.BlockSpec((batch, tm, w_x.shape[1]), lambda i, j: (0, i, 0)),
                   pl.BlockSpec((1,) + blk,
                                lambda i, j: (js(j) // per_seg, i, js(j) % per_seg, 0, 0)),
                   pl.BlockSpec(blk, lambda i, j: (i, jg(j), 0, 0))],
        out_shape=[jax.ShapeDtypeStruct((batch, seq, w_x.shape[1]), jnp.float32),
                   jax.ShapeDtypeStruct((ns // per_seg, seq // SUBLANE) + tile, jnp.float32),
                   jax.ShapeDtypeStruct((seq // SUBLANE,) + tile, jnp.float32)],
        scratch_shapes=[pltpu.VMEM((batch * tm, d), jnp.bfloat16),
                        pltpu.VMEM((ns, batch, tn), jnp.float32)],
        compiler_params=_params("arbitrary", "arbitrary"),
        name="rwkv_in_proj",
    )(x3, norm_w.reshape(1, d), w_x, mu_s.reshape(1, -1), w_shift, w_gate)


def _lora_kernel(x_ref, xp_ref, mu_ref, w2_ref, a2_ref, w0_ref, a0_ref, dec_ref, a_ref):
    nb, tm, _ = x_ref.shape
    first = pl.program_id(0) == 0
    row0 = lax.broadcasted_iota(jnp.int32, (tm, 2 * LORA), 0) == 0
    xw, xa = [], []
    for b in range(nb):
        x = x_ref[b]
        last = jnp.where(first, 0.0, xp_ref[b][SUBLANE - 1:SUBLANE])
        before = jnp.where(row0, last, pltpu.roll(x, 1, axis=0))
        x = x + (before - x) * mu_ref[...]
        xw.append(jnp.tanh(x[:, :LORA]).astype(jnp.bfloat16))
        xa.append(x[:, LORA:].astype(jnp.bfloat16))
    wl = w0_ref[...] + jnp.dot(jnp.concatenate(xw, axis=0), w2_ref[...],
                               preferred_element_type=jnp.float32)
    al = a0_ref[...] + jnp.dot(jnp.concatenate(xa, axis=0), a2_ref[...],
                               preferred_element_type=jnp.float32)
    sig_w = 1.0 / (1.0 + jnp.exp(-wl))
    _store_lanes(dec_ref, (), jnp.exp(sig_w * (-float(np.exp(-0.5)))), tm)
    _store_lanes(a_ref, (), 1.0 / (1.0 + jnp.exp(-al)), tm)


def _lora_lanes(x3, mu_x, w2p, a2p, w0p, a0p, tm=256, tn=512):
    batch, seq, _ = x3.shape
    d = w2p.shape[1]
    nv = tn // HEAD_DIM
    out = jax.ShapeDtypeStruct((seq // SUBLANE, HEAD_DIM, SUBLANE, 2 * LANE), jnp.float32)
    ospec = pl.BlockSpec((tm // SUBLANE, nv, SUBLANE, 2 * LANE), lambda i, j: (i, j, 0, 0))
    tq = tm // SUBLANE
    return pl.pallas_call(
        _lora_kernel,
        grid=(seq // tm, d // tn),
        in_specs=[pl.BlockSpec((batch, tm, 2 * LORA), lambda i, j: (0, i, 0)),
                  pl.BlockSpec((batch, SUBLANE, 2 * LORA),
                               lambda i, j: (0, jnp.maximum(i * tq - 1, 0), 0)),
                  pl.BlockSpec((1, 2 * LORA), lambda i, j: (0, 0)),
                  pl.BlockSpec((LORA, tn), lambda i, j: (0, j)),
                  pl.BlockSpec((LORA, tn), lambda i, j: (0, j)),
                  pl.BlockSpec((1, tn), lambda i, j: (0, j)),
                  pl.BlockSpec((1, tn), lambda i, j: (0, j))],
        out_specs=[ospec, ospec],
        out_shape=[out, out],
        compiler_params=_params("parallel", "arbitrary"),
        name="rwkv_lora",
    )(x3, x3, mu_x.reshape(1, -1), w2p, a2p, w0p.reshape(1, -1), a0p.reshape(1, -1))


def _scan_kernel(*refs, n_casts):
    (r_ref, k_ref, v_ref, g_ref, w_ref, a_ref,
     kkp_ref, kap_ref, rkp_ref, gnw_ref, gnb_ref) = refs[:11]
    cast_in = refs[11:11 + n_casts]
    o_ref = refs[11 + n_casts]
    cast_out = refs[12 + n_casts:12 + 2 * n_casts]
    s_ref, gcar_ref, bv_s, r_s, gam_s, k_s, kk_s, b_s, v_s, y_s = refs[12 + 2 * n_casts:]
    for src, dst in zip(cast_in, cast_out):
        dst[...] = src[...].astype(dst.dtype)
    tq = w_ref.shape[0]
    tb = tq * SUBLANE
    n = HEAD_DIM
    ch = SUBLANE

    @pl.when(pl.program_id(0) == 0)
    def _():
        s_ref[...] = jnp.zeros_like(s_ref)

    sub = lax.broadcasted_iota(jnp.int32, (ch, SUBLANE, 2 * LANE), 1)

    def to_steps(dst, pitch, q, c0, val):
        for c in range(ch):
            for slab in range(2):
                dst[slab, pl.ds(q * SUBLANE * pitch + c0 + c, SUBLANE, stride=pitch), :] = (
                    val[c, :, slab * LANE:(slab + 1) * LANE])

    def k_terms(q, cs):
        k = k_ref[0, q, cs]
        a = a_ref[q, cs]
        return k * kkp_ref[cs], k * (1.0 + (a - 1.0) * kap_ref[cs]), a

    for q in range(tq):
        def sums(j, carry, q=q):
            ssq, bon = carry
            cs = pl.ds(pl.multiple_of(j * ch, ch), ch)
            kkr, k2, _ = k_terms(q, cs)
            return ssq + kkr * kkr, bon + r_ref[0, q, cs] * k2 * rkp_ref[cs]

        zeros = jnp.zeros((ch, SUBLANE, 2 * LANE), jnp.float32)
        ssq, bon = lax.fori_loop(0, n // ch, sums, (zeros, zeros))
        nrm = jnp.sqrt(jnp.sum(ssq, axis=0, keepdims=True))
        inv_nrm = 1.0 / jnp.maximum(nrm, L2_EPS)
        bonus = jnp.sum(bon, axis=0, keepdims=True)

        def emit(j, carry, q=q, inv_nrm=inv_nrm, bonus=bonus):
            c0 = pl.multiple_of(j * ch, ch)
            cs = pl.ds(c0, ch)
            kkr, k2, a = k_terms(q, cs)
            v = v_ref[0, q, cs]
            kk = kkr * inv_nrm
            gam = w_ref[q, cs]
            for sh in (1, 2, 4):
                gam = gam * jnp.where(sub >= sh, pltpu.roll(gam, sh, axis=1), 1.0)
            before_tile = 1.0 if q == 0 else gcar_ref[cs]
            gam = gam * before_tile
            gam_prev = jnp.where(sub == 0, before_tile, pltpu.roll(gam, 1, axis=1))
            inv_gam = 1.0 / gam
            if q + 1 < tq:
                gcar_ref[cs] = jnp.broadcast_to(gam[:, SUBLANE - 1:SUBLANE, :], gam.shape)
            bv_s[q, cs] = bonus * v
            to_steps(r_s, ROW_PITCH, q, c0, r_ref[0, q, cs] * gam)
            to_steps(gam_s, ROW_PITCH, q, c0, gam)
            to_steps(k_s, ROW_PITCH, q, c0, k2 * inv_gam)
            to_steps(kk_s, ROW_PITCH, q, c0, kk * gam_prev)
            to_steps(b_s, ROW_PITCH, q, c0, kk * a * inv_gam)
            to_steps(v_s, PITCH, q, c0, v)
            return carry

        lax.fori_loop(0, n // ch, emit, 0)

    zero = jnp.zeros((n, LANE), jnp.float32)

    def row(ref, slab, i):
        return ref[slab, pl.ds(i, 1), :]

    for slab in range(2):
        def sa_first(c, acc, slab=slab):
            return acc + s_ref[slab, c] * row(kk_s, slab, c)

        def step(t, sa, slab=slab):
            base = t * ROW_PITCH
            ahead = jnp.minimum(t + 1, tb - 1) * ROW_PITCH
            tile = pl.ds(pl.multiple_of(t * PITCH, SUBLANE), n)
            vv = v_s[slab, tile, :]

            def channel(c, carry):
                y_acc, sa_next = carry
                s_new = (s_ref[slab, c] - sa * row(b_s, slab, base + c)
                         + vv * row(k_s, slab, base + c))
                s_ref[slab, c] = s_new
                return (y_acc + s_new * row(r_s, slab, base + c),
                        sa_next + s_new * row(kk_s, slab, ahead + c))

            y, sa_next = lax.fori_loop(0, n, channel, (zero, zero), unroll=True)
            y_s[slab, tile, :] = y
            return sa_next

        lax.fori_loop(0, tb, step, lax.fori_loop(0, n, sa_first, zero, unroll=8))

        def rescale(c, carry, slab=slab):
            s_ref[slab, c] = s_ref[slab, c] * row(gam_s, slab, (tb - 1) * ROW_PITCH + c)
            return carry

        lax.fori_loop(0, n, rescale, 0, unroll=8)

    for q in range(tq):
        for slab in range(2):
            lanes = pl.ds(slab * LANE, LANE)
            y = jnp.stack([y_s[slab, pl.ds(q * SUBLANE * PITCH + c, SUBLANE, stride=PITCH), :]
                           for c in range(n)], axis=0)
            mean = jnp.mean(y, axis=0, keepdims=True)
            yc = y - mean
            var = jnp.mean(yc * yc, axis=0, keepdims=True)
            y = yc * lax.rsqrt(var + GN_EPS) * gnw_ref[:, :, lanes] + gnb_ref[:, :, lanes]
            y = y + bv_s[q, :, :, lanes]
            g = g_ref[q, :, :, lanes]
            o_ref[q, :, :, lanes] = y * (g / (1.0 + jnp.exp(-g)))


def _scan(p3, g4, dec4, a4, head_params, side_casts=(), tb=SCAN_TB):
    _, tqs, n, _, lanes = p3.shape
    tq = min(tb // SUBLANE, tqs)
    steps = tqs // tq
    cast_specs = [pl.BlockSpec((w.shape[0] // steps, w.shape[1]), lambda i: (i, 0))
                  for w in side_casts]
    tile = (n, SUBLANE, lanes)
    seg = lambda s: pl.BlockSpec((1, tq) + tile, lambda i, s=s: (s, i, 0, 0, 0))
    blk = pl.BlockSpec((tq,) + tile, lambda i: (i, 0, 0, 0))
    par = pl.BlockSpec(tile, lambda i: (0, 0, 0))
    rows = (2, tq * SUBLANE * ROW_PITCH, LANE)
    tiles = (2, tq * SUBLANE * PITCH, LANE)
    scratch = [pltpu.VMEM((2, n, n, LANE), jnp.float32),
               pltpu.VMEM(tile, jnp.float32),
               pltpu.VMEM((tq,) + tile, jnp.float32)]
    scratch += [pltpu.VMEM(rows, jnp.float32) for _ in range(5)]
    scratch += [pltpu.VMEM(tiles, jnp.float32) for _ in range(2)]
    outs = pl.pallas_call(
        functools.partial(_scan_kernel, n_casts=len(side_casts)),
        grid=(steps,),
        in_specs=([seg(0), seg(1), seg(2), blk, blk, blk] + [par] * len(head_params)
                  + cast_specs),
        out_specs=[blk] + cast_specs,
        out_shape=[jax.ShapeDtypeStruct((tqs,) + tile, jnp.float32)]
        + [jax.ShapeDtypeStruct(w.shape, jnp.bfloat16) for w in side_casts],
        scratch_shapes=scratch,
        compiler_params=_params("arbitrary"),
        name="rwkv_scan",
    )(p3, p3, p3, g4, dec4, a4, *head_params, *side_casts)
    return outs[0], outs[1:]


def _outproj_lanes_kernel(y_ref, w_ref, r_ref, o_ref, lhs_ref):
    tq = y_ref.shape[0]
    tm = tq * SUBLANE
    nb, _, d = lhs_ref.shape

    @pl.when(pl.program_id(1) == 0)
    def _():
        low = lax.broadcasted_iota(jnp.int32, (tm, LANE), 1) < HEAD_DIM
        for c in range(HEAD_DIM // 2):
            cols = pl.ds(c * LANE, LANE)
            for bp in range(2):
                lanes = pl.ds(bp * LANE, LANE)
                e = y_ref[:, 2 * c, :, lanes].reshape(tm, LANE)
                o = y_ref[:, 2 * c + 1, :, lanes].reshape(tm, LANE)
                lhs_ref[2 * bp, :, cols] = jnp.where(
                    low, e, pltpu.roll(o, HEAD_DIM, axis=1)).astype(lhs_ref.dtype)
                lhs_ref[2 * bp + 1, :, cols] = jnp.where(
                    low, pltpu.roll(e, HEAD_DIM, axis=1), o).astype(lhs_ref.dtype)

    res = jnp.dot(lhs_ref[...].reshape(nb * tm, d), w_ref[...],
                  preferred_element_type=jnp.float32)
    o_ref[...] = r_ref[...] + res.reshape(o_ref.shape)


def _outproj_lanes(y4, w_perm, residual3, tm=256, tn=512):
    batch, seq, d = residual3.shape
    tq = tm // SUBLANE
    ospec = pl.BlockSpec((batch, tm, tn), lambda i, j: (0, i, j))
    return pl.pallas_call(
        _outproj_lanes_kernel,
        grid=(seq // tm, d // tn),
        in_specs=[pl.BlockSpec((tq, HEAD_DIM, SUBLANE, 2 * LANE), lambda i, j: (i, 0, 0, 0),
                               pipeline_mode=pl.Buffered(1)),
                  pl.BlockSpec((d, tn), lambda i, j: (0, j)),
                  ospec],
        out_specs=ospec,
        out_shape=jax.ShapeDtypeStruct((batch, seq, d), jnp.float32),
        scratch_shapes=[pltpu.VMEM((batch, tm, d), jnp.bfloat16)],
        compiler_params=_params("parallel", "arbitrary"),
        name="rwkv_out_proj",
    )(y4, w_perm, residual3)


def _cols_nh(w, segs=1):
    lead = w.shape[:-1]
    return w.reshape(lead + (segs, -1, HEAD_DIM)).swapaxes(-1, -2).reshape(w.shape)


def _rows_nh(w, segs=1):
    k = w.shape[-1]
    return w.reshape(segs, -1, HEAD_DIM, k).swapaxes(1, 2).reshape(w.shape)


def _param_lanes(p, batch):
    pt = jnp.tile(p.reshape(-1, HEAD_DIM).T, (1, batch))
    return jnp.broadcast_to(pt[:, None, :], (HEAD_DIM, SUBLANE, pt.shape[1]))


def _rwkv_layer(h2d, batch, seq, norm_w, w_in, mu, w0, w2, a0, a2, k_k, k_a, r_k, gn_w,
                gn_b, w_out, side_casts=()):
    d = h2d.shape[1]
    bf = jnp.bfloat16
    x_cols = slice(3 * d, 3 * d + 2 * LORA)
    w_t = w_in.astype(bf).T
    x, p3, g4 = _inproj_lanes(h2d.reshape(batch, seq, d), norm_w, w_in[:, x_cols].astype(bf),
                              _cols_nh(mu[:3 * d], 3),
                              _rows_nh(w_t[:3 * d], 3), _rows_nh(w_t[3 * d + 2 * LORA:]))
    dec4, a4 = _lora_lanes(x, mu[x_cols],
                           _cols_nh(w2).astype(bf), _cols_nh(a2).astype(bf),
                           _cols_nh(w0), _cols_nh(a0))
    head_params = [_param_lanes(x, batch) for x in (k_k, k_a, r_k.reshape(-1), gn_w, gn_b)]
    w_out_perm = w_out.reshape(-1, HEAD_DIM, d).swapaxes(0, 1).reshape(d, d)
    y4, casted = _scan(p3, g4, dec4, a4, head_params, (w_out_perm,) + tuple(side_casts))
    out = _outproj_lanes(y4, casted[0], h2d.reshape(batch, seq, d))
    return out.reshape(batch * seq, d), casted[1:]


def _attn_kernel(sinks_ref, q_ref, kc_ref, kp_ref, vc_ref, vp_ref, g0_ref, g1_ref, g2_ref,
                 g3_ref, o_ref):
    bf = jnp.bfloat16
    n = HEAD_DIM
    pairs = GROUP // 2
    nq = pairs * BLOCK
    not_first = pl.program_id(1) > 0
    kj = lax.broadcasted_iota(jnp.int32, (2 * BLOCK, nq), 0)
    qi = lax.broadcasted_iota(jnp.int32, (2 * BLOCK, nq), 1) & (BLOCK - 1)
    delta = BLOCK + qi - kj
    mask = (delta >= 0) & (delta < BLOCK) & ((kj >= BLOCK) | not_first)
    sink_slot = kj == 0
    low_kv = lax.broadcasted_iota(jnp.int32, (2 * BLOCK, LANE), 1) < n
    key0_kv = lax.broadcasted_iota(jnp.int32, (2 * BLOCK, LANE), 0) == 0
    gate_refs = (g0_ref, g1_ref, g2_ref, g3_ref)
    gate_w = g0_ref.shape[1]
    for hk in range(KV_HEADS):
        cols = slice((hk // 2) * LANE, (hk // 2 + 1) * LANE)
        kt = jnp.concatenate([kp_ref[:, cols], kc_ref[:, cols]], axis=0)
        vt = jnp.concatenate([vp_ref[:, cols], vc_ref[:, cols]], axis=0)
        kr = pltpu.roll(kt, n, axis=1)
        vr = pltpu.roll(vt, n, axis=1)
        if hk % 2 == 0:
            k_low, k_high = jnp.where(low_kv, kt, 0.0), jnp.where(low_kv, 0.0, kr)
            v_both = jnp.where(low_kv, vt, vr)
        else:
            k_low, k_high = jnp.where(low_kv, kr, 0.0), jnp.where(low_kv, 0.0, kt)
            v_both = jnp.where(low_kv, vr, vt)
        v_t = jnp.where(key0_kv, 0.0, v_both).T.astype(bf)
        qs = jnp.concatenate([q_ref[:, (hk * pairs + p) * LANE:(hk * pairs + p + 1) * LANE]
                              for p in range(pairs)], axis=0)
        qs = (qs * (n ** -0.5)).astype(bf)
        outs = []
        for parity, k_sel in ((0, k_low), (1, k_high)):
            s = lax.dot_general(k_sel.astype(bf), qs, (((1,), (1,)), ((), ())),
                                preferred_element_type=jnp.float32)
            sink = jnp.concatenate(
                [jnp.full((1, BLOCK), sinks_ref[hk * GROUP + 2 * p + parity], jnp.float32)
                 for p in range(pairs)], axis=1)
            s = jnp.where(mask, s, jnp.where(sink_slot, sink, MASK_VALUE))
            m = jnp.max(s, axis=0, keepdims=True)
            e = jnp.exp(s - m)
            inv = 1.0 / jnp.sum(e, axis=0, keepdims=True)
            outs.append(jnp.dot(v_t, (e * inv).astype(bf), preferred_element_type=jnp.float32))
        for p in range(pairs):
            col = (hk * pairs + p) * LANE
            qcols = slice(p * BLOCK, (p + 1) * BLOCK)
            o = jnp.concatenate([outs[0][:n, qcols], outs[1][n:, qcols]], axis=0).T
            g_ref = gate_refs[col // gate_w]
            gate = g_ref[:, col % gate_w:col % gate_w + LANE]
            o_ref[:, col:col + LANE] = (o * (gate / (1.0 + jnp.exp(-gate)))).astype(o_ref.dtype)


def _attention(p, batch, seq, d, sinks):
    nb = seq // BLOCK
    kvw = KV_HEADS * HEAD_DIM
    gate_w = 2 * kvw
    row = lambda b, n: b * nb + n
    prow = lambda b, n: b * nb + jnp.maximum(n - 1, 0)
    kblk, vblk = d // kvw, d // kvw + 1
    gblk = (d + 2 * kvw) // gate_w
    in_specs = [pl.BlockSpec(memory_space=pltpu.SMEM),
                pl.BlockSpec((BLOCK, d), lambda b, n: (row(b, n), 0)),
                pl.BlockSpec((BLOCK, kvw), lambda b, n: (row(b, n), kblk)),
                pl.BlockSpec((BLOCK, kvw), lambda b, n: (prow(b, n), kblk)),
                pl.BlockSpec((BLOCK, kvw), lambda b, n: (row(b, n), vblk)),
                pl.BlockSpec((BLOCK, kvw), lambda b, n: (prow(b, n), vblk))]
    in_specs += [pl.BlockSpec((BLOCK, gate_w), lambda b, n, j=j: (row(b, n), gblk + j))
                 for j in range(4)]
    return pl.pallas_call(
        _attn_kernel,
        grid=(batch, nb),
        in_specs=in_specs,
        out_specs=pl.BlockSpec((BLOCK, d), lambda b, n: (row(b, n), 0)),
        out_shape=jax.ShapeDtypeStruct((batch * seq, d), jnp.bfloat16),
        compiler_params=_params("parallel", "parallel"),
        name="swa_attention",
    )(sinks, p, p, p, p, p, p, p, p, p)


def _attn_layer(h2d, batch, seq, norm_w, w_in_bf, sinks, w_out_bf):
    d = h2d.shape[1]
    hn = _rms_norm(h2d, norm_w, jnp.bfloat16)
    p = _matmul(hn, w_in_bf, tn=1024, name="attn_in_proj")
    o = _attention(p, batch, seq, d, sinks)
    return _matmul(o, w_out_bf, residual=h2d, name="attn_out_proj")


def kernel(x, norm_w, final_norm_w, rwkv_w_in, rwkv_mu, rwkv_w0, rwkv_w2, rwkv_a0, rwkv_a2,
           rwkv_k_k, rwkv_k_a, rwkv_r_k, rwkv_gn_w, rwkv_gn_b, rwkv_w_out,
           attn_w_in, attn_sinks, attn_w_out):
    batch, seq, d = x.shape
    h = x.reshape(batch * seq, d)
    depth = norm_w.shape[0]
    bf_weights = {}
    for i in range(depth):
        j = i // 2
        if i % 2 == 0:
            ahead = (attn_w_in[j], attn_w_out[j]) if i + 1 < depth else ()
            h, casted = _rwkv_layer(h, batch, seq, norm_w[i], rwkv_w_in[j], rwkv_mu[j],
                                    rwkv_w0[j], rwkv_w2[j], rwkv_a0[j], rwkv_a2[j], rwkv_k_k[j],
                                    rwkv_k_a[j], rwkv_r_k[j], rwkv_gn_w[j], rwkv_gn_b[j],
                                    rwkv_w_out[j], ahead)
            if ahead:
                bf_weights[i + 1] = casted
        else:
            w_in_bf, w_out_bf = bf_weights.get(i) or (attn_w_in[j].astype(jnp.bfloat16),
                                                      attn_w_out[j].astype(jnp.bfloat16))
            h = _attn_layer(h, batch, seq, norm_w[i], w_in_bf, attn_sinks[j], w_out_bf)
    return _rms_norm(h, final_norm_w, jnp.float32).reshape(batch, seq, d)
```

```python
import functools

import jax
import jax.numpy as jnp
import numpy as np
from jax import lax
from jax.experimental import pallas as pl
from jax.experimental.pallas import tpu as pltpu

HEAD_DIM = 64
LORA = 128
KV_HEADS = 8
GROUP = 8
BLOCK = 128
NORM_EPS = 1e-5
GN_EPS = HEAD_DIM * 1e-5
L2_EPS = 1e-12
MASK_VALUE = -1e30

LANE = 128
SUBLANE = 8
VMEM_LIMIT_BYTES = 56 * 1024 * 1024
SCAN_TB = 16
PITCH = HEAD_DIM + SUBLANE
ROW_PITCH = HEAD_DIM + 4


def _sigmoid(x):
    return 0.5 * jnp.tanh(0.5 * x) + 0.5


def _params(*sem):
    return pltpu.CompilerParams(dimension_semantics=sem, vmem_limit_bytes=VMEM_LIMIT_BYTES)


def _rms_kernel(x_ref, g_ref, o_ref):
    x = x_ref[...]
    ms = jnp.mean(x * x, axis=-1, keepdims=True)
    o_ref[...] = (x * lax.rsqrt(ms + NORM_EPS) * g_ref[...]).astype(o_ref.dtype)


def _rms_norm(x2d, g, out_dtype, tm=256):
    m, d = x2d.shape
    return pl.pallas_call(
        _rms_kernel,
        grid=(m // tm,),
        in_specs=[pl.BlockSpec((tm, d), lambda i: (i, 0)),
                  pl.BlockSpec((1, d), lambda i: (0, 0))],
        out_specs=pl.BlockSpec((tm, d), lambda i: (i, 0)),
        out_shape=jax.ShapeDtypeStruct((m, d), out_dtype),
        compiler_params=_params("parallel"),
        name="rmsnorm",
    )(x2d, g.reshape(1, d))


def _mm_kernel(a_ref, b_ref, o_ref):
    o_ref[...] = jnp.dot(a_ref[...], b_ref[...], preferred_element_type=jnp.float32)


def _mm_res_kernel(a_ref, b_ref, r_ref, o_ref):
    o_ref[...] = r_ref[...] + jnp.dot(a_ref[...], b_ref[...],
                                      preferred_element_type=jnp.float32)


def _matmul(a, b, residual=None, tm=512, tn=1024, name="matmul"):
    m, k = a.shape
    _, n = b.shape
    tm = min(tm, m)
    grid = (n // tn, m // tm)
    in_specs = [pl.BlockSpec((tm, k), lambda j, i: (i, 0)),
                pl.BlockSpec((k, tn), lambda j, i: (0, j))]
    args = [a, b]
    kern = _mm_kernel
    if residual is not None:
        in_specs.append(pl.BlockSpec((tm, tn), lambda j, i: (i, j)))
        args.append(residual)
        kern = _mm_res_kernel
    return pl.pallas_call(
        kern,
        grid=grid,
        in_specs=in_specs,
        out_specs=pl.BlockSpec((tm, tn), lambda j, i: (i, j)),
        out_shape=jax.ShapeDtypeStruct((m, n), jnp.float32),
        compiler_params=_params("parallel", "parallel"),
        name=name,
    )(*args)


def _store_lanes(o_ref, lead, res, tm):
    low = lax.broadcasted_iota(jnp.int32, (tm, LANE), 1) < HEAD_DIM
    for bp in range(2):
        ra = res[(2 * bp) * tm:(2 * bp + 1) * tm]
        rb = res[(2 * bp + 1) * tm:(2 * bp + 2) * tm]
        for c in range(res.shape[1] // LANE):
            a = ra[:, c * LANE:(c + 1) * LANE]
            b = rb[:, c * LANE:(c + 1) * LANE]
            even = jnp.where(low, a, pltpu.roll(b, HEAD_DIM, axis=1))
            odd = jnp.where(low, pltpu.roll(a, HEAD_DIM, axis=1), b)
            lanes = pl.ds(bp * LANE, LANE)
            shape = (tm // SUBLANE, SUBLANE, LANE)
            o_ref[lead + (slice(None), 2 * c, slice(None), lanes)] = even.reshape(shape)
            o_ref[lead + (slice(None), 2 * c + 1, slice(None), lanes)] = odd.reshape(shape)


def _inproj_lanes_kernel(x_ref, g_ref, wx_ref, mu_ref, ws_ref, wg_ref,
                         xo_ref, ps_ref, pg_ref, hn_ref, carry_ref, *, shifted_blocks):
    nb, tm, d = x_ref.shape
    tn = ws_ref.shape[0]
    i, j = pl.program_id(0), pl.program_id(1)

    @pl.when(j == 0)
    def _():
        for b in range(nb):
            x = x_ref[b]
            ms = jnp.mean(x * x, axis=-1, keepdims=True)
            hn_ref[b * tm:(b + 1) * tm, :] = (
                x * lax.rsqrt(ms + NORM_EPS) * g_ref[...]).astype(hn_ref.dtype)
        xo_ref[...] = jnp.dot(hn_ref[...], wx_ref[...],
                              preferred_element_type=jnp.float32).reshape(xo_ref.shape)

    @pl.when(j < shifted_blocks)
    def _():
        res = lax.dot_general(hn_ref[...], ws_ref[...], (((1,), (1,)), ((), ())),
                              preferred_element_type=jnp.float32)
        prev = jnp.where(i == 0, 0.0, carry_ref[j])
        row0 = lax.broadcasted_iota(jnp.int32, (tm, tn), 0) == 0
        parts, lasts = [], []
        for b in range(nb):
            rb = res[b * tm:(b + 1) * tm]
            before = jnp.where(row0, prev[b:b + 1], pltpu.roll(rb, 1, axis=0))
            parts.append(rb + (before - rb) * mu_ref[...])
            lasts.append(rb[tm - 1:tm])
        carry_ref[j] = jnp.concatenate(lasts, axis=0)
        _store_lanes(ps_ref, (0,), jnp.concatenate(parts, axis=0), tm)

    @pl.when(j >= shifted_blocks)
    def _():
        res = lax.dot_general(hn_ref[...], wg_ref[...], (((1,), (1,)), ((), ())),
                              preferred_element_type=jnp.float32)
        _store_lanes(pg_ref, (), res, tm)


def _inproj_lanes(x3, norm_w, w_x, mu_s, w_shift, w_gate, tm=256, tn=512):
    batch, seq, d = x3.shape
    assert batch == 4 and d == 64 * HEAD_DIM
    nv = tn // HEAD_DIM
    per_seg = d // tn
    ns = w_shift.shape[0] // tn
    ng = w_gate.shape[0] // tn
    tile = (HEAD_DIM, SUBLANE, 2 * LANE)
    blk = (tm // SUBLANE, nv, SUBLANE, 2 * LANE)
    js = lambda j: jnp.minimum(j, ns - 1)
    jg = lambda j: jnp.maximum(j - ns, 0)
    return pl.pallas_call(
        functools.partial(_inproj_lanes_kernel, shifted_blocks=ns),
        grid=(seq // tm, ns + ng),
        in_specs=[pl.BlockSpec((batch, tm, d), lambda i, j: (0, i, 0),
                               pipeline_mode=pl.Buffered(1)),
                  pl.BlockSpec((1, d), lambda i, j: (0, 0)),
                  pl.BlockSpec((d, w_x.shape[1]), lambda i, j: (0, 0)),
                  pl.BlockSpec((1, tn), lambda i, j: (0, js(j))),
                  pl.BlockSpec((tn, d), lambda i, j: (js(j), 0)),
                  pl.BlockSpec((tn, d), lambda i, j: (jg(j), 0))],
        out_specs=[pl.BlockSpec((batch, tm, w_x.shape[1]), lambda i, j: (0, i, 0)),
                   pl.BlockSpec((1,) + blk,
                                lambda i, j: (js(j) // per_seg, i, js(j) % per_seg, 0, 0)),
                   pl.BlockSpec(blk, lambda i, j: (i, jg(j), 0, 0))],
        out_shape=[jax.ShapeDtypeStruct((batch, seq, w_x.shape[1]), jnp.float32),
                   jax.ShapeDtypeStruct((ns // per_seg, seq // SUBLANE) + tile, jnp.float32),
                   jax.ShapeDtypeStruct((seq // SUBLANE,) + tile, jnp.float32)],
        scratch_shapes=[pltpu.VMEM((batch * tm, d), jnp.bfloat16),
                        pltpu.VMEM((ns, batch, tn), jnp.float32)],
        compiler_params=_params("arbitrary", "arbitrary"),
        name="rwkv_in_proj",
    )(x3, norm_w.reshape(1, d), w_x, mu_s.reshape(1, -1), w_shift, w_gate)


def _lora_kernel(x_ref, xp_ref, mu_ref, w2_ref, a2_ref, w0_ref, a0_ref, dec_ref, a_ref):
    nb, tm, _ = x_ref.shape
    first = pl.program_id(0) == 0
    row0 = lax.broadcasted_iota(jnp.int32, (tm, 2 * LORA), 0) == 0
    xw, xa = [], []
    for b in range(nb):
        x = x_ref[b]
        last = jnp.where(first, 0.0, xp_ref[b][SUBLANE - 1:SUBLANE])
        before = jnp.where(row0, last, pltpu.roll(x, 1, axis=0))
        x = x + (before - x) * mu_ref[...]
        xw.append(jnp.tanh(x[:, :LORA]).astype(jnp.bfloat16))
        xa.append(x[:, LORA:].astype(jnp.bfloat16))
    wl = w0_ref[...] + jnp.dot(jnp.concatenate(xw, axis=0), w2_ref[...],
                               preferred_element_type=jnp.float32)
    al = a0_ref[...] + jnp.dot(jnp.concatenate(xa, axis=0), a2_ref[...],
                               preferred_element_type=jnp.float32)
    sig_w = _sigmoid(wl)
    _store_lanes(dec_ref, (), jnp.exp(sig_w * (-float(np.exp(-0.5)))), tm)
    _store_lanes(a_ref, (), _sigmoid(al), tm)


def _lora_lanes(x3, mu_x, w2p, a2p, w0p, a0p, tm=256, tn=512):
    batch, seq, _ = x3.shape
    d = w2p.shape[1]
    nv = tn // HEAD_DIM
    out = jax.ShapeDtypeStruct((seq // SUBLANE, HEAD_DIM, SUBLANE, 2 * LANE), jnp.float32)
    ospec = pl.BlockSpec((tm // SUBLANE, nv, SUBLANE, 2 * LANE), lambda i, j: (i, j, 0, 0))
    tq = tm // SUBLANE
    return pl.pallas_call(
        _lora_kernel,
        grid=(seq // tm, d // tn),
        in_specs=[pl.BlockSpec((batch, tm, 2 * LORA), lambda i, j: (0, i, 0)),
                  pl.BlockSpec((batch, SUBLANE, 2 * LORA),
                               lambda i, j: (0, jnp.maximum(i * tq - 1, 0), 0)),
                  pl.BlockSpec((1, 2 * LORA), lambda i, j: (0, 0)),
                  pl.BlockSpec((LORA, tn), lambda i, j: (0, j)),
                  pl.BlockSpec((LORA, tn), lambda i, j: (0, j)),
                  pl.BlockSpec((1, tn), lambda i, j: (0, j)),
                  pl.BlockSpec((1, tn), lambda i, j: (0, j))],
        out_specs=[ospec, ospec],
        out_shape=[out, out],
        compiler_params=_params("parallel", "arbitrary"),
        name="rwkv_lora",
    )(x3, x3, mu_x.reshape(1, -1), w2p, a2p, w0p.reshape(1, -1), a0p.reshape(1, -1))


def _scan_kernel(*refs, n_casts):
    (r_ref, k_ref, v_ref, g_ref, w_ref, a_ref,
     kkp_ref, kap_ref, rkp_ref, gnw_ref, gnb_ref) = refs[:11]
    cast_in = refs[11:11 + n_casts]
    o_ref = refs[11 + n_casts]
    cast_out = refs[12 + n_casts:12 + 2 * n_casts]
    s_ref, gcar_ref, bv_s, r_s, gam_s, k_s, kk_s, b_s, v_s, y_s = refs[12 + 2 * n_casts:]
    for src, dst in zip(cast_in, cast_out):
        dst[...] = src[...].astype(dst.dtype)
    tq = w_ref.shape[0]
    tb = tq * SUBLANE
    n = HEAD_DIM
    ch = SUBLANE

    @pl.when(pl.program_id(0) == 0)
    def _():
        s_ref[...] = jnp.zeros_like(s_ref)

    sub = lax.broadcasted_iota(jnp.int32, (ch, SUBLANE, 2 * LANE), 1)

    def to_steps(dst, pitch, q, c0, val):
        for c in range(ch):
            for slab in range(2):
                dst[slab, pl.ds(q * SUBLANE * pitch + c0 + c, SUBLANE, stride=pitch), :] = (
                    val[c, :, slab * LANE:(slab + 1) * LANE])

    def k_terms(q, cs):
        k = k_ref[0, q, cs]
        a = a_ref[q, cs]
        return k * kkp_ref[cs], k * (1.0 + (a - 1.0) * kap_ref[cs]), a

    for q in range(tq):
        def sums(j, carry, q=q):
            ssq, bon = carry
            cs = pl.ds(pl.multiple_of(j * ch, ch), ch)
            kkr, k2, _ = k_terms(q, cs)
            return ssq + kkr * kkr, bon + r_ref[0, q, cs] * k2 * rkp_ref[cs]

        zeros = jnp.zeros((ch, SUBLANE, 2 * LANE), jnp.float32)
        ssq, bon = lax.fori_loop(0, n // ch, sums, (zeros, zeros))
        nrm = jnp.sqrt(jnp.sum(ssq, axis=0, keepdims=True))
        inv_nrm = 1.0 / jnp.maximum(nrm, L2_EPS)
        bonus = jnp.sum(bon, axis=0, keepdims=True)

        def emit(j, carry, q=q, inv_nrm=inv_nrm, bonus=bonus):
            c0 = pl.multiple_of(j * ch, ch)
            cs = pl.ds(c0, ch)
            kkr, k2, a = k_terms(q, cs)
            v = v_ref[0, q, cs]
            kk = kkr * inv_nrm
            gam = w_ref[q, cs]
            for sh in (1, 2, 4):
                gam = gam * jnp.where(sub >= sh, pltpu.roll(gam, sh, axis=1), 1.0)
            before_tile = 1.0 if q == 0 else gcar_ref[cs]
            gam = gam * before_tile
            gam_prev = jnp.where(sub == 0, before_tile, pltpu.roll(gam, 1, axis=1))
            inv_gam = 1.0 / gam
            if q + 1 < tq:
                gcar_ref[cs] = jnp.broadcast_to(gam[:, SUBLANE - 1:SUBLANE, :], gam.shape)
            bv_s[q, cs] = bonus * v
            to_steps(r_s, ROW_PITCH, q, c0, r_ref[0, q, cs] * gam)
            to_steps(gam_s, ROW_PITCH, q, c0, gam)
            to_steps(k_s, ROW_PITCH, q, c0, k2 * inv_gam)
            to_steps(kk_s, ROW_PITCH, q, c0, kk * gam_prev)
            to_steps(b_s, ROW_PITCH, q, c0, kk * a * inv_gam)
            to_steps(v_s, PITCH, q, c0, v)
            return carry

        lax.fori_loop(0, n // ch, emit, 0)

    zero = jnp.zeros((n, LANE), jnp.float32)

    def row(ref, slab, i):
        return ref[slab, pl.ds(i, 1), :]

    for slab in range(2):
        def sa_first(c, acc, slab=slab):
            return acc + s_ref[slab, c] * row(kk_s, slab, c)

        def step(t, sa, slab=slab):
            base = t * ROW_PITCH
            ahead = jnp.minimum(t + 1, tb - 1) * ROW_PITCH
            tile = pl.ds(pl.multiple_of(t * PITCH, SUBLANE), n)
            vv = v_s[slab, tile, :]

            def channel(c, carry):
                y_acc, sa_next = carry
                s_new = (s_ref[slab, c] - sa * row(b_s, slab, base + c)
                         + vv * row(k_s, slab, base + c))
                s_ref[slab, c] = s_new
                return (y_acc + s_new * row(r_s, slab, base + c),
                        sa_next + s_new * row(kk_s, slab, ahead + c))

            y, sa_next = lax.fori_loop(0, n, channel, (zero, zero), unroll=True)
            y_s[slab, tile, :] = y
            return sa_next

        lax.fori_loop(0, tb, step, lax.fori_loop(0, n, sa_first, zero, unroll=8))

        def rescale(c, carry, slab=slab):
            s_ref[slab, c] = s_ref[slab, c] * row(gam_s, slab, (tb - 1) * ROW_PITCH + c)
            return carry

        lax.fori_loop(0, n, rescale, 0, unroll=8)

    for q in range(tq):
        for slab in range(2):
            lanes = pl.ds(slab * LANE, LANE)
            y = jnp.stack([y_s[slab, pl.ds(q * SUBLANE * PITCH + c, SUBLANE, stride=PITCH), :]
                           for c in range(n)], axis=0)
            mean = jnp.mean(y, axis=0, keepdims=True)
            yc = y - mean
            var = jnp.mean(yc * yc, axis=0, keepdims=True)
            y = yc * lax.rsqrt(var + GN_EPS) * gnw_ref[:, :, lanes] + gnb_ref[:, :, lanes]
            y = y + bv_s[q, :, :, lanes]
            g = g_ref[q, :, :, lanes]
            o_ref[q, :, :, lanes] = y * (g * _sigmoid(g))


def _scan(p3, g4, dec4, a4, head_params, side_casts=(), tb=SCAN_TB):
    _, tqs, n, _, lanes = p3.shape
    tq = min(tb // SUBLANE, tqs)
    steps = tqs // tq
    cast_specs = [pl.BlockSpec((w.shape[0] // steps, w.shape[1]), lambda i: (i, 0))
                  for w in side_casts]
    tile = (n, SUBLANE, lanes)
    seg = lambda s: pl.BlockSpec((1, tq) + tile, lambda i, s=s: (s, i, 0, 0, 0))
    blk = pl.BlockSpec((tq,) + tile, lambda i: (i, 0, 0, 0))
    par = pl.BlockSpec(tile, lambda i: (0, 0, 0))
    rows = (2, tq * SUBLANE * ROW_PITCH, LANE)
    tiles = (2, tq * SUBLANE * PITCH, LANE)
    scratch = [pltpu.VMEM((2, n, n, LANE), jnp.float32),
               pltpu.VMEM(tile, jnp.float32),
               pltpu.VMEM((tq,) + tile, jnp.float32)]
    scratch += [pltpu.VMEM(rows, jnp.float32) for _ in range(5)]
    scratch += [pltpu.VMEM(tiles, jnp.float32) for _ in range(2)]
    outs = pl.pallas_call(
        functools.partial(_scan_kernel, n_casts=len(side_casts)),
        grid=(steps,),
        in_specs=([seg(0), seg(1), seg(2), blk, blk, blk] + [par] * len(head_params)
                  + cast_specs),
        out_specs=[blk] + cast_specs,
        out_shape=[jax.ShapeDtypeStruct((tqs,) + tile, jnp.float32)]
        + [jax.ShapeDtypeStruct(w.shape, jnp.bfloat16) for w in side_casts],
        scratch_shapes=scratch,
        compiler_params=_params("arbitrary"),
        name="rwkv_scan",
    )(p3, p3, p3, g4, dec4, a4, *head_params, *side_casts)
    return outs[0], outs[1:]


def _outproj_lanes_kernel(y_ref, w_ref, r_ref, o_ref, lhs_ref):
    tq = y_ref.shape[0]
    tm = tq * SUBLANE
    nb, _, d = lhs_ref.shape

    @pl.when(pl.program_id(1) == 0)
    def _():
        low = lax.broadcasted_iota(jnp.int32, (tm, LANE), 1) < HEAD_DIM
        for c in range(HEAD_DIM // 2):
            cols = pl.ds(c * LANE, LANE)
            for bp in range(2):
                lanes = pl.ds(bp * LANE, LANE)
                e = y_ref[:, 2 * c, :, lanes].reshape(tm, LANE)
                o = y_ref[:, 2 * c + 1, :, lanes].reshape(tm, LANE)
                lhs_ref[2 * bp, :, cols] = jnp.where(
                    low, e, pltpu.roll(o, HEAD_DIM, axis=1)).astype(lhs_ref.dtype)
                lhs_ref[2 * bp + 1, :, cols] = jnp.where(
                    low, pltpu.roll(e, HEAD_DIM, axis=1), o).astype(lhs_ref.dtype)

    res = jnp.dot(lhs_ref[...].reshape(nb * tm, d), w_ref[...],
                  preferred_element_type=jnp.float32)
    o_ref[...] = r_ref[...] + res.reshape(o_ref.shape)


def _outproj_lanes(y4, w_perm, residual3, tm=256, tn=512):
    batch, seq, d = residual3.shape
    tq = tm // SUBLANE
    ospec = pl.BlockSpec((batch, tm, tn), lambda i, j: (0, i, j))
    return pl.pallas_call(
        _outproj_lanes_kernel,
        grid=(seq // tm, d // tn),
        in_specs=[pl.BlockSpec((tq, HEAD_DIM, SUBLANE, 2 * LANE), lambda i, j: (i, 0, 0, 0),
                               pipeline_mode=pl.Buffered(1)),
                  pl.BlockSpec((d, tn), lambda i, j: (0, j)),
                  ospec],
        out_specs=ospec,
        out_shape=jax.ShapeDtypeStruct((batch, seq, d), jnp.float32),
        scratch_shapes=[pltpu.VMEM((batch, tm, d), jnp.bfloat16)],
        compiler_params=_params("parallel", "arbitrary"),
        name="rwkv_out_proj",
    )(y4, w_perm, residual3)


def _cols_nh(w, segs=1):
    lead = w.shape[:-1]
    return w.reshape(lead + (segs, -1, HEAD_DIM)).swapaxes(-1, -2).reshape(w.shape)


def _transposed_nh(w, segs):
    k, cols = w.shape
    return w.reshape(k, segs, -1, HEAD_DIM).transpose(1, 3, 2, 0).reshape(cols, k)


def _param_lanes(p, batch):
    pt = jnp.tile(p.reshape(-1, HEAD_DIM).T, (1, batch))
    return jnp.broadcast_to(pt[:, None, :], (HEAD_DIM, SUBLANE, pt.shape[1]))


def _rwkv_layer(h2d, batch, seq, norm_w, w_in, mu, w0, w2, a0, a2, k_k, k_a, r_k, gn_w,
                gn_b, w_out, side_casts=()):
    d = h2d.shape[1]
    bf = jnp.bfloat16
    x_cols = slice(3 * d, 3 * d + 2 * LORA)
    w_bf = w_in.astype(bf)
    x, p3, g4 = _inproj_lanes(h2d.reshape(batch, seq, d), norm_w, w_bf[:, x_cols],
                              _cols_nh(mu[:3 * d], 3),
                              _transposed_nh(w_bf[:, :3 * d], 3),
                              _transposed_nh(w_bf[:, 3 * d + 2 * LORA:], 1))
    dec4, a4 = _lora_lanes(x, mu[x_cols],
                           _cols_nh(w2).astype(bf), _cols_nh(a2).astype(bf),
                           _cols_nh(w0), _cols_nh(a0))
    head_params = [_param_lanes(x, batch) for x in (k_k, k_a, r_k.reshape(-1), gn_w, gn_b)]
    w_out_perm = w_out.reshape(-1, HEAD_DIM, d).swapaxes(0, 1).reshape(d, d)
    y4, casted = _scan(p3, g4, dec4, a4, head_params, (w_out_perm,) + tuple(side_casts))
    out = _outproj_lanes(y4, casted[0], h2d.reshape(batch, seq, d))
    return out.reshape(batch * seq, d), casted[1:]


def _attn_kernel(sinks_ref, q_ref, kc_ref, kp_ref, vc_ref, vp_ref, g0_ref, g1_ref, g2_ref,
                 g3_ref, o_ref):
    bf = jnp.bfloat16
    n = HEAD_DIM
    pairs = GROUP // 2
    nq = pairs * BLOCK
    not_first = pl.program_id(1) > 0
    kj = lax.broadcasted_iota(jnp.int32, (2 * BLOCK, nq), 0)
    qi = lax.broadcasted_iota(jnp.int32, (2 * BLOCK, nq), 1) & (BLOCK - 1)
    delta = BLOCK + qi - kj
    mask = (delta >= 0) & (delta < BLOCK) & ((kj >= BLOCK) | not_first)
    sink_slot = kj == 0
    low_kv = lax.broadcasted_iota(jnp.int32, (2 * BLOCK, LANE), 1) < n
    key0_kv = lax.broadcasted_iota(jnp.int32, (2 * BLOCK, LANE), 0) == 0
    gate_refs = (g0_ref, g1_ref, g2_ref, g3_ref)
    gate_w = g0_ref.shape[1]
    for hk in range(KV_HEADS):
        cols = slice((hk // 2) * LANE, (hk // 2 + 1) * LANE)
        kt = jnp.concatenate([kp_ref[:, cols], kc_ref[:, cols]], axis=0)
        vt = jnp.concatenate([vp_ref[:, cols], vc_ref[:, cols]], axis=0)
        kr = pltpu.roll(kt, n, axis=1)
        vr = pltpu.roll(vt, n, axis=1)
        if hk % 2 == 0:
            k_low, k_high = jnp.where(low_kv, kt, 0.0), jnp.where(low_kv, 0.0, kr)
            v_both = jnp.where(low_kv, vt, vr)
        else:
            k_low, k_high = jnp.where(low_kv, kr, 0.0), jnp.where(low_kv, 0.0, kt)
            v_both = jnp.where(low_kv, vr, vt)
        v_t = jnp.where(key0_kv, 0.0, v_both).T.astype(bf)
        qs = jnp.concatenate([q_ref[:, (hk * pairs + p) * LANE:(hk * pairs + p + 1) * LANE]
                              for p in range(pairs)], axis=0)
        qs = (qs * (n ** -0.5)).astype(bf)
        outs = []
        for parity, k_sel in ((0, k_low), (1, k_high)):
            s = lax.dot_general(k_sel.astype(bf), qs, (((1,), (1,)), ((), ())),
                                preferred_element_type=jnp.float32)
            sink = jnp.concatenate(
                [jnp.full((1, BLOCK), sinks_ref[hk * GROUP + 2 * p + parity], jnp.float32)
                 for p in range(pairs)], axis=1)
            s = jnp.where(mask, s, jnp.where(sink_slot, sink, MASK_VALUE))
            m = jnp.max(s, axis=0, keepdims=True)
            e = jnp.exp(s - m)
            inv = 1.0 / jnp.sum(e, axis=0, keepdims=True)
            outs.append(jnp.dot(v_t, (e * inv).astype(bf), preferred_element_type=jnp.float32))
        for p in range(pairs):
            col = (hk * pairs + p) * LANE
            qcols = slice(p * BLOCK, (p + 1) * BLOCK)
            o = jnp.concatenate([outs[0][:n, qcols], outs[1][n:, qcols]], axis=0).T
            g_ref = gate_refs[col // gate_w]
            gate = g_ref[:, col % gate_w:col % gate_w + LANE]
            o_ref[:, col:col + LANE] = (o * (gate * _sigmoid(gate))).astype(o_ref.dtype)


def _attention(p, batch, seq, d, sinks):
    nb = seq // BLOCK
    kvw = KV_HEADS * HEAD_DIM
    gate_w = 2 * kvw
    row = lambda b, n: b * nb + n
    prow = lambda b, n: b * nb + jnp.maximum(n - 1, 0)
    kblk, vblk = d // kvw, d // kvw + 1
    gblk = (d + 2 * kvw) // gate_w
    in_specs = [pl.BlockSpec(memory_space=pltpu.SMEM),
                pl.BlockSpec((BLOCK, d), lambda b, n: (row(b, n), 0)),
                pl.BlockSpec((BLOCK, kvw), lambda b, n: (row(b, n), kblk)),
                pl.BlockSpec((BLOCK, kvw), lambda b, n: (prow(b, n), kblk)),
                pl.BlockSpec((BLOCK, kvw), lambda b, n: (row(b, n), vblk)),
                pl.BlockSpec((BLOCK, kvw), lambda b, n: (prow(b, n), vblk))]
    in_specs += [pl.BlockSpec((BLOCK, gate_w), lambda b, n, j=j: (row(b, n), gblk + j))
                 for j in range(4)]
    return pl.pallas_call(
        _attn_kernel,
        grid=(batch, nb),
        in_specs=in_specs,
        out_specs=pl.BlockSpec((BLOCK, d), lambda b, n: (row(b, n), 0)),
        out_shape=jax.ShapeDtypeStruct((batch * seq, d), jnp.bfloat16),
        compiler_params=_params("parallel", "parallel"),
        name="swa_attention",
    )(sinks, p, p, p, p, p, p, p, p, p)


def _attn_layer(h2d, batch, seq, norm_w, w_in_bf, sinks, w_out_bf):
    d = h2d.shape[1]
    hn = _rms_norm(h2d, norm_w, jnp.bfloat16)
    p = _matmul(hn, w_in_bf, tn=1024, name="attn_in_proj")
    o = _attention(p, batch, seq, d, sinks)
    return _matmul(o, w_out_bf, residual=h2d, name="attn_out_proj")


def kernel(x, norm_w, final_norm_w, rwkv_w_in, rwkv_mu, rwkv_w0, rwkv_w2, rwkv_a0, rwkv_a2,
           rwkv_k_k, rwkv_k_a, rwkv_r_k, rwkv_gn_w, rwkv_gn_b, rwkv_w_out,
           attn_w_in, attn_sinks, attn_w_out):
    batch, seq, d = x.shape
    h = x.reshape(batch * seq, d)
    depth = norm_w.shape[0]
    bf_weights = {}
    for i in range(depth):
        j = i // 2
        if i % 2 == 0:
            ahead = (attn_w_in[j], attn_w_out[j]) if i + 1 < depth else ()
            h, casted = _rwkv_layer(h, batch, seq, norm_w[i], rwkv_w_in[j], rwkv_mu[j],
                                    rwkv_w0[j], rwkv_w2[j], rwkv_a0[j], rwkv_a2[j], rwkv_k_k[j],
                                    rwkv_k_a[j], rwkv_r_k[j], rwkv_gn_w[j], rwkv_gn_b[j],
                                    rwkv_w_out[j], ahead)
            if ahead:
                bf_weights[i + 1] = casted
        else:
            w_in_bf, w_out_bf = bf_weights.get(i) or (attn_w_in[j].astype(jnp.bfloat16),
                                                      attn_w_out[j].astype(jnp.bfloat16))
            h = _attn_layer(h, batch, seq, norm_w[i], w_in_bf, attn_sinks[j], w_out_bf)
    return _rms_norm(h, final_norm_w, jnp.float32).reshape(batch, seq, d)
```

```python
import functools

import jax
import jax.numpy as jnp
import numpy as np
from jax import lax
from jax.experimental import pallas as pl
from jax.experimental.pallas import tpu as pltpu

HEAD_DIM = 64
LORA = 128
KV_HEADS = 8
GROUP = 8
BLOCK = 128
NORM_EPS = 1e-5
GN_EPS = HEAD_DIM * 1e-5
L2_EPS = 1e-12
MASK_VALUE = -1e30

LANE = 128
SUBLANE = 8
VMEM_LIMIT_BYTES = 56 * 1024 * 1024
SCAN_TB = 16
PITCH = HEAD_DIM + SUBLANE
ROW_PITCH = HEAD_DIM + 4


def _sigmoid(x):
    return 0.5 * jnp.tanh(0.5 * x) + 0.5


def _params(*sem):
    return pltpu.CompilerParams(dimension_semantics=sem, vmem_limit_bytes=VMEM_LIMIT_BYTES)


def _rms_kernel(x_ref, g_ref, o_ref):
    x = x_ref[...]
    ms = jnp.mean(x * x, axis=-1, keepdims=True)
    o_ref[...] = (x * lax.rsqrt(ms + NORM_EPS) * g_ref[...]).astype(o_ref.dtype)


def _rms_norm(x2d, g, out_dtype, tm=256):
    m, d = x2d.shape
    return pl.pallas_call(
        _rms_kernel,
        grid=(m // tm,),
        in_specs=[pl.BlockSpec((tm, d), lambda i: (i, 0)),
                  pl.BlockSpec((1, d), lambda i: (0, 0))],
        out_specs=pl.BlockSpec((tm, d), lambda i: (i, 0)),
        out_shape=jax.ShapeDtypeStruct((m, d), out_dtype),
        compiler_params=_params("parallel"),
        name="rmsnorm",
    )(x2d, g.reshape(1, d))


def _mm_kernel(a_ref, b_ref, o_ref):
    o_ref[...] = jnp.dot(a_ref[...], b_ref[...], preferred_element_type=jnp.float32)


def _mm_res_kernel(a_ref, b_ref, r_ref, o_ref):
    o_ref[...] = r_ref[...] + jnp.dot(a_ref[...], b_ref[...],
                                      preferred_element_type=jnp.float32)


def _matmul(a, b, residual=None, tm=512, tn=1024, name="matmul"):
    m, k = a.shape
    _, n = b.shape
    tm = min(tm, m)
    grid = (n // tn, m // tm)
    in_specs = [pl.BlockSpec((tm, k), lambda j, i: (i, 0)),
                pl.BlockSpec((k, tn), lambda j, i: (0, j))]
    args = [a, b]
    kern = _mm_kernel
    if residual is not None:
        in_specs.append(pl.BlockSpec((tm, tn), lambda j, i: (i, j)))
        args.append(residual)
        kern = _mm_res_kernel
    return pl.pallas_call(
        kern,
        grid=grid,
        in_specs=in_specs,
        out_specs=pl.BlockSpec((tm, tn), lambda j, i: (i, j)),
        out_shape=jax.ShapeDtypeStruct((m, n), jnp.float32),
        compiler_params=_params("parallel", "parallel"),
        name=name,
    )(*args)


def _store_lanes(o_ref, lead, res, tm):
    low = lax.broadcasted_iota(jnp.int32, (tm, LANE), 1) < HEAD_DIM
    for bp in range(2):
        ra = res[(2 * bp) * tm:(2 * bp + 1) * tm]
        rb = res[(2 * bp + 1) * tm:(2 * bp + 2) * tm]
        for c in range(res.shape[1] // LANE):
            a = ra[:, c * LANE:(c + 1) * LANE]
            b = rb[:, c * LANE:(c + 1) * LANE]
            even = jnp.where(low, a, pltpu.roll(b, HEAD_DIM, axis=1))
            odd = jnp.where(low, pltpu.roll(a, HEAD_DIM, axis=1), b)
            lanes = pl.ds(bp * LANE, LANE)
            shape = (tm // SUBLANE, SUBLANE, LANE)
            o_ref[lead + (slice(None), 2 * c, slice(None), lanes)] = even.reshape(shape)
            o_ref[lead + (slice(None), 2 * c + 1, slice(None), lanes)] = odd.reshape(shape)


def _inproj_lanes_kernel(x_ref, g_ref, wx_ref, mu_ref, ws_ref, wg_ref,
                         xo_ref, ps_ref, pg_ref, hn_ref, carry_ref, *, shifted_blocks):
    nb, tm, d = x_ref.shape
    tn = ws_ref.shape[0]
    i, j = pl.program_id(0), pl.program_id(1)

    @pl.when(j == 0)
    def _():
        for b in range(nb):
            x = x_ref[b]
            ms = jnp.mean(x * x, axis=-1, keepdims=True)
            hn_ref[b * tm:(b + 1) * tm, :] = (
                x * lax.rsqrt(ms + NORM_EPS) * g_ref[...]).astype(hn_ref.dtype)
        xo_ref[...] = jnp.dot(hn_ref[...], wx_ref[...],
                              preferred_element_type=jnp.float32).reshape(xo_ref.shape)

    @pl.when(j < shifted_blocks)
    def _():
        res = lax.dot_general(hn_ref[...], ws_ref[...], (((1,), (1,)), ((), ())),
                              preferred_element_type=jnp.float32)
        prev = jnp.where(i == 0, 0.0, carry_ref[j])
        row0 = lax.broadcasted_iota(jnp.int32, (tm, tn), 0) == 0
        parts, lasts = [], []
        for b in range(nb):
            rb = res[b * tm:(b + 1) * tm]
            before = jnp.where(row0, prev[b:b + 1], pltpu.roll(rb, 1, axis=0))
            parts.append(rb + (before - rb) * mu_ref[...])
            lasts.append(rb[tm - 1:tm])
        carry_ref[j] = jnp.concatenate(lasts, axis=0)
        _store_lanes(ps_ref, (0,), jnp.concatenate(parts, axis=0), tm)

    @pl.when(j >= shifted_blocks)
    def _():
        res = lax.dot_general(hn_ref[...], wg_ref[...], (((1,), (1,)), ((), ())),
                              preferred_element_type=jnp.float32)
        _store_lanes(pg_ref, (), res, tm)


def _inproj_lanes(x3, norm_w, w_x, mu_s, w_shift, w_gate, tm=256, tn=512):
    batch, seq, d = x3.shape
    assert batch == 4 and d == 64 * HEAD_DIM
    nv = tn // HEAD_DIM
    per_seg = d // tn
    ns = w_shift.shape[0] // tn
    ng = w_gate.shape[0] // tn
    tile = (HEAD_DIM, SUBLANE, 2 * LANE)
    blk = (tm // SUBLANE, nv, SUBLANE, 2 * LANE)
    js = lambda j: jnp.minimum(j, ns - 1)
    jg = lambda j: jnp.maximum(j - ns, 0)
    return pl.pallas_call(
        functools.partial(_inproj_lanes_kernel, shifted_blocks=ns),
        grid=(seq // tm, ns + ng),
        in_specs=[pl.BlockSpec((batch, tm, d), lambda i, j: (0, i, 0),
                               pipeline_mode=pl.Buffered(1)),
                  pl.BlockSpec((1, d), lambda i, j: (0, 0)),
                  pl.BlockSpec((d, w_x.shape[1]), lambda i, j: (0, 0)),
                  pl.BlockSpec((1, tn), lambda i, j: (0, js(j))),
                  pl.BlockSpec((tn, d), lambda i, j: (js(j), 0)),
                  pl.BlockSpec((tn, d), lambda i, j: (jg(j), 0))],
        out_specs=[pl.BlockSpec((batch, tm, w_x.shape[1]), lambda i, j: (0, i, 0)),
                   pl.BlockSpec((1,) + blk,
                                lambda i, j: (js(j) // per_seg, i, js(j) % per_seg, 0, 0)),
                   pl.BlockSpec(blk, lambda i, j: (i, jg(j), 0, 0))],
        out_shape=[jax.ShapeDtypeStruct((batch, seq, w_x.shape[1]), jnp.float32),
                   jax.ShapeDtypeStruct((ns // per_seg, seq // SUBLANE) + tile, jnp.float32),
                   jax.ShapeDtypeStruct((seq // SUBLANE,) + tile, jnp.float32)],
        scratch_shapes=[pltpu.VMEM((batch * tm, d), jnp.bfloat16),
                        pltpu.VMEM((ns, batch, tn), jnp.float32)],
        compiler_params=_params("arbitrary", "arbitrary"),
        name="rwkv_in_proj",
    )(x3, norm_w.reshape(1, d), w_x, mu_s.reshape(1, -1), w_shift, w_gate)


def _lora_kernel(x_ref, xp_ref, mu_ref, w2_ref, a2_ref, w0_ref, a0_ref, dec_ref, a_ref):
    nb, tm, _ = x_ref.shape
    first = pl.program_id(0) == 0
    row0 = lax.broadcasted_iota(jnp.int32, (tm, 2 * LORA), 0) == 0
    xw, xa = [], []
    for b in range(nb):
        x = x_ref[b]
        last = jnp.where(first, 0.0, xp_ref[b][SUBLANE - 1:SUBLANE])
        before = jnp.where(row0, last, pltpu.roll(x, 1, axis=0))
        x = x + (before - x) * mu_ref[...]
        xw.append(jnp.tanh(x[:, :LORA]).astype(jnp.bfloat16))
        xa.append(x[:, LORA:].astype(jnp.bfloat16))
    wl = w0_ref[...] + jnp.dot(jnp.concatenate(xw, axis=0), w2_ref[...],
                               preferred_element_type=jnp.float32)
    al = a0_ref[...] + jnp.dot(jnp.concatenate(xa, axis=0), a2_ref[...],
                               preferred_element_type=jnp.float32)
    sig_w = _sigmoid(wl)
    _store_lanes(dec_ref, (), jnp.exp(sig_w * (-float(np.exp(-0.5)))), tm)
    _store_lanes(a_ref, (), _sigmoid(al), tm)


def _lora_lanes(x3, mu_x, w2p, a2p, w0p, a0p, tm=256, tn=512):
    batch, seq, _ = x3.shape
    d = w2p.shape[1]
    nv = tn // HEAD_DIM
    out = jax.ShapeDtypeStruct((seq // SUBLANE, HEAD_DIM, SUBLANE, 2 * LANE), jnp.float32)
    ospec = pl.BlockSpec((tm // SUBLANE, nv, SUBLANE, 2 * LANE), lambda i, j: (i, j, 0, 0))
    tq = tm // SUBLANE
    return pl.pallas_call(
        _lora_kernel,
        grid=(seq // tm, d // tn),
        in_specs=[pl.BlockSpec((batch, tm, 2 * LORA), lambda i, j: (0, i, 0)),
                  pl.BlockSpec((batch, SUBLANE, 2 * LORA),
                               lambda i, j: (0, jnp.maximum(i * tq - 1, 0), 0)),
                  pl.BlockSpec((1, 2 * LORA), lambda i, j: (0, 0)),
                  pl.BlockSpec((LORA, tn), lambda i, j: (0, j)),
                  pl.BlockSpec((LORA, tn), lambda i, j: (0, j)),
                  pl.BlockSpec((1, tn), lambda i, j: (0, j)),
                  pl.BlockSpec((1, tn), lambda i, j: (0, j))],
        out_specs=[ospec, ospec],
        out_shape=[out, out],
        compiler_params=_params("parallel", "arbitrary"),
        name="rwkv_lora",
    )(x3, x3, mu_x.reshape(1, -1), w2p, a2p, w0p.reshape(1, -1), a0p.reshape(1, -1))


def _scan_kernel(*refs, n_casts):
    (r_ref, k_ref, v_ref, g_ref, w_ref, a_ref,
     kkp_ref, kap_ref, rkp_ref, gnw_ref, gnb_ref) = refs[:11]
    cast_in = refs[11:11 + n_casts]
    o_ref = refs[11 + n_casts]
    cast_out = refs[12 + n_casts:12 + 2 * n_casts]
    s_ref, gcar_ref, bv_s, r_s, gam_s, k_s, kk_s, b_s, v_s, y_s = refs[12 + 2 * n_casts:]
    for src, dst in zip(cast_in, cast_out):
        dst[...] = src[...].astype(dst.dtype)
    tq = w_ref.shape[0]
    tb = tq * SUBLANE
    n = HEAD_DIM
    ch = SUBLANE

    @pl.when(pl.program_id(0) == 0)
    def _():
        s_ref[...] = jnp.zeros_like(s_ref)

    sub = lax.broadcasted_iota(jnp.int32, (ch, SUBLANE, 2 * LANE), 1)

    def to_steps(dst, pitch, q, c0, val):
        for c in range(ch):
            for slab in range(2):
                dst[slab, pl.ds(q * SUBLANE * pitch + c0 + c, SUBLANE, stride=pitch), :] = (
                    val[c, :, slab * LANE:(slab + 1) * LANE])

    def k_terms(q, cs):
        k = k_ref[0, q, cs]
        a = a_ref[q, cs]
        return k * kkp_ref[cs], k * (1.0 + (a - 1.0) * kap_ref[cs]), a

    for q in range(tq):
        def sums(j, carry, q=q):
            ssq, bon = carry
            cs = pl.ds(pl.multiple_of(j * ch, ch), ch)
            kkr, k2, _ = k_terms(q, cs)
            return ssq + kkr * kkr, bon + r_ref[0, q, cs] * k2 * rkp_ref[cs]

        zeros = jnp.zeros((ch, SUBLANE, 2 * LANE), jnp.float32)
        ssq, bon = lax.fori_loop(0, n // ch, sums, (zeros, zeros))
        nrm = jnp.sqrt(jnp.sum(ssq, axis=0, keepdims=True))
        inv_nrm = 1.0 / jnp.maximum(nrm, L2_EPS)
        bonus = jnp.sum(bon, axis=0, keepdims=True)

        def emit(j, carry, q=q, inv_nrm=inv_nrm, bonus=bonus):
            c0 = pl.multiple_of(j * ch, ch)
            cs = pl.ds(c0, ch)
            kkr, k2, a = k_terms(q, cs)
            v = v_ref[0, q, cs]
            kk = kkr * inv_nrm
            gam = w_ref[q, cs]
            for sh in (1, 2, 4):
                gam = gam * jnp.where(sub >= sh, pltpu.roll(gam, sh, axis=1), 1.0)
            before_tile = 1.0 if q == 0 else gcar_ref[cs]
            gam = gam * before_tile
            gam_prev = jnp.where(sub == 0, before_tile, pltpu.roll(gam, 1, axis=1))
            inv_gam = 1.0 / gam
            if q + 1 < tq:
                gcar_ref[cs] = jnp.broadcast_to(gam[:, SUBLANE - 1:SUBLANE, :], gam.shape)
            bv_s[q, cs] = bonus * v
            to_steps(r_s, ROW_PITCH, q, c0, r_ref[0, q, cs] * gam)
            to_steps(gam_s, ROW_PITCH, q, c0, gam)
            to_steps(k_s, ROW_PITCH, q, c0, k2 * inv_gam)
            to_steps(kk_s, ROW_PITCH, q, c0, kk * gam_prev)
            to_steps(b_s, ROW_PITCH, q, c0, kk * a * inv_gam)
            to_steps(v_s, PITCH, q, c0, v)
            return carry

        lax.fori_loop(0, n // ch, emit, 0)

    zero = jnp.zeros((n, LANE), jnp.float32)

    def row(ref, slab, i):
        return ref[slab, pl.ds(i, 1), :]

    for slab in range(2):
        def sa_first(c, acc, slab=slab):
            return acc + s_ref[slab, c] * row(kk_s, slab, c)

        def step(t, sa, slab=slab):
            base = t * ROW_PITCH
            ahead = jnp.minimum(t + 1, tb - 1) * ROW_PITCH
            tile = pl.ds(pl.multiple_of(t * PITCH, SUBLANE), n)
            vv = v_s[slab, tile, :]

            def channel(c, carry):
                y_acc, sa_next = carry
                s_new = (s_ref[slab, c] - sa * row(b_s, slab, base + c)
                         + vv * row(k_s, slab, base + c))
                s_ref[slab, c] = s_new
                return (y_acc + s_new * row(r_s, slab, base + c),
                        sa_next + s_new * row(kk_s, slab, ahead + c))

            y, sa_next = lax.fori_loop(0, n, channel, (zero, zero), unroll=True)
            y_s[slab, tile, :] = y
            return sa_next

        lax.fori_loop(0, tb, step, lax.fori_loop(0, n, sa_first, zero, unroll=8))

        def rescale(c, carry, slab=slab):
            s_ref[slab, c] = s_ref[slab, c] * row(gam_s, slab, (tb - 1) * ROW_PITCH + c)
            return carry

        lax.fori_loop(0, n, rescale, 0, unroll=8)

    for q in range(tq):
        for slab in range(2):
            lanes = pl.ds(slab * LANE, LANE)
            y = jnp.stack([y_s[slab, pl.ds(q * SUBLANE * PITCH + c, SUBLANE, stride=PITCH), :]
                           for c in range(n)], axis=0)
            mean = jnp.mean(y, axis=0, keepdims=True)
            yc = y - mean
            var = jnp.mean(yc * yc, axis=0, keepdims=True)
            y = yc * lax.rsqrt(var + GN_EPS) * gnw_ref[:, :, lanes] + gnb_ref[:, :, lanes]
            y = y + bv_s[q, :, :, lanes]
            g = g_ref[q, :, :, lanes]
            o_ref[q, :, :, lanes] = y * (g * _sigmoid(g))


def _scan(p3, g4, dec4, a4, head_params, side_casts=(), tb=SCAN_TB):
    _, tqs, n, _, lanes = p3.shape
    tq = min(tb // SUBLANE, tqs)
    steps = tqs // tq
    cast_specs = [pl.BlockSpec((w.shape[0] // steps, w.shape[1]), lambda i: (i, 0))
                  for w in side_casts]
    tile = (n, SUBLANE, lanes)
    seg = lambda s: pl.BlockSpec((1, tq) + tile, lambda i, s=s: (s, i, 0, 0, 0))
    blk = pl.BlockSpec((tq,) + tile, lambda i: (i, 0, 0, 0))
    par = pl.BlockSpec(tile, lambda i: (0, 0, 0))
    rows = (2, tq * SUBLANE * ROW_PITCH, LANE)
    tiles = (2, tq * SUBLANE * PITCH, LANE)
    scratch = [pltpu.VMEM((2, n, n, LANE), jnp.float32),
               pltpu.VMEM(tile, jnp.float32),
               pltpu.VMEM((tq,) + tile, jnp.float32)]
    scratch += [pltpu.VMEM(rows, jnp.float32) for _ in range(5)]
    scratch += [pltpu.VMEM(tiles, jnp.float32) for _ in range(2)]
    outs = pl.pallas_call(
        functools.partial(_scan_kernel, n_casts=len(side_casts)),
        grid=(steps,),
        in_specs=([seg(0), seg(1), seg(2), blk, blk, blk] + [par] * len(head_params)
                  + cast_specs),
        out_specs=[blk] + cast_specs,
        out_shape=[jax.ShapeDtypeStruct((tqs,) + tile, jnp.float32)]
        + [jax.ShapeDtypeStruct(w.shape, jnp.bfloat16) for w in side_casts],
        scratch_shapes=scratch,
        compiler_params=_params("arbitrary"),
        name="rwkv_scan",
    )(p3, p3, p3, g4, dec4, a4, *head_params, *side_casts)
    return outs[0], outs[1:]


def _outproj_lanes_kernel(y_ref, w_ref, r_ref, o_ref, lhs_ref):
    tq = y_ref.shape[0]
    tm = tq * SUBLANE
    nb, _, d = lhs_ref.shape

    @pl.when(pl.program_id(1) == 0)
    def _():
        low = lax.broadcasted_iota(jnp.int32, (tm, LANE), 1) < HEAD_DIM
        for c in range(HEAD_DIM // 2):
            cols = pl.ds(c * LANE, LANE)
            for bp in range(2):
                lanes = pl.ds(bp * LANE, LANE)
                e = y_ref[:, 2 * c, :, lanes].reshape(tm, LANE)
                o = y_ref[:, 2 * c + 1, :, lanes].reshape(tm, LANE)
                lhs_ref[2 * bp, :, cols] = jnp.where(
                    low, e, pltpu.roll(o, HEAD_DIM, axis=1)).astype(lhs_ref.dtype)
                lhs_ref[2 * bp + 1, :, cols] = jnp.where(
                    low, pltpu.roll(e, HEAD_DIM, axis=1), o).astype(lhs_ref.dtype)

    res = jnp.dot(lhs_ref[...].reshape(nb * tm, d), w_ref[...],
                  preferred_element_type=jnp.float32)
    o_ref[...] = r_ref[...] + res.reshape(o_ref.shape)


def _outproj_lanes(y4, w_perm, residual3, tm=256, tn=512):
    batch, seq, d = residual3.shape
    tq = tm // SUBLANE
    ospec = pl.BlockSpec((batch, tm, tn), lambda i, j: (0, i, j))
    return pl.pallas_call(
        _outproj_lanes_kernel,
        grid=(seq // tm, d // tn),
        in_specs=[pl.BlockSpec((tq, HEAD_DIM, SUBLANE, 2 * LANE), lambda i, j: (i, 0, 0, 0),
                               pipeline_mode=pl.Buffered(1)),
                  pl.BlockSpec((d, tn), lambda i, j: (0, j)),
                  ospec],
        out_specs=ospec,
        out_shape=jax.ShapeDtypeStruct((batch, seq, d), jnp.float32),
        scratch_shapes=[pltpu.VMEM((batch, tm, d), jnp.bfloat16)],
        compiler_params=_params("parallel", "arbitrary"),
        name="rwkv_out_proj",
    )(y4, w_perm, residual3)


def _cols_nh(w, segs=1):
    lead = w.shape[:-1]
    return w.reshape(lead + (segs, -1, HEAD_DIM)).swapaxes(-1, -2).reshape(w.shape)


def _transpose_nh_kernel(w_ref, o_ref, t_ref):
    tk = w_ref.shape[0]
    n = HEAD_DIM
    heads = w_ref.shape[1] // n
    slabs = tk // LANE
    for pair in range(heads // 2):
        wt = w_ref[:, pair * LANE:(pair + 1) * LANE].T
        for half in range(2):
            for slab in range(slabs):
                t_ref[slab, pl.ds((2 * pair + half) * PITCH, n), :] = (
                    wt[half * n:(half + 1) * n, slab * LANE:(slab + 1) * LANE])
    rows_per_store = 2 * SUBLANE

    def per_channel(c, carry):
        for slab in range(slabs):
            for g in range(heads // rows_per_store):
                parts = [t_ref[slab, pl.ds((g * 2 + u) * SUBLANE * PITCH + c, SUBLANE, stride=PITCH), :]
                         for u in range(2)]
                row0 = pl.multiple_of(c * heads + g * rows_per_store, rows_per_store)
                o_ref[pl.ds(row0, rows_per_store), slab * LANE:(slab + 1) * LANE] = (
                    jnp.concatenate(parts, axis=0).astype(o_ref.dtype))
        return carry

    lax.fori_loop(0, n, per_channel, 0)


def _transpose_nh(w, col0, segs, tk=512):
    k = w.shape[0]
    d = 64 * HEAD_DIM
    assert col0 % d == 0
    return pl.pallas_call(
        _transpose_nh_kernel,
        grid=(segs, k // tk),
        in_specs=[pl.BlockSpec((tk, d), lambda s, i: (i, col0 // d + s))],
        out_specs=pl.BlockSpec((d, tk), lambda s, i: (s, i)),
        out_shape=jax.ShapeDtypeStruct((segs * d, k), jnp.bfloat16),
        scratch_shapes=[pltpu.VMEM((tk // LANE, 64 * PITCH, LANE), jnp.float32)],
        compiler_params=_params("parallel", "parallel"),
        name="weight_transpose",
    )(w)


def _param_lanes(p, batch):
    pt = jnp.tile(p.reshape(-1, HEAD_DIM).T, (1, batch))
    return jnp.broadcast_to(pt[:, None, :], (HEAD_DIM, SUBLANE, pt.shape[1]))


def _rwkv_layer(h2d, batch, seq, norm_w, w_in, mu, w0, w2, a0, a2, k_k, k_a, r_k, gn_w,
                gn_b, w_out, side_casts=()):
    d = h2d.shape[1]
    bf = jnp.bfloat16
    x_cols = slice(3 * d, 3 * d + 2 * LORA)
    x, p3, g4 = _inproj_lanes(h2d.reshape(batch, seq, d), norm_w, w_in[:, x_cols].astype(bf),
                              _cols_nh(mu[:3 * d], 3),
                              _transpose_nh(w_in, 0, 3),
                              _transpose_nh(w_in[:, 3 * d + 2 * LORA:], 0, 1))
    dec4, a4 = _lora_lanes(x, mu[x_cols],
                           _cols_nh(w2).astype(bf), _cols_nh(a2).astype(bf),
                           _cols_nh(w0), _cols_nh(a0))
    head_params = [_param_lanes(x, batch) for x in (k_k, k_a, r_k.reshape(-1), gn_w, gn_b)]
    w_out_perm = w_out.reshape(-1, HEAD_DIM, d).swapaxes(0, 1).reshape(d, d)
    y4, casted = _scan(p3, g4, dec4, a4, head_params, (w_out_perm,) + tuple(side_casts))
    out = _outproj_lanes(y4, casted[0], h2d.reshape(batch, seq, d))
    return out.reshape(batch * seq, d), casted[1:]


def _attn_kernel(sinks_ref, q_ref, kc_ref, kp_ref, vc_ref, vp_ref, g0_ref, g1_ref, g2_ref,
                 g3_ref, o_ref):
    bf = jnp.bfloat16
    n = HEAD_DIM
    pairs = GROUP // 2
    nq = pairs * BLOCK
    not_first = pl.program_id(1) > 0
    kj = lax.broadcasted_iota(jnp.int32, (2 * BLOCK, nq), 0)
    qi = lax.broadcasted_iota(jnp.int32, (2 * BLOCK, nq), 1) & (BLOCK - 1)
    delta = BLOCK + qi - kj
    mask = (delta >= 0) & (delta < BLOCK) & ((kj >= BLOCK) | not_first)
    sink_slot = kj == 0
    low_kv = lax.broadcasted_iota(jnp.int32, (2 * BLOCK, LANE), 1) < n
    key0_kv = lax.broadcasted_iota(jnp.int32, (2 * BLOCK, LANE), 0) == 0
    gate_refs = (g0_ref, g1_ref, g2_ref, g3_ref)
    gate_w = g0_ref.shape[1]
    for hk in range(KV_HEADS):
        cols = slice((hk // 2) * LANE, (hk // 2 + 1) * LANE)
        kt = jnp.concatenate([kp_ref[:, cols], kc_ref[:, cols]], axis=0)
        vt = jnp.concatenate([vp_ref[:, cols], vc_ref[:, cols]], axis=0)
        kr = pltpu.roll(kt, n, axis=1)
        vr = pltpu.roll(vt, n, axis=1)
        if hk % 2 == 0:
            k_low, k_high = jnp.where(low_kv, kt, 0.0), jnp.where(low_kv, 0.0, kr)
            v_both = jnp.where(low_kv, vt, vr)
        else:
            k_low, k_high = jnp.where(low_kv, kr, 0.0), jnp.where(low_kv, 0.0, kt)
            v_both = jnp.where(low_kv, vr, vt)
        v_t = jnp.where(key0_kv, 0.0, v_both).T.astype(bf)
        qs = jnp.concatenate([q_ref[:, (hk * pairs + p) * LANE:(hk * pairs + p + 1) * LANE]
                              for p in range(pairs)], axis=0)
        qs = (qs * (n ** -0.5)).astype(bf)
        outs = []
        for parity, k_sel in ((0, k_low), (1, k_high)):
            s = lax.dot_general(k_sel.astype(bf), qs, (((1,), (1,)), ((), ())),
                                preferred_element_type=jnp.float32)
            sink = jnp.concatenate(
                [jnp.full((1, BLOCK), sinks_ref[hk * GROUP + 2 * p + parity], jnp.float32)
                 for p in range(pairs)], axis=1)
            s = jnp.where(mask, s, jnp.where(sink_slot, sink, MASK_VALUE))
            m = jnp.max(s, axis=0, keepdims=True)
            e = jnp.exp(s - m)
            inv = 1.0 / jnp.sum(e, axis=0, keepdims=True)
            outs.append(jnp.dot(v_t, (e * inv).astype(bf), preferred_element_type=jnp.float32))
        for p in range(pairs):
            col = (hk * pairs + p) * LANE
            qcols = slice(p * BLOCK, (p + 1) * BLOCK)
            o = jnp.concatenate([outs[0][:n, qcols], outs[1][n:, qcols]], axis=0).T
            g_ref = gate_refs[col // gate_w]
            gate = g_ref[:, col % gate_w:col % gate_w + LANE]
            o_ref[:, col:col + LANE] = (o * (gate * _sigmoid(gate))).astype(o_ref.dtype)


def _attention(p, batch, seq, d, sinks):
    nb = seq // BLOCK
    kvw = KV_HEADS * HEAD_DIM
    gate_w = 2 * kvw
    row = lambda b, n: b * nb + n
    prow = lambda b, n: b * nb + jnp.maximum(n - 1, 0)
    kblk, vblk = d // kvw, d // kvw + 1
    gblk = (d + 2 * kvw) // gate_w
    in_specs = [pl.BlockSpec(memory_space=pltpu.SMEM),
                pl.BlockSpec((BLOCK, d), lambda b, n: (row(b, n), 0)),
                pl.BlockSpec((BLOCK, kvw), lambda b, n: (row(b, n), kblk)),
                pl.BlockSpec((BLOCK, kvw), lambda b, n: (prow(b, n), kblk)),
                pl.BlockSpec((BLOCK, kvw), lambda b, n: (row(b, n), vblk)),
                pl.BlockSpec((BLOCK, kvw), lambda b, n: (prow(b, n), vblk))]
    in_specs += [pl.BlockSpec((BLOCK, gate_w), lambda b, n, j=j: (row(b, n), gblk + j))
                 for j in range(4)]
    return pl.pallas_call(
        _attn_kernel,
        grid=(batch, nb),
        in_specs=in_specs,
        out_specs=pl.BlockSpec((BLOCK, d), lambda b, n: (row(b, n), 0)),
        out_shape=jax.ShapeDtypeStruct((batch * seq, d), jnp.bfloat16),
        compiler_params=_params("parallel", "parallel"),
        name="swa_attention",
    )(sinks, p, p, p, p, p, p, p, p, p)


def _attn_layer(h2d, batch, seq, norm_w, w_in_bf, sinks, w_out_bf):
    d = h2d.shape[1]
    hn = _rms_norm(h2d, norm_w, jnp.bfloat16)
    p = _matmul(hn, w_in_bf, tn=1024, name="attn_in_proj")
    o = _attention(p, batch, seq, d, sinks)
    return _matmul(o, w_out_bf, residual=h2d, name="attn_out_proj")


def kernel(x, norm_w, final_norm_w, rwkv_w_in, rwkv_mu, rwkv_w0, rwkv_w2, rwkv_a0, rwkv_a2,
           rwkv_k_k, rwkv_k_a, rwkv_r_k, rwkv_gn_w, rwkv_gn_b, rwkv_w_out,
           attn_w_in, attn_sinks, attn_w_out):
    batch, seq, d = x.shape
    h = x.reshape(batch * seq, d)
    depth = norm_w.shape[0]
    bf_weights = {}
    for i in range(depth):
        j = i // 2
        if i % 2 == 0:
            ahead = (attn_w_in[j], attn_w_out[j]) if i + 1 < depth else ()
            h, casted = _rwkv_layer(h, batch, seq, norm_w[i], rwkv_w_in[j], rwkv_mu[j],
                                    rwkv_w0[j], rwkv_w2[j], rwkv_a0[j], rwkv_a2[j], rwkv_k_k[j],
                                    rwkv_k_a[j], rwkv_r_k[j], rwkv_gn_w[j], rwkv_gn_b[j],
                                    rwkv_w_out[j], ahead)
            if ahead:
                bf_weights[i + 1] = casted
        else:
            w_in_bf, w_out_bf = bf_weights.get(i) or (attn_w_in[j].astype(jnp.bfloat16),
                                                      attn_w_out[j].astype(jnp.bfloat16))
            h = _attn_layer(h, batch, seq, norm_w[i], w_in_bf, attn_sinks[j], w_out_bf)
    return _rms_norm(h, final_norm_w, jnp.float32).reshape(batch, seq, d)
```

```python
import functools

import jax
import jax.numpy as jnp
import numpy as np
from jax import lax
from jax.experimental import pallas as pl
from jax.experimental.pallas import tpu as pltpu

HEAD_DIM = 64
LORA = 128
KV_HEADS = 8
GROUP = 8
BLOCK = 128
NORM_EPS = 1e-5
GN_EPS = HEAD_DIM * 1e-5
L2_EPS = 1e-12
MASK_VALUE = -1e30

LANE = 128
SUBLANE = 8
VMEM_LIMIT_BYTES = 56 * 1024 * 1024
SCAN_TB = 16
PITCH = HEAD_DIM + SUBLANE
ROW_PITCH = HEAD_DIM + 4


def _sigmoid(x):
    return 0.5 * jnp.tanh(0.5 * x) + 0.5


def _params(*sem):
    return pltpu.CompilerParams(dimension_semantics=sem, vmem_limit_bytes=VMEM_LIMIT_BYTES)


def _rms_kernel(x_ref, g_ref, o_ref):
    x = x_ref[...]
    ms = jnp.mean(x * x, axis=-1, keepdims=True)
    o_ref[...] = (x * lax.rsqrt(ms + NORM_EPS) * g_ref[...]).astype(o_ref.dtype)


def _rms_norm(x2d, g, out_dtype, tm=256):
    m, d = x2d.shape
    return pl.pallas_call(
        _rms_kernel,
        grid=(m // tm,),
        in_specs=[pl.BlockSpec((tm, d), lambda i: (i, 0)),
                  pl.BlockSpec((1, d), lambda i: (0, 0))],
        out_specs=pl.BlockSpec((tm, d), lambda i: (i, 0)),
        out_shape=jax.ShapeDtypeStruct((m, d), out_dtype),
        compiler_params=_params("parallel"),
        name="rmsnorm",
    )(x2d, g.reshape(1, d))


def _mm_kernel(a_ref, b_ref, o_ref):
    o_ref[...] = jnp.dot(a_ref[...], b_ref[...], preferred_element_type=jnp.float32)


def _mm_res_kernel(a_ref, b_ref, r_ref, o_ref):
    o_ref[...] = r_ref[...] + jnp.dot(a_ref[...], b_ref[...],
                                      preferred_element_type=jnp.float32)


def _matmul(a, b, residual=None, tm=512, tn=1024, name="matmul"):
    m, k = a.shape
    _, n = b.shape
    tm = min(tm, m)
    grid = (n // tn, m // tm)
    in_specs = [pl.BlockSpec((tm, k), lambda j, i: (i, 0)),
                pl.BlockSpec((k, tn), lambda j, i: (0, j))]
    args = [a, b]
    kern = _mm_kernel
    if residual is not None:
        in_specs.append(pl.BlockSpec((tm, tn), lambda j, i: (i, j)))
        args.append(residual)
        kern = _mm_res_kernel
    return pl.pallas_call(
        kern,
        grid=grid,
        in_specs=in_specs,
        out_specs=pl.BlockSpec((tm, tn), lambda j, i: (i, j)),
        out_shape=jax.ShapeDtypeStruct((m, n), jnp.float32),
        compiler_params=_params("parallel", "parallel"),
        name=name,
    )(*args)


def _store_lanes(o_ref, lead, res, tm):
    low = lax.broadcasted_iota(jnp.int32, (tm, LANE), 1) < HEAD_DIM
    for bp in range(2):
        ra = res[(2 * bp) * tm:(2 * bp + 1) * tm]
        rb = res[(2 * bp + 1) * tm:(2 * bp + 2) * tm]
        for c in range(res.shape[1] // LANE):
            a = ra[:, c * LANE:(c + 1) * LANE]
            b = rb[:, c * LANE:(c + 1) * LANE]
            even = jnp.where(low, a, pltpu.roll(b, HEAD_DIM, axis=1))
            odd = jnp.where(low, pltpu.roll(a, HEAD_DIM, axis=1), b)
            lanes = pl.ds(bp * LANE, LANE)
            shape = (tm // SUBLANE, SUBLANE, LANE)
            o_ref[lead + (slice(None), 2 * c, slice(None), lanes)] = even.reshape(shape)
            o_ref[lead + (slice(None), 2 * c + 1, slice(None), lanes)] = odd.reshape(shape)


def _inproj_lanes_kernel(x_ref, g_ref, wx_ref, mu_ref, ws_ref, wg_ref,
                         xo_ref, ps_ref, pg_ref, hn_ref, carry_ref, *, shifted_blocks):
    nb, tm, d = x_ref.shape
    tn = ws_ref.shape[0]
    i, j = pl.program_id(0), pl.program_id(1)

    @pl.when(j == 0)
    def _():
        for b in range(nb):
            x = x_ref[b]
            ms = jnp.mean(x * x, axis=-1, keepdims=True)
            hn_ref[b * tm:(b + 1) * tm, :] = (
                x * lax.rsqrt(ms + NORM_EPS) * g_ref[...]).astype(hn_ref.dtype)
        xo_ref[...] = jnp.dot(hn_ref[...], wx_ref[...],
                              preferred_element_type=jnp.float32).reshape(xo_ref.shape)

    @pl.when(j < shifted_blocks)
    def _():
        res = lax.dot_general(hn_ref[...], ws_ref[...], (((1,), (1,)), ((), ())),
                              preferred_element_type=jnp.float32)
        prev = jnp.where(i == 0, 0.0, carry_ref[j])
        row0 = lax.broadcasted_iota(jnp.int32, (tm, tn), 0) == 0
        parts, lasts = [], []
        for b in range(nb):
            rb = res[b * tm:(b + 1) * tm]
            before = jnp.where(row0, prev[b:b + 1], pltpu.roll(rb, 1, axis=0))
            parts.append(rb + (before - rb) * mu_ref[...])
            lasts.append(rb[tm - 1:tm])
        carry_ref[j] = jnp.concatenate(lasts, axis=0)
        _store_lanes(ps_ref, (0,), jnp.concatenate(parts, axis=0), tm)

    @pl.when(j >= shifted_blocks)
    def _():
        res = lax.dot_general(hn_ref[...], wg_ref[...], (((1,), (1,)), ((), ())),
                              preferred_element_type=jnp.float32)
        _store_lanes(pg_ref, (), res, tm)


def _inproj_lanes(x3, norm_w, w_x, mu_s, w_shift, w_gate, tm=256, tn=512):
    batch, seq, d = x3.shape
    assert batch == 4 and d == 64 * HEAD_DIM
    nv = tn // HEAD_DIM
    per_seg = d // tn
    ns = w_shift.shape[0] // tn
    ng = w_gate.shape[0] // tn
    tile = (HEAD_DIM, SUBLANE, 2 * LANE)
    blk = (tm // SUBLANE, nv, SUBLANE, 2 * LANE)
    js = lambda j: jnp.minimum(j, ns - 1)
    jg = lambda j: jnp.maximum(j - ns, 0)
    return pl.pallas_call(
        functools.partial(_inproj_lanes_kernel, shifted_blocks=ns),
        grid=(seq // tm, ns + ng),
        in_specs=[pl.BlockSpec((batch, tm, d), lambda i, j: (0, i, 0),
                               pipeline_mode=pl.Buffered(1)),
                  pl.BlockSpec((1, d), lambda i, j: (0, 0)),
                  pl.BlockSpec((d, w_x.shape[1]), lambda i, j: (0, 0)),
                  pl.BlockSpec((1, tn), lambda i, j: (0, js(j))),
                  pl.BlockSpec((tn, d), lambda i, j: (js(j), 0)),
                  pl.BlockSpec((tn, d), lambda i, j: (jg(j), 0))],
        out_specs=[pl.BlockSpec((batch, tm, w_x.shape[1]), lambda i, j: (0, i, 0)),
                   pl.BlockSpec((1,) + blk,
                                lambda i, j: (js(j) // per_seg, i, js(j) % per_seg, 0, 0)),
                   pl.BlockSpec(blk, lambda i, j: (i, jg(j), 0, 0))],
        out_shape=[jax.ShapeDtypeStruct((batch, seq, w_x.shape[1]), jnp.float32),
                   jax.ShapeDtypeStruct((ns // per_seg, seq // SUBLANE) + tile, jnp.float32),
                   jax.ShapeDtypeStruct((seq // SUBLANE,) + tile, jnp.float32)],
        scratch_shapes=[pltpu.VMEM((batch * tm, d), jnp.bfloat16),
                        pltpu.VMEM((ns, batch, tn), jnp.float32)],
        compiler_params=_params("arbitrary", "arbitrary"),
        name="rwkv_in_proj",
    )(x3, norm_w.reshape(1, d), w_x, mu_s.reshape(1, -1), w_shift, w_gate)


def _lora_kernel(x_ref, xp_ref, mu_ref, w2_ref, a2_ref, w0_ref, a0_ref, dec_ref, a_ref):
    nb, tm, _ = x_ref.shape
    first = pl.program_id(0) == 0
    row0 = lax.broadcasted_iota(jnp.int32, (tm, 2 * LORA), 0) == 0
    xw, xa = [], []
    for b in range(nb):
        x = x_ref[b]
        last = jnp.where(first, 0.0, xp_ref[b][SUBLANE - 1:SUBLANE])
        before = jnp.where(row0, last, pltpu.roll(x, 1, axis=0))
        x = x + (before - x) * mu_ref[...]
        xw.append(jnp.tanh(x[:, :LORA]).astype(jnp.bfloat16))
        xa.append(x[:, LORA:].astype(jnp.bfloat16))
    wl = w0_ref[...] + jnp.dot(jnp.concatenate(xw, axis=0), w2_ref[...],
                               preferred_element_type=jnp.float32)
    al = a0_ref[...] + jnp.dot(jnp.concatenate(xa, axis=0), a2_ref[...],
                               preferred_element_type=jnp.float32)
    sig_w = _sigmoid(wl)
    _store_lanes(dec_ref, (), jnp.exp(sig_w * (-float(np.exp(-0.5)))), tm)
    _store_lanes(a_ref, (), _sigmoid(al), tm)


def _lora_lanes(x3, mu_x, w2p, a2p, w0p, a0p, tm=256, tn=512):
    batch, seq, _ = x3.shape
    d = w2p.shape[1]
    nv = tn // HEAD_DIM
    out = jax.ShapeDtypeStruct((seq // SUBLANE, HEAD_DIM, SUBLANE, 2 * LANE), jnp.float32)
    ospec = pl.BlockSpec((tm // SUBLANE, nv, SUBLANE, 2 * LANE), lambda i, j: (i, j, 0, 0))
    tq = tm // SUBLANE
    return pl.pallas_call(
        _lora_kernel,
        grid=(seq // tm, d // tn),
        in_specs=[pl.BlockSpec((batch, tm, 2 * LORA), lambda i, j: (0, i, 0)),
                  pl.BlockSpec((batch, SUBLANE, 2 * LORA),
                               lambda i, j: (0, jnp.maximum(i * tq - 1, 0), 0)),
                  pl.BlockSpec((1, 2 * LORA), lambda i, j: (0, 0)),
                  pl.BlockSpec((LORA, tn), lambda i, j: (0, j)),
                  pl.BlockSpec((LORA, tn), lambda i, j: (0, j)),
                  pl.BlockSpec((1, tn), lambda i, j: (0, j)),
                  pl.BlockSpec((1, tn), lambda i, j: (0, j))],
        out_specs=[ospec, ospec],
        out_shape=[out, out],
        compiler_params=_params("parallel", "arbitrary"),
        name="rwkv_lora",
    )(x3, x3, mu_x.reshape(1, -1), w2p, a2p, w0p.reshape(1, -1), a0p.reshape(1, -1))


def _scan_kernel(*refs, n_casts):
    (r_ref, k_ref, v_ref, g_ref, w_ref, a_ref,
     kkp_ref, kap_ref, rkp_ref, gnw_ref, gnb_ref) = refs[:11]
    cast_in = refs[11:11 + n_casts]
    o_ref = refs[11 + n_casts]
    cast_out = refs[12 + n_casts:12 + 2 * n_casts]
    s_ref, gcar_ref, bv_s, r_s, gam_s, k_s, kk_s, b_s, v_s, y_s = refs[12 + 2 * n_casts:]
    for src, dst in zip(cast_in, cast_out):
        dst[...] = src[...].astype(dst.dtype)
    tq = w_ref.shape[0]
    tb = tq * SUBLANE
    n = HEAD_DIM
    ch = SUBLANE

    @pl.when(pl.program_id(0) == 0)
    def _():
        s_ref[...] = jnp.zeros_like(s_ref)

    sub = lax.broadcasted_iota(jnp.int32, (ch, SUBLANE, 2 * LANE), 1)

    def to_steps(dst, pitch, q, c0, val):
        for c in range(ch):
            for slab in range(2):
                dst[slab, pl.ds(q * SUBLANE * pitch + c0 + c, SUBLANE, stride=pitch), :] = (
                    val[c, :, slab * LANE:(slab + 1) * LANE])

    def k_terms(q, cs):
        k = k_ref[0, q, cs]
        a = a_ref[q, cs]
        return k * kkp_ref[cs], k * (1.0 + (a - 1.0) * kap_ref[cs]), a

    for q in range(tq):
        def sums(j, carry, q=q):
            ssq, bon = carry
            cs = pl.ds(pl.multiple_of(j * ch, ch), ch)
            kkr, k2, _ = k_terms(q, cs)
            return ssq + kkr * kkr, bon + r_ref[0, q, cs] * k2 * rkp_ref[cs]

        zeros = jnp.zeros((ch, SUBLANE, 2 * LANE), jnp.float32)
        ssq, bon = lax.fori_loop(0, n // ch, sums, (zeros, zeros))
        nrm = jnp.sqrt(jnp.sum(ssq, axis=0, keepdims=True))
        inv_nrm = 1.0 / jnp.maximum(nrm, L2_EPS)
        bonus = jnp.sum(bon, axis=0, keepdims=True)

        def emit(j, carry, q=q, inv_nrm=inv_nrm, bonus=bonus):
            c0 = pl.multiple_of(j * ch, ch)
            cs = pl.ds(c0, ch)
            kkr, k2, a = k_terms(q, cs)
            v = v_ref[0, q, cs]
            kk = kkr * inv_nrm
            gam = w_ref[q, cs]
            for sh in (1, 2, 4):
                gam = gam * jnp.where(sub >= sh, pltpu.roll(gam, sh, axis=1), 1.0)
            before_tile = 1.0 if q == 0 else gcar_ref[cs]
            gam = gam * before_tile
            gam_prev = jnp.where(sub == 0, before_tile, pltpu.roll(gam, 1, axis=1))
            inv_gam = 1.0 / gam
            if q + 1 < tq:
                gcar_ref[cs] = jnp.broadcast_to(gam[:, SUBLANE - 1:SUBLANE, :], gam.shape)
            bv_s[q, cs] = bonus * v
            to_steps(r_s, ROW_PITCH, q, c0, r_ref[0, q, cs] * gam)
            to_steps(gam_s, ROW_PITCH, q, c0, gam)
            to_steps(k_s, ROW_PITCH, q, c0, k2 * inv_gam)
            to_steps(kk_s, ROW_PITCH, q, c0, kk * gam_prev)
            to_steps(b_s, ROW_PITCH, q, c0, kk * a * inv_gam)
            to_steps(v_s, PITCH, q, c0, v)
            return carry

        lax.fori_loop(0, n // ch, emit, 0)

    zero = jnp.zeros((n, LANE), jnp.float32)

    def row(ref, slab, i):
        return ref[slab, pl.ds(i, 1), :]

    for slab in range(2):
        def sa_first(c, acc, slab=slab):
            return acc + s_ref[slab, c] * row(kk_s, slab, c)

        def step(t, sa, slab=slab):
            base = t * ROW_PITCH
            ahead = jnp.minimum(t + 1, tb - 1) * ROW_PITCH
            tile = pl.ds(pl.multiple_of(t * PITCH, SUBLANE), n)
            vv = v_s[slab, tile, :]

            def channel(c, carry):
                y_acc, sa_next = carry
                s_new = (s_ref[slab, c] - sa * row(b_s, slab, base + c)
                         + vv * row(k_s, slab, base + c))
                s_ref[slab, c] = s_new
                return (y_acc + s_new * row(r_s, slab, base + c),
                        sa_next + s_new * row(kk_s, slab, ahead + c))

            y, sa_next = lax.fori_loop(0, n, channel, (zero, zero), unroll=True)
            y_s[slab, tile, :] = y
            return sa_next

        lax.fori_loop(0, tb, step, lax.fori_loop(0, n, sa_first, zero, unroll=8))

        def rescale(c, carry, slab=slab):
            s_ref[slab, c] = s_ref[slab, c] * row(gam_s, slab, (tb - 1) * ROW_PITCH + c)
            return carry

        lax.fori_loop(0, n, rescale, 0, unroll=8)

    for q in range(tq):
        for slab in range(2):
            lanes = pl.ds(slab * LANE, LANE)
            y = jnp.stack([y_s[slab, pl.ds(q * SUBLANE * PITCH + c, SUBLANE, stride=PITCH), :]
                           for c in range(n)], axis=0)
            mean = jnp.mean(y, axis=0, keepdims=True)
            yc = y - mean
            var = jnp.mean(yc * yc, axis=0, keepdims=True)
            y = yc * lax.rsqrt(var + GN_EPS) * gnw_ref[:, :, lanes] + gnb_ref[:, :, lanes]
            y = y + bv_s[q, :, :, lanes]
            g = g_ref[q, :, :, lanes]
            o_ref[q, :, :, lanes] = y * (g * _sigmoid(g))


def _scan(p3, g4, dec4, a4, head_params, side_casts=(), tb=SCAN_TB):
    _, tqs, n, _, lanes = p3.shape
    tq = min(tb // SUBLANE, tqs)
    steps = tqs // tq
    cast_specs = [pl.BlockSpec((w.shape[0] // steps, w.shape[1]), lambda i: (i, 0))
                  for w in side_casts]
    tile = (n, SUBLANE, lanes)
    seg = lambda s: pl.BlockSpec((1, tq) + tile, lambda i, s=s: (s, i, 0, 0, 0))
    blk = pl.BlockSpec((tq,) + tile, lambda i: (i, 0, 0, 0))
    par = pl.BlockSpec(tile, lambda i: (0, 0, 0))
    rows = (2, tq * SUBLANE * ROW_PITCH, LANE)
    tiles = (2, tq * SUBLANE * PITCH, LANE)
    scratch = [pltpu.VMEM((2, n, n, LANE), jnp.float32),
               pltpu.VMEM(tile, jnp.float32),
               pltpu.VMEM((tq,) + tile, jnp.float32)]
    scratch += [pltpu.VMEM(rows, jnp.float32) for _ in range(5)]
    scratch += [pltpu.VMEM(tiles, jnp.float32) for _ in range(2)]
    outs = pl.pallas_call(
        functools.partial(_scan_kernel, n_casts=len(side_casts)),
        grid=(steps,),
        in_specs=([seg(0), seg(1), seg(2), blk, blk, blk] + [par] * len(head_params)
                  + cast_specs),
        out_specs=[blk] + cast_specs,
        out_shape=[jax.ShapeDtypeStruct((tqs,) + tile, jnp.float32)]
        + [jax.ShapeDtypeStruct(w.shape, jnp.bfloat16) for w in side_casts],
        scratch_shapes=scratch,
        compiler_params=_params("arbitrary"),
        name="rwkv_scan",
    )(p3, p3, p3, g4, dec4, a4, *head_params, *side_casts)
    return outs[0], outs[1:]


def _outproj_lanes_kernel(y_ref, w_ref, r_ref, o_ref, lhs_ref):
    tq = y_ref.shape[0]
    tm = tq * SUBLANE
    nb, _, d = lhs_ref.shape

    @pl.when(pl.program_id(1) == 0)
    def _():
        low = lax.broadcasted_iota(jnp.int32, (tm, LANE), 1) < HEAD_DIM
        for c in range(HEAD_DIM // 2):
            cols = pl.ds(c * LANE, LANE)
            for bp in range(2):
                lanes = pl.ds(bp * LANE, LANE)
                e = y_ref[:, 2 * c, :, lanes].reshape(tm, LANE)
                o = y_ref[:, 2 * c + 1, :, lanes].reshape(tm, LANE)
                lhs_ref[2 * bp, :, cols] = jnp.where(
                    low, e, pltpu.roll(o, HEAD_DIM, axis=1)).astype(lhs_ref.dtype)
                lhs_ref[2 * bp + 1, :, cols] = jnp.where(
                    low, pltpu.roll(e, HEAD_DIM, axis=1), o).astype(lhs_ref.dtype)

    res = jnp.dot(lhs_ref[...].reshape(nb * tm, d), w_ref[...],
                  preferred_element_type=jnp.float32)
    o_ref[...] = r_ref[...] + res.reshape(o_ref.shape)


def _outproj_lanes(y4, w_perm, residual3, tm=256, tn=512):
    batch, seq, d = residual3.shape
    tq = tm // SUBLANE
    ospec = pl.BlockSpec((batch, tm, tn), lambda i, j: (0, i, j))
    return pl.pallas_call(
        _outproj_lanes_kernel,
        grid=(seq // tm, d // tn),
        in_specs=[pl.BlockSpec((tq, HEAD_DIM, SUBLANE, 2 * LANE), lambda i, j: (i, 0, 0, 0),
                               pipeline_mode=pl.Buffered(1)),
                  pl.BlockSpec((d, tn), lambda i, j: (0, j)),
                  ospec],
        out_specs=ospec,
        out_shape=jax.ShapeDtypeStruct((batch, seq, d), jnp.float32),
        scratch_shapes=[pltpu.VMEM((batch, tm, d), jnp.bfloat16)],
        compiler_params=_params("parallel", "arbitrary"),
        name="rwkv_out_proj",
    )(y4, w_perm, residual3)


def _cols_nh(w, segs=1):
    lead = w.shape[:-1]
    return w.reshape(lead + (segs, -1, HEAD_DIM)).swapaxes(-1, -2).reshape(w.shape)


def _transpose_nh_kernel(w_ref, o_ref, t_ref):
    tk = w_ref.shape[0]
    n = HEAD_DIM
    heads = w_ref.shape[1] // n
    slabs = tk // LANE
    for pair in range(heads // 2):
        wt = w_ref[:, pair * LANE:(pair + 1) * LANE].T
        for half in range(2):
            for slab in range(slabs):
                t_ref[slab, pl.ds((2 * pair + half) * PITCH, n), :] = (
                    wt[half * n:(half + 1) * n, slab * LANE:(slab + 1) * LANE])
    rows_per_store = 2 * SUBLANE

    def per_channel(c, carry):
        for slab in range(slabs):
            for g in range(heads // rows_per_store):
                parts = [t_ref[slab, pl.ds((g * 2 + u) * SUBLANE * PITCH + c, SUBLANE, stride=PITCH), :]
                         for u in range(2)]
                row0 = pl.multiple_of(c * heads + g * rows_per_store, rows_per_store)
                o_ref[pl.ds(row0, rows_per_store), slab * LANE:(slab + 1) * LANE] = (
                    jnp.concatenate(parts, axis=0).astype(o_ref.dtype))
        return carry

    lax.fori_loop(0, n, per_channel, 0, unroll=4)


def _transpose_nh(w, col0, segs, tk=512):
    k = w.shape[0]
    d = 64 * HEAD_DIM
    assert col0 % LANE == 0
    return pl.pallas_call(
        _transpose_nh_kernel,
        grid=(segs, k // tk),
        in_specs=[pl.BlockSpec((pl.Element(tk), pl.Element(d)),
                               lambda s, i: (pl.multiple_of(i * tk, tk), pl.multiple_of(col0 + s * d, LANE)))],
        out_specs=pl.BlockSpec((d, tk), lambda s, i: (s, i)),
        out_shape=jax.ShapeDtypeStruct((segs * d, k), jnp.bfloat16),
        scratch_shapes=[pltpu.VMEM((tk // LANE, 64 * PITCH, LANE), jnp.float32)],
        compiler_params=_params("parallel", "parallel"),
        name="weight_transpose",
    )(w)


def _param_lanes(p, batch):
    pt = jnp.tile(p.reshape(-1, HEAD_DIM).T, (1, batch))
    return jnp.broadcast_to(pt[:, None, :], (HEAD_DIM, SUBLANE, pt.shape[1]))


def _rwkv_layer(h2d, batch, seq, norm_w, w_in, mu, w0, w2, a0, a2, k_k, k_a, r_k, gn_w,
                gn_b, w_out, side_casts=()):
    d = h2d.shape[1]
    bf = jnp.bfloat16
    x_cols = slice(3 * d, 3 * d + 2 * LORA)
    x, p3, g4 = _inproj_lanes(h2d.reshape(batch, seq, d), norm_w, w_in[:, x_cols].astype(bf),
                              _cols_nh(mu[:3 * d], 3),
                              _transpose_nh(w_in, 0, 3),
                              _transpose_nh(w_in, 3 * d + 2 * LORA, 1))
    dec4, a4 = _lora_lanes(x, mu[x_cols],
                           _cols_nh(w2).astype(bf), _cols_nh(a2).astype(bf),
                           _cols_nh(w0), _cols_nh(a0))
    head_params = [_param_lanes(x, batch) for x in (k_k, k_a, r_k.reshape(-1), gn_w, gn_b)]
    w_out_perm = w_out.reshape(-1, HEAD_DIM, d).swapaxes(0, 1).reshape(d, d)
    y4, casted = _scan(p3, g4, dec4, a4, head_params, (w_out_perm,) + tuple(side_casts))
    out = _outproj_lanes(y4, casted[0], h2d.reshape(batch, seq, d))
    return out.reshape(batch * seq, d), casted[1:]


def _attn_kernel(sinks_ref, q_ref, kc_ref, kp_ref, vc_ref, vp_ref, g0_ref, g1_ref, g2_ref,
                 g3_ref, o_ref):
    bf = jnp.bfloat16
    n = HEAD_DIM
    pairs = GROUP // 2
    nq = pairs * BLOCK
    not_first = pl.program_id(1) > 0
    kj = lax.broadcasted_iota(jnp.int32, (2 * BLOCK, nq), 0)
    qi = lax.broadcasted_iota(jnp.int32, (2 * BLOCK, nq), 1) & (BLOCK - 1)
    delta = BLOCK + qi - kj
    mask = (delta >= 0) & (delta < BLOCK) & ((kj >= BLOCK) | not_first)
    sink_slot = kj == 0
    low_kv = lax.broadcasted_iota(jnp.int32, (2 * BLOCK, LANE), 1) < n
    key0_kv = lax.broadcasted_iota(jnp.int32, (2 * BLOCK, LANE), 0) == 0
    gate_refs = (g0_ref, g1_ref, g2_ref, g3_ref)
    gate_w = g0_ref.shape[1]
    for hk in range(KV_HEADS):
        cols = slice((hk // 2) * LANE, (hk // 2 + 1) * LANE)
        kt = jnp.concatenate([kp_ref[:, cols], kc_ref[:, cols]], axis=0)
        vt = jnp.concatenate([vp_ref[:, cols], vc_ref[:, cols]], axis=0)
        kr = pltpu.roll(kt, n, axis=1)
        vr = pltpu.roll(vt, n, axis=1)
        if hk % 2 == 0:
            k_low, k_high = jnp.where(low_kv, kt, 0.0), jnp.where(low_kv, 0.0, kr)
            v_both = jnp.where(low_kv, vt, vr)
        else:
            k_low, k_high = jnp.where(low_kv, kr, 0.0), jnp.where(low_kv, 0.0, kt)
            v_both = jnp.where(low_kv, vr, vt)
        v_t = jnp.where(key0_kv, 0.0, v_both).T.astype(bf)
        qs = jnp.concatenate([q_ref[:, (hk * pairs + p) * LANE:(hk * pairs + p + 1) * LANE]
                              for p in range(pairs)], axis=0)
        qs = (qs * (n ** -0.5)).astype(bf)
        outs = []
        for parity, k_sel in ((0, k_low), (1, k_high)):
            s = lax.dot_general(k_sel.astype(bf), qs, (((1,), (1,)), ((), ())),
                                preferred_element_type=jnp.float32)
            sink = jnp.concatenate(
                [jnp.full((1, BLOCK), sinks_ref[hk * GROUP + 2 * p + parity], jnp.float32)
                 for p in range(pairs)], axis=1)
            s = jnp.where(mask, s, jnp.where(sink_slot, sink, MASK_VALUE))
            m = jnp.max(s, axis=0, keepdims=True)
            e = jnp.exp(s - m)
            inv = 1.0 / jnp.sum(e, axis=0, keepdims=True)
            outs.append(jnp.dot(v_t, (e * inv).astype(bf), preferred_element_type=jnp.float32))
        for p in range(pairs):
            col = (hk * pairs + p) * LANE
            qcols = slice(p * BLOCK, (p + 1) * BLOCK)
            o = jnp.concatenate([outs[0][:n, qcols], outs[1][n:, qcols]], axis=0).T
            g_ref = gate_refs[col // gate_w]
            gate = g_ref[:, col % gate_w:col % gate_w + LANE]
            o_ref[:, col:col + LANE] = (o * (gate * _sigmoid(gate))).astype(o_ref.dtype)


def _attention(p, batch, seq, d, sinks):
    nb = seq // BLOCK
    kvw = KV_HEADS * HEAD_DIM
    gate_w = 2 * kvw
    row = lambda b, n: b * nb + n
    prow = lambda b, n: b * nb + jnp.maximum(n - 1, 0)
    kblk, vblk = d // kvw, d // kvw + 1
    gblk = (d + 2 * kvw) // gate_w
    in_specs = [pl.BlockSpec(memory_space=pltpu.SMEM),
                pl.BlockSpec((BLOCK, d), lambda b, n: (row(b, n), 0)),
                pl.BlockSpec((BLOCK, kvw), lambda b, n: (row(b, n), kblk)),
                pl.BlockSpec((BLOCK, kvw), lambda b, n: (prow(b, n), kblk)),
                pl.BlockSpec((BLOCK, kvw), lambda b, n: (row(b, n), vblk)),
                pl.BlockSpec((BLOCK, kvw), lambda b, n: (prow(b, n), vblk))]
    in_specs += [pl.BlockSpec((BLOCK, gate_w), lambda b, n, j=j: (row(b, n), gblk + j))
                 for j in range(4)]
    return pl.pallas_call(
        _attn_kernel,
        grid=(batch, nb),
        in_specs=in_specs,
        out_specs=pl.BlockSpec((BLOCK, d), lambda b, n: (row(b, n), 0)),
        out_shape=jax.ShapeDtypeStruct((batch * seq, d), jnp.bfloat16),
        compiler_params=_params("parallel", "parallel"),
        name="swa_attention",
    )(sinks, p, p, p, p, p, p, p, p, p)


def _attn_layer(h2d, batch, seq, norm_w, w_in_bf, sinks, w_out_bf):
    d = h2d.shape[1]
    hn = _rms_norm(h2d, norm_w, jnp.bfloat16)
    p = _matmul(hn, w_in_bf, tn=1024, name="attn_in_proj")
    o = _attention(p, batch, seq, d, sinks)
    return _matmul(o, w_out_bf, residual=h2d, name="attn_out_proj")


def kernel(x, norm_w, final_norm_w, rwkv_w_in, rwkv_mu, rwkv_w0, rwkv_w2, rwkv_a0, rwkv_a2,
           rwkv_k_k, rwkv_k_a, rwkv_r_k, rwkv_gn_w, rwkv_gn_b, rwkv_w_out,
           attn_w_in, attn_sinks, attn_w_out):
    batch, seq, d = x.shape
    h = x.reshape(batch * seq, d)
    depth = norm_w.shape[0]
    bf_weights = {}
    for i in range(depth):
        j = i // 2
        if i % 2 == 0:
            ahead = (attn_w_in[j], attn_w_out[j]) if i + 1 < depth else ()
            h, casted = _rwkv_layer(h, batch, seq, norm_w[i], rwkv_w_in[j], rwkv_mu[j],
                                    rwkv_w0[j], rwkv_w2[j], rwkv_a0[j], rwkv_a2[j], rwkv_k_k[j],
                                    rwkv_k_a[j], rwkv_r_k[j], rwkv_gn_w[j], rwkv_gn_b[j],
                                    rwkv_w_out[j], ahead)
            if ahead:
                bf_weights[i + 1] = casted
        else:
            w_in_bf, w_out_bf = bf_weights.get(i) or (attn_w_in[j].astype(jnp.bfloat16),
                                                      attn_w_out[j].astype(jnp.bfloat16))
            h = _attn_layer(h, batch, seq, norm_w[i], w_in_bf, attn_sinks[j], w_out_bf)
    return _rms_norm(h, final_norm_w, jnp.float32).reshape(batch, seq, d)
```

```python
import functools

import jax
import jax.numpy as jnp
import numpy as np
from jax import lax
from jax.experimental import pallas as pl
from jax.experimental.pallas import tpu as pltpu

HEAD_DIM = 64
LORA = 128
KV_HEADS = 8
GROUP = 8
BLOCK = 128
NORM_EPS = 1e-5
GN_EPS = HEAD_DIM * 1e-5
L2_EPS = 1e-12
MASK_VALUE = -1e30

LANE = 128
SUBLANE = 8
VMEM_LIMIT_BYTES = 56 * 1024 * 1024
SCAN_TB = 16
PITCH = HEAD_DIM + SUBLANE
ROW_PITCH = HEAD_DIM + 4


def _sigmoid(x):
    return 0.5 * jnp.tanh(0.5 * x) + 0.5


def _params(*sem):
    return pltpu.CompilerParams(dimension_semantics=sem, vmem_limit_bytes=VMEM_LIMIT_BYTES)


def _rms_kernel(x_ref, g_ref, o_ref):
    x = x_ref[...]
    ms = jnp.mean(x * x, axis=-1, keepdims=True)
    o_ref[...] = (x * lax.rsqrt(ms + NORM_EPS) * g_ref[...]).astype(o_ref.dtype)


def _rms_norm(x2d, g, out_dtype, tm=256):
    m, d = x2d.shape
    return pl.pallas_call(
        _rms_kernel,
        grid=(m // tm,),
        in_specs=[pl.BlockSpec((tm, d), lambda i: (i, 0)),
                  pl.BlockSpec((1, d), lambda i: (0, 0))],
        out_specs=pl.BlockSpec((tm, d), lambda i: (i, 0)),
        out_shape=jax.ShapeDtypeStruct((m, d), out_dtype),
        compiler_params=_params("parallel"),
        name="rmsnorm",
    )(x2d, g.reshape(1, d))


def _mm_kernel(a_ref, b_ref, o_ref):
    o_ref[...] = jnp.dot(a_ref[...], b_ref[...], preferred_element_type=jnp.float32)


def _mm_res_kernel(a_ref, b_ref, r_ref, o_ref):
    o_ref[...] = r_ref[...] + jnp.dot(a_ref[...], b_ref[...],
                                      preferred_element_type=jnp.float32)


def _matmul(a, b, residual=None, tm=512, tn=1024, name="matmul"):
    m, k = a.shape
    _, n = b.shape
    tm = min(tm, m)
    grid = (n // tn, m // tm)
    in_specs = [pl.BlockSpec((tm, k), lambda j, i: (i, 0)),
                pl.BlockSpec((k, tn), lambda j, i: (0, j))]
    args = [a, b]
    kern = _mm_kernel
    if residual is not None:
        in_specs.append(pl.BlockSpec((tm, tn), lambda j, i: (i, j)))
        args.append(residual)
        kern = _mm_res_kernel
    return pl.pallas_call(
        kern,
        grid=grid,
        in_specs=in_specs,
        out_specs=pl.BlockSpec((tm, tn), lambda j, i: (i, j)),
        out_shape=jax.ShapeDtypeStruct((m, n), jnp.float32),
        compiler_params=_params("parallel", "parallel"),
        name=name,
    )(*args)


def _store_lanes(o_ref, lead, res, tm):
    low = lax.broadcasted_iota(jnp.int32, (tm, LANE), 1) < HEAD_DIM
    for bp in range(2):
        ra = res[(2 * bp) * tm:(2 * bp + 1) * tm]
        rb = res[(2 * bp + 1) * tm:(2 * bp + 2) * tm]
        for c in range(res.shape[1] // LANE):
            a = ra[:, c * LANE:(c + 1) * LANE]
            b = rb[:, c * LANE:(c + 1) * LANE]
            even = jnp.where(low, a, pltpu.roll(b, HEAD_DIM, axis=1))
            odd = jnp.where(low, pltpu.roll(a, HEAD_DIM, axis=1), b)
            lanes = pl.ds(bp * LANE, LANE)
            shape = (tm // SUBLANE, SUBLANE, LANE)
            o_ref[lead + (slice(None), 2 * c, slice(None), lanes)] = even.reshape(shape)
            o_ref[lead + (slice(None), 2 * c + 1, slice(None), lanes)] = odd.reshape(shape)


def _inproj_lanes_kernel(x_ref, g_ref, wx_ref, mu_ref, ws_ref, wg_ref,
                         xo_ref, ps_ref, pg_ref, hn_ref, carry_ref, *, shifted_blocks):
    nb, tm, d = x_ref.shape
    tn = ws_ref.shape[0]
    i, j = pl.program_id(0), pl.program_id(1)

    @pl.when(j == 0)
    def _():
        for b in range(nb):
            x = x_ref[b]
            ms = jnp.mean(x * x, axis=-1, keepdims=True)
            hn_ref[b * tm:(b + 1) * tm, :] = (
                x * lax.rsqrt(ms + NORM_EPS) * g_ref[...]).astype(hn_ref.dtype)
        xo_ref[...] = jnp.dot(hn_ref[...], wx_ref[...],
                              preferred_element_type=jnp.float32).reshape(xo_ref.shape)

    @pl.when(j < shifted_blocks)
    def _():
        res = lax.dot_general(hn_ref[...], ws_ref[...], (((1,), (1,)), ((), ())),
                              preferred_element_type=jnp.float32)
        prev = jnp.where(i == 0, 0.0, carry_ref[j])
        row0 = lax.broadcasted_iota(jnp.int32, (tm, tn), 0) == 0
        parts, lasts = [], []
        for b in range(nb):
            rb = res[b * tm:(b + 1) * tm]
            before = jnp.where(row0, prev[b:b + 1], pltpu.roll(rb, 1, axis=0))
            parts.append(rb + (before - rb) * mu_ref[...])
            lasts.append(rb[tm - 1:tm])
        carry_ref[j] = jnp.concatenate(lasts, axis=0)
        _store_lanes(ps_ref, (0,), jnp.concatenate(parts, axis=0), tm)

    @pl.when(j >= shifted_blocks)
    def _():
        res = lax.dot_general(hn_ref[...], wg_ref[...], (((1,), (1,)), ((), ())),
                              preferred_element_type=jnp.float32)
        _store_lanes(pg_ref, (), res, tm)


def _inproj_lanes(x3, norm_w, w_x, mu_s, w_shift, w_gate, tm=256, tn=512):
    batch, seq, d = x3.shape
    assert batch == 4 and d == 64 * HEAD_DIM
    nv = tn // HEAD_DIM
    per_seg = d // tn
    ns = w_shift.shape[0] // tn
    ng = w_gate.shape[0] // tn
    tile = (HEAD_DIM, SUBLANE, 2 * LANE)
    blk = (tm // SUBLANE, nv, SUBLANE, 2 * LANE)
    js = lambda j: jnp.minimum(j, ns - 1)
    jg = lambda j: jnp.maximum(j - ns, 0)
    return pl.pallas_call(
        functools.partial(_inproj_lanes_kernel, shifted_blocks=ns),
        grid=(seq // tm, ns + ng),
        in_specs=[pl.BlockSpec((batch, tm, d), lambda i, j: (0, i, 0),
                               pipeline_mode=pl.Buffered(1)),
                  pl.BlockSpec((1, d), lambda i, j: (0, 0)),
                  pl.BlockSpec((d, w_x.shape[1]), lambda i, j: (0, 0)),
                  pl.BlockSpec((1, tn), lambda i, j: (0, js(j))),
                  pl.BlockSpec((tn, d), lambda i, j: (js(j), 0)),
                  pl.BlockSpec((tn, d), lambda i, j: (jg(j), 0))],
        out_specs=[pl.BlockSpec((batch, tm, w_x.shape[1]), lambda i, j: (0, i, 0)),
                   pl.BlockSpec((1,) + blk,
                                lambda i, j: (js(j) // per_seg, i, js(j) % per_seg, 0, 0)),
                   pl.BlockSpec(blk, lambda i, j: (i, jg(j), 0, 0))],
        out_shape=[jax.ShapeDtypeStruct((batch, seq, w_x.shape[1]), jnp.float32),
                   jax.ShapeDtypeStruct((ns // per_seg, seq // SUBLANE) + tile, jnp.float32),
                   jax.ShapeDtypeStruct((seq // SUBLANE,) + tile, jnp.float32)],
        scratch_shapes=[pltpu.VMEM((batch * tm, d), jnp.bfloat16),
                        pltpu.VMEM((ns, batch, tn), jnp.float32)],
        compiler_params=_params("arbitrary", "arbitrary"),
        name="rwkv_in_proj",
    )(x3, norm_w.reshape(1, d), w_x, mu_s.reshape(1, -1), w_shift, w_gate)


def _lora_kernel(x_ref, xp_ref, mu_ref, w2_ref, a2_ref, w0_ref, a0_ref, dec_ref, a_ref):
    nb, tm, _ = x_ref.shape
    first = pl.program_id(0) == 0
    row0 = lax.broadcasted_iota(jnp.int32, (tm, 2 * LORA), 0) == 0
    xw, xa = [], []
    for b in range(nb):
        x = x_ref[b]
        last = jnp.where(first, 0.0, xp_ref[b][SUBLANE - 1:SUBLANE])
        before = jnp.where(row0, last, pltpu.roll(x, 1, axis=0))
        x = x + (before - x) * mu_ref[...]
        xw.append(jnp.tanh(x[:, :LORA]).astype(jnp.bfloat16))
        xa.append(x[:, LORA:].astype(jnp.bfloat16))
    wl = w0_ref[...] + jnp.dot(jnp.concatenate(xw, axis=0), w2_ref[...],
                               preferred_element_type=jnp.float32)
    al = a0_ref[...] + jnp.dot(jnp.concatenate(xa, axis=0), a2_ref[...],
                               preferred_element_type=jnp.float32)
    sig_w = _sigmoid(wl)
    _store_lanes(dec_ref, (), jnp.exp(sig_w * (-float(np.exp(-0.5)))), tm)
    _store_lanes(a_ref, (), _sigmoid(al), tm)


def _lora_lanes(x3, mu_x, w2p, a2p, w0p, a0p, tm=256, tn=512):
    batch, seq, _ = x3.shape
    d = w2p.shape[1]
    nv = tn // HEAD_DIM
    out = jax.ShapeDtypeStruct((seq // SUBLANE, HEAD_DIM, SUBLANE, 2 * LANE), jnp.float32)
    ospec = pl.BlockSpec((tm // SUBLANE, nv, SUBLANE, 2 * LANE), lambda i, j: (i, j, 0, 0))
    tq = tm // SUBLANE
    return pl.pallas_call(
        _lora_kernel,
        grid=(seq // tm, d // tn),
        in_specs=[pl.BlockSpec((batch, tm, 2 * LORA), lambda i, j: (0, i, 0)),
                  pl.BlockSpec((batch, SUBLANE, 2 * LORA),
                               lambda i, j: (0, jnp.maximum(i * tq - 1, 0), 0)),
                  pl.BlockSpec((1, 2 * LORA), lambda i, j: (0, 0)),
                  pl.BlockSpec((LORA, tn), lambda i, j: (0, j)),
                  pl.BlockSpec((LORA, tn), lambda i, j: (0, j)),
                  pl.BlockSpec((1, tn), lambda i, j: (0, j)),
                  pl.BlockSpec((1, tn), lambda i, j: (0, j))],
        out_specs=[ospec, ospec],
        out_shape=[out, out],
        compiler_params=_params("parallel", "arbitrary"),
        name="rwkv_lora",
    )(x3, x3, mu_x.reshape(1, -1), w2p, a2p, w0p.reshape(1, -1), a0p.reshape(1, -1))


def _scan_kernel(*refs, n_casts):
    (r_ref, k_ref, v_ref, g_ref, w_ref, a_ref,
     kkp_ref, kap_ref, rkp_ref, gnw_ref, gnb_ref) = refs[:11]
    cast_in = refs[11:11 + n_casts]
    o_ref = refs[11 + n_casts]
    cast_out = refs[12 + n_casts:12 + 2 * n_casts]
    s_ref, gend_s, gcar_ref, bv_s, r_s, gam_s, k_s, kk_s, b_s, v_s, y_s = refs[12 + 2 * n_casts:]
    for src, dst in zip(cast_in, cast_out):
        dst[...] = src[...].astype(dst.dtype)
    tq = w_ref.shape[0]
    tb = tq * SUBLANE
    n = HEAD_DIM
    ch = SUBLANE

    @pl.when(pl.program_id(0) == 0)
    def _():
        s_ref[...] = jnp.zeros_like(s_ref)
        gend_s[...] = jnp.ones_like(gend_s)

    sub = lax.broadcasted_iota(jnp.int32, (ch, SUBLANE, 2 * LANE), 1)

    def to_steps(dst, pitch, q, c0, val):
        for c in range(ch):
            for slab in range(2):
                dst[slab, pl.ds(q * SUBLANE * pitch + c0 + c, SUBLANE, stride=pitch), :] = (
                    val[c, :, slab * LANE:(slab + 1) * LANE])

    def k_terms(q, cs):
        k = k_ref[0, q, cs]
        a = a_ref[q, cs]
        return k * kkp_ref[cs], k * (1.0 + (a - 1.0) * kap_ref[cs]), a

    for q in range(tq):
        def sums(j, carry, q=q):
            ssq, bon = carry
            cs = pl.ds(pl.multiple_of(j * ch, ch), ch)
            kkr, k2, _ = k_terms(q, cs)
            return ssq + kkr * kkr, bon + r_ref[0, q, cs] * k2 * rkp_ref[cs]

        zeros = jnp.zeros((ch, SUBLANE, 2 * LANE), jnp.float32)
        ssq, bon = lax.fori_loop(0, n // ch, sums, (zeros, zeros))
        nrm = jnp.sqrt(jnp.sum(ssq, axis=0, keepdims=True))
        inv_nrm = 1.0 / jnp.maximum(nrm, L2_EPS)
        bonus = jnp.sum(bon, axis=0, keepdims=True)

        def emit(j, carry, q=q, inv_nrm=inv_nrm, bonus=bonus):
            c0 = pl.multiple_of(j * ch, ch)
            cs = pl.ds(c0, ch)
            kkr, k2, a = k_terms(q, cs)
            v = v_ref[0, q, cs]
            kk = kkr * inv_nrm
            gam = w_ref[q, cs]
            for sh in (1, 2, 4):
                gam = gam * jnp.where(sub >= sh, pltpu.roll(gam, sh, axis=1), 1.0)
            before_tile = 1.0 if q == 0 else gcar_ref[cs]
            gam = gam * before_tile
            gam_prev = jnp.where(sub == 0, before_tile, pltpu.roll(gam, 1, axis=1))
            inv_gam = 1.0 / gam
            if q + 1 < tq:
                gcar_ref[cs] = jnp.broadcast_to(gam[:, SUBLANE - 1:SUBLANE, :], gam.shape)
            bv_s[q, cs] = bonus * v
            to_steps(r_s, ROW_PITCH, q, c0, r_ref[0, q, cs] * gam)
            to_steps(gam_s, ROW_PITCH, q, c0, gam)
            to_steps(k_s, ROW_PITCH, q, c0, k2 * inv_gam)
            to_steps(kk_s, ROW_PITCH, q, c0, kk * gam_prev)
            to_steps(b_s, ROW_PITCH, q, c0, kk * a * inv_gam)
            to_steps(v_s, PITCH, q, c0, v)
            return carry

        lax.fori_loop(0, n // ch, emit, 0)

    zero = jnp.zeros((n, LANE), jnp.float32)

    def row(ref, slab, i):
        return ref[slab, pl.ds(i, 1), :]

    for slab in range(2):
        def sa_first(c, acc, slab=slab):
            s = s_ref[slab, c] * row(gend_s, slab, c)
            s_ref[slab, c] = s
            return acc + s * row(kk_s, slab, c)

        def step(t, sa, slab=slab):
            base = t * ROW_PITCH
            ahead = jnp.minimum(t + 1, tb - 1) * ROW_PITCH
            tile = pl.ds(pl.multiple_of(t * PITCH, SUBLANE), n)
            vv = v_s[slab, tile, :]

            def channel(c, carry):
                y_acc, sa_next = carry
                s_new = (s_ref[slab, c] - sa * row(b_s, slab, base + c)
                         + vv * row(k_s, slab, base + c))
                s_ref[slab, c] = s_new
                return (y_acc + s_new * row(r_s, slab, base + c),
                        sa_next + s_new * row(kk_s, slab, ahead + c))

            y, sa_next = lax.fori_loop(0, n, channel, (zero, zero), unroll=True)
            y_s[slab, tile, :] = y
            return sa_next

        lax.fori_loop(0, tb, step, lax.fori_loop(0, n, sa_first, zero, unroll=8))

        gend_s[slab] = gam_s[slab, pl.ds((tb - 1) * ROW_PITCH, n), :]

    for q in range(tq):
        for slab in range(2):
            lanes = pl.ds(slab * LANE, LANE)
            y = jnp.stack([y_s[slab, pl.ds(q * SUBLANE * PITCH + c, SUBLANE, stride=PITCH), :]
                           for c in range(n)], axis=0)
            mean = jnp.mean(y, axis=0, keepdims=True)
            yc = y - mean
            var = jnp.mean(yc * yc, axis=0, keepdims=True)
            y = yc * lax.rsqrt(var + GN_EPS) * gnw_ref[:, :, lanes] + gnb_ref[:, :, lanes]
            y = y + bv_s[q, :, :, lanes]
            g = g_ref[q, :, :, lanes]
            o_ref[q, :, :, lanes] = y * (g * _sigmoid(g))


def _scan(p3, g4, dec4, a4, head_params, side_casts=(), tb=SCAN_TB):
    _, tqs, n, _, lanes = p3.shape
    tq = min(tb // SUBLANE, tqs)
    steps = tqs // tq
    cast_specs = [pl.BlockSpec((w.shape[0] // steps, w.shape[1]), lambda i: (i, 0))
                  for w in side_casts]
    tile = (n, SUBLANE, lanes)
    seg = lambda s: pl.BlockSpec((1, tq) + tile, lambda i, s=s: (s, i, 0, 0, 0))
    blk = pl.BlockSpec((tq,) + tile, lambda i: (i, 0, 0, 0))
    par = pl.BlockSpec(tile, lambda i: (0, 0, 0))
    rows = (2, tq * SUBLANE * ROW_PITCH, LANE)
    tiles = (2, tq * SUBLANE * PITCH, LANE)
    scratch = [pltpu.VMEM((2, n, n, LANE), jnp.float32),
               pltpu.VMEM((2, n, LANE), jnp.float32),
               pltpu.VMEM(tile, jnp.float32),
               pltpu.VMEM((tq,) + tile, jnp.float32)]
    scratch += [pltpu.VMEM(rows, jnp.float32) for _ in range(5)]
    scratch += [pltpu.VMEM(tiles, jnp.float32) for _ in range(2)]
    outs = pl.pallas_call(
        functools.partial(_scan_kernel, n_casts=len(side_casts)),
        grid=(steps,),
        in_specs=([seg(0), seg(1), seg(2), blk, blk, blk] + [par] * len(head_params)
                  + cast_specs),
        out_specs=[blk] + cast_specs,
        out_shape=[jax.ShapeDtypeStruct((tqs,) + tile, jnp.float32)]
        + [jax.ShapeDtypeStruct(w.shape, jnp.bfloat16) for w in side_casts],
        scratch_shapes=scratch,
        compiler_params=_params("arbitrary"),
        name="rwkv_scan",
    )(p3, p3, p3, g4, dec4, a4, *head_params, *side_casts)
    return outs[0], outs[1:]


def _outproj_lanes_kernel(y_ref, w_ref, r_ref, o_ref, lhs_ref):
    tq = y_ref.shape[0]
    tm = tq * SUBLANE
    nb, _, d = lhs_ref.shape

    @pl.when(pl.program_id(1) == 0)
    def _():
        low = lax.broadcasted_iota(jnp.int32, (tm, LANE), 1) < HEAD_DIM
        for c in range(HEAD_DIM // 2):
            cols = pl.ds(c * LANE, LANE)
            for bp in range(2):
                lanes = pl.ds(bp * LANE, LANE)
                e = y_ref[:, 2 * c, :, lanes].reshape(tm, LANE)
                o = y_ref[:, 2 * c + 1, :, lanes].reshape(tm, LANE)
                lhs_ref[2 * bp, :, cols] = jnp.where(
                    low, e, pltpu.roll(o, HEAD_DIM, axis=1)).astype(lhs_ref.dtype)
                lhs_ref[2 * bp + 1, :, cols] = jnp.where(
                    low, pltpu.roll(e, HEAD_DIM, axis=1), o).astype(lhs_ref.dtype)

    res = jnp.dot(lhs_ref[...].reshape(nb * tm, d), w_ref[...],
                  preferred_element_type=jnp.float32)
    o_ref[...] = r_ref[...] + res.reshape(o_ref.shape)


def _outproj_lanes(y4, w_perm, residual3, tm=256, tn=512):
    batch, seq, d = residual3.shape
    tq = tm // SUBLANE
    ospec = pl.BlockSpec((batch, tm, tn), lambda i, j: (0, i, j))
    return pl.pallas_call(
        _outproj_lanes_kernel,
        grid=(seq // tm, d // tn),
        in_specs=[pl.BlockSpec((tq, HEAD_DIM, SUBLANE, 2 * LANE), lambda i, j: (i, 0, 0, 0),
                               pipeline_mode=pl.Buffered(1)),
                  pl.BlockSpec((d, tn), lambda i, j: (0, j)),
                  ospec],
        out_specs=ospec,
        out_shape=jax.ShapeDtypeStruct((batch, seq, d), jnp.float32),
        scratch_shapes=[pltpu.VMEM((batch, tm, d), jnp.bfloat16)],
        compiler_params=_params("parallel", "arbitrary"),
        name="rwkv_out_proj",
    )(y4, w_perm, residual3)


def _cols_nh(w, segs=1):
    lead = w.shape[:-1]
    return w.reshape(lead + (segs, -1, HEAD_DIM)).swapaxes(-1, -2).reshape(w.shape)


def _transpose_nh_kernel(w_ref, o_ref, t_ref):
    tk = w_ref.shape[0]
    n = HEAD_DIM
    heads = w_ref.shape[1] // n
    slabs = tk // LANE
    for pair in range(heads // 2):
        wt = w_ref[:, pair * LANE:(pair + 1) * LANE].T
        for half in range(2):
            for slab in range(slabs):
                t_ref[slab, pl.ds((2 * pair + half) * PITCH, n), :] = (
                    wt[half * n:(half + 1) * n, slab * LANE:(slab + 1) * LANE])
    rows_per_store = 2 * SUBLANE

    def per_channel(c, carry):
        for slab in range(slabs):
            for g in range(heads // rows_per_store):
                parts = [t_ref[slab, pl.ds((g * 2 + u) * SUBLANE * PITCH + c, SUBLANE, stride=PITCH), :]
                         for u in range(2)]
                row0 = pl.multiple_of(c * heads + g * rows_per_store, rows_per_store)
                o_ref[pl.ds(row0, rows_per_store), slab * LANE:(slab + 1) * LANE] = (
                    jnp.concatenate(parts, axis=0).astype(o_ref.dtype))
        return carry

    lax.fori_loop(0, n, per_channel, 0, unroll=4)


def _transpose_nh(w, col0, segs, tk=512):
    k = w.shape[0]
    d = 64 * HEAD_DIM
    assert col0 % LANE == 0
    return pl.pallas_call(
        _transpose_nh_kernel,
        grid=(segs, k // tk),
        in_specs=[pl.BlockSpec((pl.Element(tk), pl.Element(d)),
                               lambda s, i: (pl.multiple_of(i * tk, tk), pl.multiple_of(col0 + s * d, LANE)))],
        out_specs=pl.BlockSpec((d, tk), lambda s, i: (s, i)),
        out_shape=jax.ShapeDtypeStruct((segs * d, k), jnp.bfloat16),
        scratch_shapes=[pltpu.VMEM((tk // LANE, 64 * PITCH, LANE), jnp.float32)],
        compiler_params=_params("parallel", "parallel"),
        name="weight_transpose",
    )(w)


def _param_lanes(p, batch):
    pt = jnp.tile(p.reshape(-1, HEAD_DIM).T, (1, batch))
    return jnp.broadcast_to(pt[:, None, :], (HEAD_DIM, SUBLANE, pt.shape[1]))


def _rwkv_layer(h2d, batch, seq, norm_w, w_in, mu, w0, w2, a0, a2, k_k, k_a, r_k, gn_w,
                gn_b, w_out, side_casts=()):
    d = h2d.shape[1]
    bf = jnp.bfloat16
    x_cols = slice(3 * d, 3 * d + 2 * LORA)
    x, p3, g4 = _inproj_lanes(h2d.reshape(batch, seq, d), norm_w, w_in[:, x_cols].astype(bf),
                              _cols_nh(mu[:3 * d], 3),
                              _transpose_nh(w_in, 0, 3),
                              _transpose_nh(w_in, 3 * d + 2 * LORA, 1))
    dec4, a4 = _lora_lanes(x, mu[x_cols],
                           _cols_nh(w2).astype(bf), _cols_nh(a2).astype(bf),
                           _cols_nh(w0), _cols_nh(a0))
    head_params = [_param_lanes(x, batch) for x in (k_k, k_a, r_k.reshape(-1), gn_w, gn_b)]
    w_out_perm = w_out.reshape(-1, HEAD_DIM, d).swapaxes(0, 1).reshape(d, d)
    y4, casted = _scan(p3, g4, dec4, a4, head_params, (w_out_perm,) + tuple(side_casts))
    out = _outproj_lanes(y4, casted[0], h2d.reshape(batch, seq, d))
    return out.reshape(batch * seq, d), casted[1:]


def _attn_kernel(sinks_ref, q_ref, kc_ref, kp_ref, vc_ref, vp_ref, g_ref, o_ref):
    bf = jnp.bfloat16
    n = HEAD_DIM
    pairs = GROUP // 2
    nq = pairs * BLOCK
    not_first = pl.program_id(1) > 0
    kj = lax.broadcasted_iota(jnp.int32, (2 * BLOCK, nq), 0)
    qi = lax.broadcasted_iota(jnp.int32, (2 * BLOCK, nq), 1) & (BLOCK - 1)
    delta = BLOCK + qi - kj
    mask = (delta >= 0) & (delta < BLOCK) & ((kj >= BLOCK) | not_first)
    sink_slot = kj == 0
    low_kv = lax.broadcasted_iota(jnp.int32, (2 * BLOCK, LANE), 1) < n
    key0_kv = lax.broadcasted_iota(jnp.int32, (2 * BLOCK, LANE), 0) == 0
    for hk in range(KV_HEADS):
        cols = slice((hk // 2) * LANE, (hk // 2 + 1) * LANE)
        kt = jnp.concatenate([kp_ref[:, cols], kc_ref[:, cols]], axis=0)
        vt = jnp.concatenate([vp_ref[:, cols], vc_ref[:, cols]], axis=0)
        kr = pltpu.roll(kt, n, axis=1)
        vr = pltpu.roll(vt, n, axis=1)
        if hk % 2 == 0:
            k_low, k_high = jnp.where(low_kv, kt, 0.0), jnp.where(low_kv, 0.0, kr)
            v_both = jnp.where(low_kv, vt, vr)
        else:
            k_low, k_high = jnp.where(low_kv, kr, 0.0), jnp.where(low_kv, 0.0, kt)
            v_both = jnp.where(low_kv, vr, vt)
        v_t = jnp.where(key0_kv, 0.0, v_both).T.astype(bf)
        qs = jnp.concatenate([q_ref[:, (hk * pairs + p) * LANE:(hk * pairs + p + 1) * LANE]
                              for p in range(pairs)], axis=0)
        qs = (qs * (n ** -0.5)).astype(bf)
        outs = []
        for parity, k_sel in ((0, k_low), (1, k_high)):
            s = lax.dot_general(k_sel.astype(bf), qs, (((1,), (1,)), ((), ())),
                                preferred_element_type=jnp.float32)
            sink = jnp.concatenate(
                [jnp.full((1, BLOCK), sinks_ref[hk * GROUP + 2 * p + parity], jnp.float32)
                 for p in range(pairs)], axis=1)
            s = jnp.where(mask, s, jnp.where(sink_slot, sink, MASK_VALUE))
            m = jnp.max(s, axis=0, keepdims=True)
            e = jnp.exp(s - m)
            inv = 1.0 / jnp.sum(e, axis=0, keepdims=True)
            outs.append(jnp.dot(v_t, (e * inv).astype(bf), preferred_element_type=jnp.float32))
        for p in range(pairs):
            col = (hk * pairs + p) * LANE
            qcols = slice(p * BLOCK, (p + 1) * BLOCK)
            o = jnp.concatenate([outs[0][:n, qcols], outs[1][n:, qcols]], axis=0).T
            gate = g_ref[:, col:col + LANE]
            o_ref[:, col:col + LANE] = (o * (gate * _sigmoid(gate))).astype(o_ref.dtype)


def _attention(p, batch, seq, d, sinks):
    nb = seq // BLOCK
    kvw = KV_HEADS * HEAD_DIM
    row = lambda b, n: b * nb + n
    prow = lambda b, n: b * nb + jnp.maximum(n - 1, 0)
    kblk, vblk = d // kvw, d // kvw + 1
    in_specs = [pl.BlockSpec(memory_space=pltpu.SMEM),
                pl.BlockSpec((BLOCK, d), lambda b, n: (row(b, n), 0)),
                pl.BlockSpec((BLOCK, kvw), lambda b, n: (row(b, n), kblk)),
                pl.BlockSpec((BLOCK, kvw), lambda b, n: (prow(b, n), kblk)),
                pl.BlockSpec((BLOCK, kvw), lambda b, n: (row(b, n), vblk)),
                pl.BlockSpec((BLOCK, kvw), lambda b, n: (prow(b, n), vblk))]
    in_specs += [pl.BlockSpec((pl.Element(BLOCK), pl.Element(d)),
                              lambda b, n: (pl.multiple_of(row(b, n) * BLOCK, BLOCK), d + 2 * kvw))]
    return pl.pallas_call(
        _attn_kernel,
        grid=(batch, nb),
        in_specs=in_specs,
        out_specs=pl.BlockSpec((BLOCK, d), lambda b, n: (row(b, n), 0)),
        out_shape=jax.ShapeDtypeStruct((batch * seq, d), jnp.bfloat16),
        compiler_params=_params("parallel", "parallel"),
        name="swa_attention",
    )(sinks, p, p, p, p, p, p)


def _attn_layer(h2d, batch, seq, norm_w, w_in_bf, sinks, w_out_bf):
    d = h2d.shape[1]
    hn = _rms_norm(h2d, norm_w, jnp.bfloat16)
    p = _matmul(hn, w_in_bf, tn=1024, name="attn_in_proj")
    o = _attention(p, batch, seq, d, sinks)
    return _matmul(o, w_out_bf, residual=h2d, name="attn_out_proj")


def kernel(x, norm_w, final_norm_w, rwkv_w_in, rwkv_mu, rwkv_w0, rwkv_w2, rwkv_a0, rwkv_a2,
           rwkv_k_k, rwkv_k_a, rwkv_r_k, rwkv_gn_w, rwkv_gn_b, rwkv_w_out,
           attn_w_in, attn_sinks, attn_w_out):
    batch, seq, d = x.shape
    h = x.reshape(batch * seq, d)
    depth = norm_w.shape[0]
    bf_weights = {}
    for i in range(depth):
        j = i // 2
        if i % 2 == 0:
            ahead = (attn_w_in[j], attn_w_out[j]) if i + 1 < depth else ()
            h, casted = _rwkv_layer(h, batch, seq, norm_w[i], rwkv_w_in[j], rwkv_mu[j],
                                    rwkv_w0[j], rwkv_w2[j], rwkv_a0[j], rwkv_a2[j], rwkv_k_k[j],
                                    rwkv_k_a[j], rwkv_r_k[j], rwkv_gn_w[j], rwkv_gn_b[j],
                                    rwkv_w_out[j], ahead)
            if ahead:
                bf_weights[i + 1] = casted
        else:
            w_in_bf, w_out_bf = bf_weights.get(i) or (attn_w_in[j].astype(jnp.bfloat16),
                                                      attn_w_out[j].astype(jnp.bfloat16))
            h = _attn_layer(h, batch, seq, norm_w[i], w_in_bf, attn_sinks[j], w_out_bf)
    return _rms_norm(h, final_norm_w, jnp.float32).reshape(batch, seq, d)
```

```python
import functools

import jax
import jax.numpy as jnp
import numpy as np
from jax import lax
from jax.experimental import pallas as pl
from jax.experimental.pallas import tpu as pltpu

HEAD_DIM = 64
LORA = 128
KV_HEADS = 8
GROUP = 8
BLOCK = 128
NORM_EPS = 1e-5
GN_EPS = HEAD_DIM * 1e-5
L2_EPS = 1e-12
MASK_VALUE = -1e30

LANE = 128
SUBLANE = 8
VMEM_LIMIT_BYTES = 56 * 1024 * 1024
SCAN_TB = 16
PITCH = HEAD_DIM + SUBLANE
ROW_PITCH = HEAD_DIM + 4


def _sigmoid(x):
    return 0.5 * jnp.tanh(0.5 * x) + 0.5


def _params(*sem):
    return pltpu.CompilerParams(dimension_semantics=sem, vmem_limit_bytes=VMEM_LIMIT_BYTES)


def _rms_kernel(x_ref, g_ref, o_ref):
    x = x_ref[...]
    ms = jnp.mean(x * x, axis=-1, keepdims=True)
    o_ref[...] = (x * lax.rsqrt(ms + NORM_EPS) * g_ref[...]).astype(o_ref.dtype)


def _rms_norm(x2d, g, out_dtype, tm=256):
    m, d = x2d.shape
    return pl.pallas_call(
        _rms_kernel,
        grid=(m // tm,),
        in_specs=[pl.BlockSpec((tm, d), lambda i: (i, 0)),
                  pl.BlockSpec((1, d), lambda i: (0, 0))],
        out_specs=pl.BlockSpec((tm, d), lambda i: (i, 0)),
        out_shape=jax.ShapeDtypeStruct((m, d), out_dtype),
        compiler_params=_params("parallel"),
        name="rmsnorm",
    )(x2d, g.reshape(1, d))


def _mm_kernel(a_ref, b_ref, o_ref):
    o_ref[...] = jnp.dot(a_ref[...], b_ref[...], preferred_element_type=jnp.float32)


def _mm_res_kernel(a_ref, b_ref, r_ref, o_ref):
    o_ref[...] = r_ref[...] + jnp.dot(a_ref[...], b_ref[...],
                                      preferred_element_type=jnp.float32)


def _matmul(a, b, residual=None, tm=512, tn=1024, name="matmul"):
    m, k = a.shape
    _, n = b.shape
    tm = min(tm, m)
    grid = (n // tn, m // tm)
    in_specs = [pl.BlockSpec((tm, k), lambda j, i: (i, 0)),
                pl.BlockSpec((k, tn), lambda j, i: (0, j))]
    args = [a, b]
    kern = _mm_kernel
    if residual is not None:
        in_specs.append(pl.BlockSpec((tm, tn), lambda j, i: (i, j)))
        args.append(residual)
        kern = _mm_res_kernel
    return pl.pallas_call(
        kern,
        grid=grid,
        in_specs=in_specs,
        out_specs=pl.BlockSpec((tm, tn), lambda j, i: (i, j)),
        out_shape=jax.ShapeDtypeStruct((m, n), jnp.float32),
        compiler_params=_params("parallel", "parallel"),
        name=name,
    )(*args)


def _store_lanes(o_ref, lead, res, tm):
    low = lax.broadcasted_iota(jnp.int32, (tm, LANE), 1) < HEAD_DIM
    for bp in range(2):
        ra = res[(2 * bp) * tm:(2 * bp + 1) * tm]
        rb = res[(2 * bp + 1) * tm:(2 * bp + 2) * tm]
        for c in range(res.shape[1] // LANE):
            a = ra[:, c * LANE:(c + 1) * LANE]
            b = rb[:, c * LANE:(c + 1) * LANE]
            even = jnp.where(low, a, pltpu.roll(b, HEAD_DIM, axis=1))
            odd = jnp.where(low, pltpu.roll(a, HEAD_DIM, axis=1), b)
            lanes = pl.ds(bp * LANE, LANE)
            shape = (tm // SUBLANE, SUBLANE, LANE)
            o_ref[lead + (slice(None), 2 * c, slice(None), lanes)] = even.reshape(shape)
            o_ref[lead + (slice(None), 2 * c + 1, slice(None), lanes)] = odd.reshape(shape)


def _inproj_lanes_kernel(x_ref, g_ref, wx_ref, mu_ref, w_ref,
                         xo_ref, ps_ref, pg_ref, hn_ref, carry_ref, *, shifted_blocks):
    nb, tm, d = x_ref.shape
    tn = w_ref.shape[0]
    i, j = pl.program_id(0), pl.program_id(1)

    @pl.when(j == 0)
    def _():
        for b in range(nb):
            x = x_ref[b]
            ms = jnp.mean(x * x, axis=-1, keepdims=True)
            hn_ref[b * tm:(b + 1) * tm, :] = (
                x * lax.rsqrt(ms + NORM_EPS) * g_ref[...]).astype(hn_ref.dtype)
        xo_ref[...] = jnp.dot(hn_ref[...], wx_ref[...],
                              preferred_element_type=jnp.float32).reshape(xo_ref.shape)

    @pl.when(j < shifted_blocks)
    def _():
        res = lax.dot_general(hn_ref[...], w_ref[...], (((1,), (1,)), ((), ())),
                              preferred_element_type=jnp.float32)
        prev = jnp.where(i == 0, 0.0, carry_ref[j])
        row0 = lax.broadcasted_iota(jnp.int32, (tm, tn), 0) == 0
        parts, lasts = [], []
        for b in range(nb):
            rb = res[b * tm:(b + 1) * tm]
            before = jnp.where(row0, prev[b:b + 1], pltpu.roll(rb, 1, axis=0))
            parts.append(rb + (before - rb) * mu_ref[...])
            lasts.append(rb[tm - 1:tm])
        carry_ref[j] = jnp.concatenate(lasts, axis=0)
        _store_lanes(ps_ref, (0,), jnp.concatenate(parts, axis=0), tm)

    @pl.when(j >= shifted_blocks)
    def _():
        res = lax.dot_general(hn_ref[...], w_ref[...], (((1,), (1,)), ((), ())),
                              preferred_element_type=jnp.float32)
        _store_lanes(pg_ref, (), res, tm)


def _inproj_lanes(x3, norm_w, w_x, mu_s, w_t, tm=256, tn=512):
    batch, seq, d = x3.shape
    assert batch == 4 and d == 64 * HEAD_DIM
    nv = tn // HEAD_DIM
    per_seg = d // tn
    ns = mu_s.shape[0] // tn
    ng = w_t.shape[0] // tn - ns
    tile = (HEAD_DIM, SUBLANE, 2 * LANE)
    blk = (tm // SUBLANE, nv, SUBLANE, 2 * LANE)
    js = lambda j: jnp.minimum(j, ns - 1)
    jg = lambda j: jnp.maximum(j - ns, 0)
    return pl.pallas_call(
        functools.partial(_inproj_lanes_kernel, shifted_blocks=ns),
        grid=(seq // tm, ns + ng),
        in_specs=[pl.BlockSpec((batch, tm, d), lambda i, j: (0, i, 0),
                               pipeline_mode=pl.Buffered(1)),
                  pl.BlockSpec((1, d), lambda i, j: (0, 0)),
                  pl.BlockSpec((d, w_x.shape[1]), lambda i, j: (0, 0)),
                  pl.BlockSpec((1, tn), lambda i, j: (0, js(j))),
                  pl.BlockSpec((tn, d), lambda i, j: (j, 0))],
        out_specs=[pl.BlockSpec((batch, tm, w_x.shape[1]), lambda i, j: (0, i, 0)),
                   pl.BlockSpec((1,) + blk,
                                lambda i, j: (js(j) // per_seg, i, js(j) % per_seg, 0, 0)),
                   pl.BlockSpec(blk, lambda i, j: (i, jg(j), 0, 0))],
        out_shape=[jax.ShapeDtypeStruct((batch, seq, w_x.shape[1]), jnp.float32),
                   jax.ShapeDtypeStruct((ns // per_seg, seq // SUBLANE) + tile, jnp.float32),
                   jax.ShapeDtypeStruct((seq // SUBLANE,) + tile, jnp.float32)],
        scratch_shapes=[pltpu.VMEM((batch * tm, d), jnp.bfloat16),
                        pltpu.VMEM((ns, batch, tn), jnp.float32)],
        compiler_params=_params("arbitrary", "arbitrary"),
        name="rwkv_in_proj",
    )(x3, norm_w.reshape(1, d), w_x, mu_s.reshape(1, -1), w_t)


def _lora_kernel(x_ref, xp_ref, mu_ref, w2_ref, a2_ref, w0_ref, a0_ref, dec_ref, a_ref):
    nb, tm, _ = x_ref.shape
    first = pl.program_id(0) == 0
    row0 = lax.broadcasted_iota(jnp.int32, (tm, 2 * LORA), 0) == 0
    xw, xa = [], []
    for b in range(nb):
        x = x_ref[b]
        last = jnp.where(first, 0.0, xp_ref[b][SUBLANE - 1:SUBLANE])
        before = jnp.where(row0, last, pltpu.roll(x, 1, axis=0))
        x = x + (before - x) * mu_ref[...]
        xw.append(jnp.tanh(x[:, :LORA]).astype(jnp.bfloat16))
        xa.append(x[:, LORA:].astype(jnp.bfloat16))
    wl = w0_ref[...] + jnp.dot(jnp.concatenate(xw, axis=0), w2_ref[...],
                               preferred_element_type=jnp.float32)
    al = a0_ref[...] + jnp.dot(jnp.concatenate(xa, axis=0), a2_ref[...],
                               preferred_element_type=jnp.float32)
    sig_w = _sigmoid(wl)
    _store_lanes(dec_ref, (), jnp.exp(sig_w * (-float(np.exp(-0.5)))), tm)
    _store_lanes(a_ref, (), _sigmoid(al), tm)


def _lora_lanes(x3, mu_x, w2p, a2p, w0p, a0p, tm=256, tn=512):
    batch, seq, _ = x3.shape
    d = w2p.shape[1]
    nv = tn // HEAD_DIM
    out = jax.ShapeDtypeStruct((seq // SUBLANE, HEAD_DIM, SUBLANE, 2 * LANE), jnp.float32)
    ospec = pl.BlockSpec((tm // SUBLANE, nv, SUBLANE, 2 * LANE), lambda i, j: (i, j, 0, 0))
    tq = tm // SUBLANE
    return pl.pallas_call(
        _lora_kernel,
        grid=(seq // tm, d // tn),
        in_specs=[pl.BlockSpec((batch, tm, 2 * LORA), lambda i, j: (0, i, 0)),
                  pl.BlockSpec((batch, SUBLANE, 2 * LORA),
                               lambda i, j: (0, jnp.maximum(i * tq - 1, 0), 0)),
                  pl.BlockSpec((1, 2 * LORA), lambda i, j: (0, 0)),
                  pl.BlockSpec((LORA, tn), lambda i, j: (0, j)),
                  pl.BlockSpec((LORA, tn), lambda i, j: (0, j)),
                  pl.BlockSpec((1, tn), lambda i, j: (0, j)),
                  pl.BlockSpec((1, tn), lambda i, j: (0, j))],
        out_specs=[ospec, ospec],
        out_shape=[out, out],
        compiler_params=_params("parallel", "arbitrary"),
        name="rwkv_lora",
    )(x3, x3, mu_x.reshape(1, -1), w2p, a2p, w0p.reshape(1, -1), a0p.reshape(1, -1))


def _scan_kernel(*refs, n_casts):
    (r_ref, k_ref, v_ref, g_ref, w_ref, a_ref,
     kkp_ref, kap_ref, rkp_ref, gnw_ref, gnb_ref) = refs[:11]
    cast_in = refs[11:11 + n_casts]
    o_ref = refs[11 + n_casts]
    cast_out = refs[12 + n_casts:12 + 2 * n_casts]
    s_ref, gend_s, gcar_ref, bv_s, r_s, gam_s, k_s, kk_s, b_s, v_s, y_s = refs[12 + 2 * n_casts:]
    for src, dst in zip(cast_in, cast_out):
        dst[...] = src[...].astype(dst.dtype)
    tq = w_ref.shape[0]
    tb = tq * SUBLANE
    n = HEAD_DIM
    ch = SUBLANE

    @pl.when(pl.program_id(0) == 0)
    def _():
        s_ref[...] = jnp.zeros_like(s_ref)
        gend_s[...] = jnp.ones_like(gend_s)

    sub = lax.broadcasted_iota(jnp.int32, (ch, SUBLANE, 2 * LANE), 1)

    def to_steps(dst, pitch, q, c0, val):
        for c in range(ch):
            for slab in range(2):
                dst[slab, pl.ds(q * SUBLANE * pitch + c0 + c, SUBLANE, stride=pitch), :] = (
                    val[c, :, slab * LANE:(slab + 1) * LANE])

    def k_terms(q, cs):
        k = k_ref[0, q, cs]
        a = a_ref[q, cs]
        return k * kkp_ref[cs], k * (1.0 + (a - 1.0) * kap_ref[cs]), a

    for q in range(tq):
        def sums(j, carry, q=q):
            ssq, bon = carry
            cs = pl.ds(pl.multiple_of(j * ch, ch), ch)
            kkr, k2, _ = k_terms(q, cs)
            return ssq + kkr * kkr, bon + r_ref[0, q, cs] * k2 * rkp_ref[cs]

        zeros = jnp.zeros((ch, SUBLANE, 2 * LANE), jnp.float32)
        ssq, bon = lax.fori_loop(0, n // ch, sums, (zeros, zeros))
        nrm = jnp.sqrt(jnp.sum(ssq, axis=0, keepdims=True))
        inv_nrm = 1.0 / jnp.maximum(nrm, L2_EPS)
        bonus = jnp.sum(bon, axis=0, keepdims=True)

        def emit(j, carry, q=q, inv_nrm=inv_nrm, bonus=bonus):
            c0 = pl.multiple_of(j * ch, ch)
            cs = pl.ds(c0, ch)
            kkr, k2, a = k_terms(q, cs)
            v = v_ref[0, q, cs]
            kk = kkr * inv_nrm
            gam = w_ref[q, cs]
            for sh in (1, 2, 4):
                gam = gam * jnp.where(sub >= sh, pltpu.roll(gam, sh, axis=1), 1.0)
            before_tile = 1.0 if q == 0 else gcar_ref[cs]
            gam = gam * before_tile
            gam_prev = jnp.where(sub == 0, before_tile, pltpu.roll(gam, 1, axis=1))
            inv_gam = 1.0 / gam
            if q + 1 < tq:
                gcar_ref[cs] = jnp.broadcast_to(gam[:, SUBLANE - 1:SUBLANE, :], gam.shape)
            bv_s[q, cs] = bonus * v
            to_steps(r_s, ROW_PITCH, q, c0, r_ref[0, q, cs] * gam)
            to_steps(gam_s, ROW_PITCH, q, c0, gam)
            to_steps(k_s, ROW_PITCH, q, c0, k2 * inv_gam)
            to_steps(kk_s, ROW_PITCH, q, c0, kk * gam_prev)
            to_steps(b_s, ROW_PITCH, q, c0, kk * a * inv_gam)
            to_steps(v_s, PITCH, q, c0, v)
            return carry

        lax.fori_loop(0, n // ch, emit, 0)

    zero = jnp.zeros((n, LANE), jnp.float32)

    def row(ref, slab, i):
        return ref[slab, pl.ds(i, 1), :]

    for slab in range(2):
        def sa_first(c, acc, slab=slab):
            s = s_ref[slab, c] * row(gend_s, slab, c)
            s_ref[slab, c] = s
            return acc + s * row(kk_s, slab, c)

        def step(t, sa, slab=slab):
            base = t * ROW_PITCH
            ahead = jnp.minimum(t + 1, tb - 1) * ROW_PITCH
            tile = pl.ds(pl.multiple_of(t * PITCH, SUBLANE), n)
            vv = v_s[slab, tile, :]

            def channel(c, carry):
                y_acc, sa_next = carry
                s_new = (s_ref[slab, c] - sa * row(b_s, slab, base + c)
                         + vv * row(k_s, slab, base + c))
                s_ref[slab, c] = s_new
                return (y_acc + s_new * row(r_s, slab, base + c),
                        sa_next + s_new * row(kk_s, slab, ahead + c))

            y, sa_next = lax.fori_loop(0, n, channel, (zero, zero), unroll=True)
            y_s[slab, tile, :] = y
            return sa_next

        lax.fori_loop(0, tb, step, lax.fori_loop(0, n, sa_first, zero, unroll=8))

        gend_s[slab] = gam_s[slab, pl.ds((tb - 1) * ROW_PITCH, n), :]

    for q in range(tq):
        for slab in range(2):
            lanes = pl.ds(slab * LANE, LANE)
            y = jnp.stack([y_s[slab, pl.ds(q * SUBLANE * PITCH + c, SUBLANE, stride=PITCH), :]
                           for c in range(n)], axis=0)
            mean = jnp.mean(y, axis=0, keepdims=True)
            yc = y - mean
            var = jnp.mean(yc * yc, axis=0, keepdims=True)
            y = yc * lax.rsqrt(var + GN_EPS) * gnw_ref[:, :, lanes] + gnb_ref[:, :, lanes]
            y = y + bv_s[q, :, :, lanes]
            g = g_ref[q, :, :, lanes]
            o_ref[q, :, :, lanes] = y * (g * _sigmoid(g))


def _scan(p3, g4, dec4, a4, head_params, side_casts=(), tb=SCAN_TB):
    _, tqs, n, _, lanes = p3.shape
    tq = min(tb // SUBLANE, tqs)
    steps = tqs // tq
    cast_specs = [pl.BlockSpec((w.shape[0] // steps, w.shape[1]), lambda i: (i, 0))
                  for w in side_casts]
    tile = (n, SUBLANE, lanes)
    seg = lambda s: pl.BlockSpec((1, tq) + tile, lambda i, s=s: (s, i, 0, 0, 0))
    blk = pl.BlockSpec((tq,) + tile, lambda i: (i, 0, 0, 0))
    par = pl.BlockSpec(tile, lambda i: (0, 0, 0))
    rows = (2, tq * SUBLANE * ROW_PITCH, LANE)
    tiles = (2, tq * SUBLANE * PITCH, LANE)
    scratch = [pltpu.VMEM((2, n, n, LANE), jnp.float32),
               pltpu.VMEM((2, n, LANE), jnp.float32),
               pltpu.VMEM(tile, jnp.float32),
               pltpu.VMEM((tq,) + tile, jnp.float32)]
    scratch += [pltpu.VMEM(rows, jnp.float32) for _ in range(5)]
    scratch += [pltpu.VMEM(tiles, jnp.float32) for _ in range(2)]
    outs = pl.pallas_call(
        functools.partial(_scan_kernel, n_casts=len(side_casts)),
        grid=(steps,),
        in_specs=([seg(0), seg(1), seg(2), blk, blk, blk] + [par] * len(head_params)
                  + cast_specs),
        out_specs=[blk] + cast_specs,
        out_shape=[jax.ShapeDtypeStruct((tqs,) + tile, jnp.float32)]
        + [jax.ShapeDtypeStruct(w.shape, jnp.bfloat16) for w in side_casts],
        scratch_shapes=scratch,
        compiler_params=_params("arbitrary"),
        name="rwkv_scan",
    )(p3, p3, p3, g4, dec4, a4, *head_params, *side_casts)
    return outs[0], outs[1:]


def _outproj_lanes_kernel(y_ref, w_ref, r_ref, o_ref, lhs_ref):
    tq = y_ref.shape[0]
    tm = tq * SUBLANE
    nb, _, d = lhs_ref.shape

    @pl.when(pl.program_id(1) == 0)
    def _():
        low = lax.broadcasted_iota(jnp.int32, (tm, LANE), 1) < HEAD_DIM
        for c in range(HEAD_DIM // 2):
            cols = pl.ds(c * LANE, LANE)
            for bp in range(2):
                lanes = pl.ds(bp * LANE, LANE)
                e = y_ref[:, 2 * c, :, lanes].reshape(tm, LANE)
                o = y_ref[:, 2 * c + 1, :, lanes].reshape(tm, LANE)
                lhs_ref[2 * bp, :, cols] = jnp.where(
                    low, e, pltpu.roll(o, HEAD_DIM, axis=1)).astype(lhs_ref.dtype)
                lhs_ref[2 * bp + 1, :, cols] = jnp.where(
                    low, pltpu.roll(e, HEAD_DIM, axis=1), o).astype(lhs_ref.dtype)

    res = jnp.dot(lhs_ref[...].reshape(nb * tm, d), w_ref[...],
                  preferred_element_type=jnp.float32)
    o_ref[...] = r_ref[...] + res.reshape(o_ref.shape)


def _outproj_lanes(y4, w_perm, residual3, tm=256, tn=512):
    batch, seq, d = residual3.shape
    tq = tm // SUBLANE
    ospec = pl.BlockSpec((batch, tm, tn), lambda i, j: (0, i, j))
    return pl.pallas_call(
        _outproj_lanes_kernel,
        grid=(seq // tm, d // tn),
        in_specs=[pl.BlockSpec((tq, HEAD_DIM, SUBLANE, 2 * LANE), lambda i, j: (i, 0, 0, 0),
                               pipeline_mode=pl.Buffered(1)),
                  pl.BlockSpec((d, tn), lambda i, j: (0, j)),
                  ospec],
        out_specs=ospec,
        out_shape=jax.ShapeDtypeStruct((batch, seq, d), jnp.float32),
        scratch_shapes=[pltpu.VMEM((batch, tm, d), jnp.bfloat16)],
        compiler_params=_params("parallel", "arbitrary"),
        name="rwkv_out_proj",
    )(y4, w_perm, residual3)


def _cols_nh(w, segs=1):
    lead = w.shape[:-1]
    return w.reshape(lead + (segs, -1, HEAD_DIM)).swapaxes(-1, -2).reshape(w.shape)


def _transpose_nh_kernel(w_ref, o_ref, t_ref):
    tk = w_ref.shape[0]
    n = HEAD_DIM
    heads = w_ref.shape[1] // n
    slabs = tk // LANE
    for pair in range(heads // 2):
        wt = w_ref[:, pair * LANE:(pair + 1) * LANE].T
        for half in range(2):
            for slab in range(slabs):
                t_ref[slab, pl.ds((2 * pair + half) * PITCH, n), :] = (
                    wt[half * n:(half + 1) * n, slab * LANE:(slab + 1) * LANE])
    rows_per_store = 2 * SUBLANE

    def per_channel(c, carry):
        for slab in range(slabs):
            for g in range(heads // rows_per_store):
                parts = [t_ref[slab, pl.ds((g * 2 + u) * SUBLANE * PITCH + c, SUBLANE, stride=PITCH), :]
                         for u in range(2)]
                row0 = pl.multiple_of(c * heads + g * rows_per_store, rows_per_store)
                o_ref[pl.ds(row0, rows_per_store), slab * LANE:(slab + 1) * LANE] = (
                    jnp.concatenate(parts, axis=0).astype(o_ref.dtype))
        return carry

    lax.fori_loop(0, n, per_channel, 0, unroll=4)


def _transpose_nh(w, col_offsets, tk=512):
    k = w.shape[0]
    d = 64 * HEAD_DIM
    segs = len(col_offsets)
    assert all(c % LANE == 0 for c in col_offsets)

    def col(s):
        return pl.multiple_of(sum(jnp.where(s == i, c, 0) for i, c in enumerate(col_offsets)), LANE)

    return pl.pallas_call(
        _transpose_nh_kernel,
        grid=(segs, k // tk),
        in_specs=[pl.BlockSpec((pl.Element(tk), pl.Element(d)),
                               lambda s, i: (pl.multiple_of(i * tk, tk), col(s)))],
        out_specs=pl.BlockSpec((d, tk), lambda s, i: (s, i)),
        out_shape=jax.ShapeDtypeStruct((segs * d, k), jnp.bfloat16),
        scratch_shapes=[pltpu.VMEM((tk // LANE, 64 * PITCH, LANE), jnp.float32)],
        compiler_params=_params("parallel", "parallel"),
        name="weight_transpose",
    )(w)


def _param_lanes(p, batch):
    pt = jnp.tile(p.reshape(-1, HEAD_DIM).T, (1, batch))
    return jnp.broadcast_to(pt[:, None, :], (HEAD_DIM, SUBLANE, pt.shape[1]))


def _rwkv_layer(h2d, batch, seq, norm_w, w_in, mu, w0, w2, a0, a2, k_k, k_a, r_k, gn_w,
                gn_b, w_out, side_casts=()):
    d = h2d.shape[1]
    bf = jnp.bfloat16
    x_cols = slice(3 * d, 3 * d + 2 * LORA)
    x, p3, g4 = _inproj_lanes(h2d.reshape(batch, seq, d), norm_w, w_in[:, x_cols].astype(bf),
                              _cols_nh(mu[:3 * d], 3),
                              _transpose_nh(w_in, (0, d, 2 * d, 3 * d + 2 * LORA)))
    dec4, a4 = _lora_lanes(x, mu[x_cols],
                           _cols_nh(w2).astype(bf), _cols_nh(a2).astype(bf),
                           _cols_nh(w0), _cols_nh(a0))
    head_params = [_param_lanes(x, batch) for x in (k_k, k_a, r_k.reshape(-1), gn_w, gn_b)]
    w_out_perm = w_out.reshape(-1, HEAD_DIM, d).swapaxes(0, 1).reshape(d, d)
    y4, casted = _scan(p3, g4, dec4, a4, head_params, (w_out_perm,) + tuple(side_casts))
    out = _outproj_lanes(y4, casted[0], h2d.reshape(batch, seq, d))
    return out.reshape(batch * seq, d), casted[1:]


def _attn_kernel(sinks_ref, q_ref, kc_ref, kp_ref, vc_ref, vp_ref, g_ref, o_ref):
    bf = jnp.bfloat16
    n = HEAD_DIM
    pairs = GROUP // 2
    nq = pairs * BLOCK
    not_first = pl.program_id(1) > 0
    kj = lax.broadcasted_iota(jnp.int32, (2 * BLOCK, nq), 0)
    qi = lax.broadcasted_iota(jnp.int32, (2 * BLOCK, nq), 1) & (BLOCK - 1)
    delta = BLOCK + qi - kj
    mask = (delta >= 0) & (delta < BLOCK) & ((kj >= BLOCK) | not_first)
    sink_slot = kj == 0
    low_kv = lax.broadcasted_iota(jnp.int32, (2 * BLOCK, LANE), 1) < n
    key0_kv = lax.broadcasted_iota(jnp.int32, (2 * BLOCK, LANE), 0) == 0
    for hk in range(KV_HEADS):
        cols = slice((hk // 2) * LANE, (hk // 2 + 1) * LANE)
        kt = jnp.concatenate([kp_ref[:, cols], kc_ref[:, cols]], axis=0)
        vt = jnp.concatenate([vp_ref[:, cols], vc_ref[:, cols]], axis=0)
        kr = pltpu.roll(kt, n, axis=1)
        vr = pltpu.roll(vt, n, axis=1)
        if hk % 2 == 0:
            k_low, k_high = jnp.where(low_kv, kt, 0.0), jnp.where(low_kv, 0.0, kr)
            v_both = jnp.where(low_kv, vt, vr)
        else:
            k_low, k_high = jnp.where(low_kv, kr, 0.0), jnp.where(low_kv, 0.0, kt)
            v_both = jnp.where(low_kv, vr, vt)
        v_t = jnp.where(key0_kv, 0.0, v_both).T.astype(bf)
        qs = jnp.concatenate([q_ref[:, (hk * pairs + p) * LANE:(hk * pairs + p + 1) * LANE]
                              for p in range(pairs)], axis=0)
        qs = (qs * (n ** -0.5)).astype(bf)
        outs = []
        for parity, k_sel in ((0, k_low), (1, k_high)):
            s = lax.dot_general(k_sel.astype(bf), qs, (((1,), (1,)), ((), ())),
                                preferred_element_type=jnp.float32)
            sink = jnp.concatenate(
                [jnp.full((1, BLOCK), sinks_ref[hk * GROUP + 2 * p + parity], jnp.float32)
                 for p in range(pairs)], axis=1)
            s = jnp.where(mask, s, jnp.where(sink_slot, sink, MASK_VALUE))
            m = jnp.max(s, axis=0, keepdims=True)
            e = jnp.exp(s - m)
            inv = 1.0 / jnp.sum(e, axis=0, keepdims=True)
            outs.append(jnp.dot(v_t, (e * inv).astype(bf), preferred_element_type=jnp.float32))
        for p in range(pairs):
            col = (hk * pairs + p) * LANE
            qcols = slice(p * BLOCK, (p + 1) * BLOCK)
            o = jnp.concatenate([outs[0][:n, qcols], outs[1][n:, qcols]], axis=0).T
            gate = g_ref[:, col:col + LANE]
            o_ref[:, col:col + LANE] = (o * (gate * _sigmoid(gate))).astype(o_ref.dtype)


def _attention(p, batch, seq, d, sinks):
    nb = seq // BLOCK
    kvw = KV_HEADS * HEAD_DIM
    row = lambda b, n: b * nb + n
    prow = lambda b, n: b * nb + jnp.maximum(n - 1, 0)
    kblk, vblk = d // kvw, d // kvw + 1
    in_specs = [pl.BlockSpec(memory_space=pltpu.SMEM),
                pl.BlockSpec((BLOCK, d), lambda b, n: (row(b, n), 0)),
                pl.BlockSpec((BLOCK, kvw), lambda b, n: (row(b, n), kblk)),
                pl.BlockSpec((BLOCK, kvw), lambda b, n: (prow(b, n), kblk)),
                pl.BlockSpec((BLOCK, kvw), lambda b, n: (row(b, n), vblk)),
                pl.BlockSpec((BLOCK, kvw), lambda b, n: (prow(b, n), vblk))]
    in_specs += [pl.BlockSpec((pl.Element(BLOCK), pl.Element(d)),
                              lambda b, n: (pl.multiple_of(row(b, n) * BLOCK, BLOCK), d + 2 * kvw))]
    return pl.pallas_call(
        _attn_kernel,
        grid=(batch, nb),
        in_specs=in_specs,
        out_specs=pl.BlockSpec((BLOCK, d), lambda b, n: (row(b, n), 0)),
        out_shape=jax.ShapeDtypeStruct((batch * seq, d), jnp.bfloat16),
        compiler_params=_params("parallel", "parallel"),
        name="swa_attention",
    )(sinks, p, p, p, p, p, p)


def _attn_layer(h2d, batch, seq, norm_w, w_in_bf, sinks, w_out_bf):
    d = h2d.shape[1]
    hn = _rms_norm(h2d, norm_w, jnp.bfloat16)
    p = _matmul(hn, w_in_bf, tn=1024, name="attn_in_proj")
    o = _attention(p, batch, seq, d, sinks)
    return _matmul(o, w_out_bf, residual=h2d, name="attn_out_proj")


def kernel(x, norm_w, final_norm_w, rwkv_w_in, rwkv_mu, rwkv_w0, rwkv_w2, rwkv_a0, rwkv_a2,
           rwkv_k_k, rwkv_k_a, rwkv_r_k, rwkv_gn_w, rwkv_gn_b, rwkv_w_out,
           attn_w_in, attn_sinks, attn_w_out):
    batch, seq, d = x.shape
    h = x.reshape(batch * seq, d)
    depth = norm_w.shape[0]
    bf_weights = {}
    for i in range(depth):
        j = i // 2
        if i % 2 == 0:
            ahead = (attn_w_in[j], attn_w_out[j]) if i + 1 < depth else ()
            h, casted = _rwkv_layer(h, batch, seq, norm_w[i], rwkv_w_in[j], rwkv_mu[j],
                                    rwkv_w0[j], rwkv_w2[j], rwkv_a0[j], rwkv_a2[j], rwkv_k_k[j],
                                    rwkv_k_a[j], rwkv_r_k[j], rwkv_gn_w[j], rwkv_gn_b[j],
                                    rwkv_w_out[j], ahead)
            if ahead:
                bf_weights[i + 1] = casted
        else:
            w_in_bf, w_out_bf = bf_weights.get(i) or (attn_w_in[j].astype(jnp.bfloat16),
                                                      attn_w_out[j].astype(jnp.bfloat16))
            h = _attn_layer(h, batch, seq, norm_w[i], w_in_bf, attn_sinks[j], w_out_bf)
    return _rms_norm(h, final_norm_w, jnp.float32).reshape(batch, seq, d)
```

```python
import functools

import jax
import jax.numpy as jnp
import numpy as np
from jax import lax
from jax.experimental import pallas as pl
from jax.experimental.pallas import tpu as pltpu

HEAD_DIM = 64
LORA = 128
KV_HEADS = 8
GROUP = 8
BLOCK = 128
NORM_EPS = 1e-5
GN_EPS = HEAD_DIM * 1e-5
L2_EPS = 1e-12
MASK_VALUE = -1e30

LANE = 128
SUBLANE = 8
VMEM_LIMIT_BYTES = 56 * 1024 * 1024
SCAN_TB = 16
PITCH = HEAD_DIM + SUBLANE
ROW_PITCH = HEAD_DIM + 4


def _sigmoid(x):
    return 0.5 * jnp.tanh(0.5 * x) + 0.5


def _params(*sem):
    return pltpu.CompilerParams(dimension_semantics=sem, vmem_limit_bytes=VMEM_LIMIT_BYTES)


def _rms_kernel(x_ref, g_ref, o_ref):
    x = x_ref[...]
    ms = jnp.mean(x * x, axis=-1, keepdims=True)
    o_ref[...] = (x * lax.rsqrt(ms + NORM_EPS) * g_ref[...]).astype(o_ref.dtype)


def _rms_norm(x2d, g, out_dtype, tm=256):
    m, d = x2d.shape
    return pl.pallas_call(
        _rms_kernel,
        grid=(m // tm,),
        in_specs=[pl.BlockSpec((tm, d), lambda i: (i, 0)),
                  pl.BlockSpec((1, d), lambda i: (0, 0))],
        out_specs=pl.BlockSpec((tm, d), lambda i: (i, 0)),
        out_shape=jax.ShapeDtypeStruct((m, d), out_dtype),
        compiler_params=_params("parallel"),
        name="rmsnorm",
    )(x2d, g.reshape(1, d))


def _mm_kernel(a_ref, b_ref, o_ref):
    o_ref[...] = jnp.dot(a_ref[...], b_ref[...], preferred_element_type=jnp.float32)


def _mm_res_kernel(a_ref, b_ref, r_ref, o_ref):
    o_ref[...] = r_ref[...] + jnp.dot(a_ref[...], b_ref[...],
                                      preferred_element_type=jnp.float32)


def _matmul(a, b, residual=None, tm=512, tn=1024, name="matmul"):
    m, k = a.shape
    _, n = b.shape
    tm = min(tm, m)
    grid = (n // tn, m // tm)
    in_specs = [pl.BlockSpec((tm, k), lambda j, i: (i, 0)),
                pl.BlockSpec((k, tn), lambda j, i: (0, j))]
    args = [a, b]
    kern = _mm_kernel
    if residual is not None:
        in_specs.append(pl.BlockSpec((tm, tn), lambda j, i: (i, j)))
        args.append(residual)
        kern = _mm_res_kernel
    return pl.pallas_call(
        kern,
        grid=grid,
        in_specs=in_specs,
        out_specs=pl.BlockSpec((tm, tn), lambda j, i: (i, j)),
        out_shape=jax.ShapeDtypeStruct((m, n), jnp.float32),
        compiler_params=_params("parallel", "parallel"),
        name=name,
    )(*args)


def _store_lanes(o_ref, lead, res, tm):
    low = lax.broadcasted_iota(jnp.int32, (tm, LANE), 1) < HEAD_DIM
    for bp in range(2):
        ra = res[(2 * bp) * tm:(2 * bp + 1) * tm]
        rb = res[(2 * bp + 1) * tm:(2 * bp + 2) * tm]
        for c in range(res.shape[1] // LANE):
            a = ra[:, c * LANE:(c + 1) * LANE]
            b = rb[:, c * LANE:(c + 1) * LANE]
            even = jnp.where(low, a, pltpu.roll(b, HEAD_DIM, axis=1))
            odd = jnp.where(low, pltpu.roll(a, HEAD_DIM, axis=1), b)
            lanes = pl.ds(bp * LANE, LANE)
            shape = (tm // SUBLANE, SUBLANE, LANE)
            o_ref[lead + (slice(None), 2 * c, slice(None), lanes)] = even.reshape(shape)
            o_ref[lead + (slice(None), 2 * c + 1, slice(None), lanes)] = odd.reshape(shape)


def _inproj_lanes_kernel(x_ref, g_ref, wx_ref, mu_ref, ws_ref, wg_ref,
                         xo_ref, ps_ref, pg_ref, hn_ref, carry_ref, *, shifted_blocks):
    nb, tm, d = x_ref.shape
    tn = ws_ref.shape[0]
    i, j = pl.program_id(0), pl.program_id(1)

    @pl.when(j == 0)
    def _():
        for b in range(nb):
            x = x_ref[b]
            ms = jnp.mean(x * x, axis=-1, keepdims=True)
            hn_ref[b * tm:(b + 1) * tm, :] = (
                x * lax.rsqrt(ms + NORM_EPS) * g_ref[...]).astype(hn_ref.dtype)
        xo_ref[...] = jnp.dot(hn_ref[...], wx_ref[...],
                              preferred_element_type=jnp.float32).reshape(xo_ref.shape)

    @pl.when(j < shifted_blocks)
    def _():
        res = lax.dot_general(hn_ref[...], ws_ref[...], (((1,), (1,)), ((), ())),
                              preferred_element_type=jnp.float32)
        prev = jnp.where(i == 0, 0.0, carry_ref[j])
        row0 = lax.broadcasted_iota(jnp.int32, (tm, tn), 0) == 0
        parts, lasts = [], []
        for b in range(nb):
            rb = res[b * tm:(b + 1) * tm]
            before = jnp.where(row0, prev[b:b + 1], pltpu.roll(rb, 1, axis=0))
            parts.append(rb + (before - rb) * mu_ref[...])
            lasts.append(rb[tm - 1:tm])
        carry_ref[j] = jnp.concatenate(lasts, axis=0)
        _store_lanes(ps_ref, (0,), jnp.concatenate(parts, axis=0), tm)

    @pl.when(j >= shifted_blocks)
    def _():
        res = lax.dot_general(hn_ref[...], wg_ref[...], (((1,), (1,)), ((), ())),
                              preferred_element_type=jnp.float32)
        _store_lanes(pg_ref, (), res, tm)


def _inproj_lanes(x3, norm_w, w_x, mu_s, w_shift, w_gate, tm=256, tn=512):
    batch, seq, d = x3.shape
    assert batch == 4 and d == 64 * HEAD_DIM
    nv = tn // HEAD_DIM
    per_seg = d // tn
    ns = w_shift.shape[0] // tn
    ng = w_gate.shape[0] // tn
    tile = (HEAD_DIM, SUBLANE, 2 * LANE)
    blk = (tm // SUBLANE, nv, SUBLANE, 2 * LANE)
    js = lambda j: jnp.minimum(j, ns - 1)
    jg = lambda j: jnp.maximum(j - ns, 0)
    return pl.pallas_call(
        functools.partial(_inproj_lanes_kernel, shifted_blocks=ns),
        grid=(seq // tm, ns + ng),
        in_specs=[pl.BlockSpec((batch, tm, d), lambda i, j: (0, i, 0),
                               pipeline_mode=pl.Buffered(1)),
                  pl.BlockSpec((1, d), lambda i, j: (0, 0)),
                  pl.BlockSpec((d, w_x.shape[1]), lambda i, j: (0, 0)),
                  pl.BlockSpec((1, tn), lambda i, j: (0, js(j))),
                  pl.BlockSpec((tn, d), lambda i, j: (js(j), 0)),
                  pl.BlockSpec((tn, d), lambda i, j: (jg(j), 0))],
        out_specs=[pl.BlockSpec((batch, tm, w_x.shape[1]), lambda i, j: (0, i, 0)),
                   pl.BlockSpec((1,) + blk,
                                lambda i, j: (js(j) // per_seg, i, js(j) % per_seg, 0, 0)),
                   pl.BlockSpec(blk, lambda i, j: (i, jg(j), 0, 0))],
        out_shape=[jax.ShapeDtypeStruct((batch, seq, w_x.shape[1]), jnp.float32),
                   jax.ShapeDtypeStruct((ns // per_seg, seq // SUBLANE) + tile, jnp.float32),
                   jax.ShapeDtypeStruct((seq // SUBLANE,) + tile, jnp.float32)],
        scratch_shapes=[pltpu.VMEM((batch * tm, d), jnp.bfloat16),
                        pltpu.VMEM((ns, batch, tn), jnp.float32)],
        compiler_params=_params("arbitrary", "arbitrary"),
        name="rwkv_in_proj",
    )(x3, norm_w.reshape(1, d), w_x, mu_s.reshape(1, -1), w_shift, w_gate)


def _lora_kernel(x_ref, xp_ref, mu_ref, w2_ref, a2_ref, w0_ref, a0_ref, dec_ref, a_ref):
    nb, tm, _ = x_ref.shape
    first = pl.program_id(0) == 0
    row0 = lax.broadcasted_iota(jnp.int32, (tm, 2 * LORA), 0) == 0
    xw, xa = [], []
    for b in range(nb):
        x = x_ref[b]
        last = jnp.where(first, 0.0, xp_ref[b][SUBLANE - 1:SUBLANE])
        before = jnp.where(row0, last, pltpu.roll(x, 1, axis=0))
        x = x + (before - x) * mu_ref[...]
        xw.append(jnp.tanh(x[:, :LORA]).astype(jnp.bfloat16))
        xa.append(x[:, LORA:].astype(jnp.bfloat16))
    wl = w0_ref[...] + jnp.dot(jnp.concatenate(xw, axis=0), w2_ref[...],
                               preferred_element_type=jnp.float32)
    al = a0_ref[...] + jnp.dot(jnp.concatenate(xa, axis=0), a2_ref[...],
                               preferred_element_type=jnp.float32)
    sig_w = _sigmoid(wl)
    _store_lanes(dec_ref, (), jnp.exp(sig_w * (-float(np.exp(-0.5)))), tm)
    _store_lanes(a_ref, (), _sigmoid(al), tm)


def _lora_lanes(x3, mu_x, w2p, a2p, w0p, a0p, tm=256, tn=512):
    batch, seq, _ = x3.shape
    d = w2p.shape[1]
    nv = tn // HEAD_DIM
    out = jax.ShapeDtypeStruct((seq // SUBLANE, HEAD_DIM, SUBLANE, 2 * LANE), jnp.float32)
    ospec = pl.BlockSpec((tm // SUBLANE, nv, SUBLANE, 2 * LANE), lambda i, j: (i, j, 0, 0))
    tq = tm // SUBLANE
    return pl.pallas_call(
        _lora_kernel,
        grid=(seq // tm, d // tn),
        in_specs=[pl.BlockSpec((batch, tm, 2 * LORA), lambda i, j: (0, i, 0)),
                  pl.BlockSpec((batch, SUBLANE, 2 * LORA),
                               lambda i, j: (0, jnp.maximum(i * tq - 1, 0), 0)),
                  pl.BlockSpec((1, 2 * LORA), lambda i, j: (0, 0)),
                  pl.BlockSpec((LORA, tn), lambda i, j: (0, j)),
                  pl.BlockSpec((LORA, tn), lambda i, j: (0, j)),
                  pl.BlockSpec((1, tn), lambda i, j: (0, j)),
                  pl.BlockSpec((1, tn), lambda i, j: (0, j))],
        out_specs=[ospec, ospec],
        out_shape=[out, out],
        compiler_params=_params("parallel", "arbitrary"),
        name="rwkv_lora",
    )(x3, x3, mu_x.reshape(1, -1), w2p, a2p, w0p.reshape(1, -1), a0p.reshape(1, -1))


def _scan_kernel(*refs, n_casts):
    (r_ref, k_ref, v_ref, g_ref, w_ref, a_ref,
     kkp_ref, kap_ref, rkp_ref, gnw_ref, gnb_ref) = refs[:11]
    cast_in = refs[11:11 + n_casts]
    o_ref = refs[11 + n_casts]
    cast_out = refs[12 + n_casts:12 + 2 * n_casts]
    s_ref, gend_s, gcar_ref, bv_s, r_s, gam_s, k_s, kk_s, b_s, v_s, y_s = refs[12 + 2 * n_casts:]
    for src, dst in zip(cast_in, cast_out):
        dst[...] = src[...].astype(dst.dtype)
    tq = w_ref.shape[0]
    tb = tq * SUBLANE
    n = HEAD_DIM
    ch = SUBLANE

    @pl.when(pl.program_id(0) == 0)
    def _():
        s_ref[...] = jnp.zeros_like(s_ref)
        gend_s[...] = jnp.ones_like(gend_s)

    sub = lax.broadcasted_iota(jnp.int32, (ch, SUBLANE, 2 * LANE), 1)

    def to_steps(dst, pitch, q, c0, val):
        for c in range(ch):
            for slab in range(2):
                dst[slab, pl.ds(q * SUBLANE * pitch + c0 + c, SUBLANE, stride=pitch), :] = (
                    val[c, :, slab * LANE:(slab + 1) * LANE])

    def k_terms(q, cs):
        k = k_ref[0, q, cs]
        a = a_ref[q, cs]
        return k * kkp_ref[cs], k * (1.0 + (a - 1.0) * kap_ref[cs]), a

    for q in range(tq):
        def sums(j, carry, q=q):
            ssq, bon = carry
            cs = pl.ds(pl.multiple_of(j * ch, ch), ch)
            kkr, k2, _ = k_terms(q, cs)
            return ssq + kkr * kkr, bon + r_ref[0, q, cs] * k2 * rkp_ref[cs]

        zeros = jnp.zeros((ch, SUBLANE, 2 * LANE), jnp.float32)
        ssq, bon = lax.fori_loop(0, n // ch, sums, (zeros, zeros))
        nrm = jnp.sqrt(jnp.sum(ssq, axis=0, keepdims=True))
        inv_nrm = 1.0 / jnp.maximum(nrm, L2_EPS)
        bonus = jnp.sum(bon, axis=0, keepdims=True)

        def emit(j, carry, q=q, inv_nrm=inv_nrm, bonus=bonus):
            c0 = pl.multiple_of(j * ch, ch)
            cs = pl.ds(c0, ch)
            kkr, k2, a = k_terms(q, cs)
            v = v_ref[0, q, cs]
            kk = kkr * inv_nrm
            gam = w_ref[q, cs]
            for sh in (1, 2, 4):
                gam = gam * jnp.where(sub >= sh, pltpu.roll(gam, sh, axis=1), 1.0)
            before_tile = 1.0 if q == 0 else gcar_ref[cs]
            gam = gam * before_tile
            gam_prev = jnp.where(sub == 0, before_tile, pltpu.roll(gam, 1, axis=1))
            inv_gam = 1.0 / gam
            if q + 1 < tq:
                gcar_ref[cs] = jnp.broadcast_to(gam[:, SUBLANE - 1:SUBLANE, :], gam.shape)
            bv_s[q, cs] = bonus * v
            to_steps(r_s, ROW_PITCH, q, c0, r_ref[0, q, cs] * gam)
            to_steps(gam_s, ROW_PITCH, q, c0, gam)
            to_steps(k_s, ROW_PITCH, q, c0, k2 * inv_gam)
            to_steps(kk_s, ROW_PITCH, q, c0, kk * gam_prev)
            to_steps(b_s, ROW_PITCH, q, c0, kk * a * inv_gam)
            to_steps(v_s, PITCH, q, c0, v)
            return carry

        lax.fori_loop(0, n // ch, emit, 0)

    zero = jnp.zeros((n, LANE), jnp.float32)

    def row(ref, slab, i):
        return ref[slab, pl.ds(i, 1), :]

    for slab in range(2):
        def sa_first(c, acc, slab=slab):
            s = s_ref[slab, c] * row(gend_s, slab, c)
            s_ref[slab, c] = s
            return acc + s * row(kk_s, slab, c)

        def step(t, sa, slab=slab):
            base = t * ROW_PITCH
            ahead = jnp.minimum(t + 1, tb - 1) * ROW_PITCH
            tile = pl.ds(pl.multiple_of(t * PITCH, SUBLANE), n)
            vv = v_s[slab, tile, :]

            def channel(c, carry):
                y_acc, sa_next = carry
                s_new = (s_ref[slab, c] - sa * row(b_s, slab, base + c)
                         + vv * row(k_s, slab, base + c))
                s_ref[slab, c] = s_new
                return (y_acc + s_new * row(r_s, slab, base + c),
                        sa_next + s_new * row(kk_s, slab, ahead + c))

            y, sa_next = lax.fori_loop(0, n, channel, (zero, zero), unroll=True)
            y_s[slab, tile, :] = y
            return sa_next

        lax.fori_loop(0, tb, step, lax.fori_loop(0, n, sa_first, zero, unroll=8))

        gend_s[slab] = gam_s[slab, pl.ds((tb - 1) * ROW_PITCH, n), :]

    for q in range(tq):
        for slab in range(2):
            lanes = pl.ds(slab * LANE, LANE)
            y = jnp.stack([y_s[slab, pl.ds(q * SUBLANE * PITCH + c, SUBLANE, stride=PITCH), :]
                           for c in range(n)], axis=0)
            mean = jnp.mean(y, axis=0, keepdims=True)
            yc = y - mean
            var = jnp.mean(yc * yc, axis=0, keepdims=True)
            y = yc * lax.rsqrt(var + GN_EPS) * gnw_ref[:, :, lanes] + gnb_ref[:, :, lanes]
            y = y + bv_s[q, :, :, lanes]
            g = g_ref[q, :, :, lanes]
            o_ref[q, :, :, lanes] = y * (g * _sigmoid(g))


def _scan(p3, g4, dec4, a4, head_params, side_casts=(), tb=SCAN_TB):
    _, tqs, n, _, lanes = p3.shape
    tq = min(tb // SUBLANE, tqs)
    steps = tqs // tq
    cast_specs = [pl.BlockSpec((w.shape[0] // steps, w.shape[1]), lambda i: (i, 0))
                  for w in side_casts]
    tile = (n, SUBLANE, lanes)
    seg = lambda s: pl.BlockSpec((1, tq) + tile, lambda i, s=s: (s, i, 0, 0, 0))
    blk = pl.BlockSpec((tq,) + tile, lambda i: (i, 0, 0, 0))
    par = pl.BlockSpec(tile, lambda i: (0, 0, 0))
    rows = (2, tq * SUBLANE * ROW_PITCH, LANE)
    tiles = (2, tq * SUBLANE * PITCH, LANE)
    scratch = [pltpu.VMEM((2, n, n, LANE), jnp.float32),
               pltpu.VMEM((2, n, LANE), jnp.float32),
               pltpu.VMEM(tile, jnp.float32),
               pltpu.VMEM((tq,) + tile, jnp.float32)]
    scratch += [pltpu.VMEM(rows, jnp.float32) for _ in range(5)]
    scratch += [pltpu.VMEM(tiles, jnp.float32) for _ in range(2)]
    outs = pl.pallas_call(
        functools.partial(_scan_kernel, n_casts=len(side_casts)),
        grid=(steps,),
        in_specs=([seg(0), seg(1), seg(2), blk, blk, blk] + [par] * len(head_params)
                  + cast_specs),
        out_specs=[blk] + cast_specs,
        out_shape=[jax.ShapeDtypeStruct((tqs,) + tile, jnp.float32)]
        + [jax.ShapeDtypeStruct(w.shape, jnp.bfloat16) for w in side_casts],
        scratch_shapes=scratch,
        compiler_params=_params("arbitrary"),
        name="rwkv_scan",
    )(p3, p3, p3, g4, dec4, a4, *head_params, *side_casts)
    return outs[0], outs[1:]


def _outproj_lanes_kernel(y_ref, w_ref, r_ref, o_ref, lhs_ref):
    tq = y_ref.shape[0]
    tm = tq * SUBLANE
    nb, _, d = lhs_ref.shape

    @pl.when(pl.program_id(1) == 0)
    def _():
        low = lax.broadcasted_iota(jnp.int32, (tm, LANE), 1) < HEAD_DIM
        for c in range(HEAD_DIM // 2):
            cols = pl.ds(c * LANE, LANE)
            for bp in range(2):
                lanes = pl.ds(bp * LANE, LANE)
                e = y_ref[:, 2 * c, :, lanes].reshape(tm, LANE)
                o = y_ref[:, 2 * c + 1, :, lanes].reshape(tm, LANE)
                lhs_ref[2 * bp, :, cols] = jnp.where(
                    low, e, pltpu.roll(o, HEAD_DIM, axis=1)).astype(lhs_ref.dtype)
                lhs_ref[2 * bp + 1, :, cols] = jnp.where(
                    low, pltpu.roll(e, HEAD_DIM, axis=1), o).astype(lhs_ref.dtype)

    res = jnp.dot(lhs_ref[...].reshape(nb * tm, d), w_ref[...],
                  preferred_element_type=jnp.float32)
    o_ref[...] = r_ref[...] + res.reshape(o_ref.shape)


def _outproj_lanes(y4, w_perm, residual3, tm=256, tn=512):
    batch, seq, d = residual3.shape
    tq = tm // SUBLANE
    ospec = pl.BlockSpec((batch, tm, tn), lambda i, j: (0, i, j))
    return pl.pallas_call(
        _outproj_lanes_kernel,
        grid=(seq // tm, d // tn),
        in_specs=[pl.BlockSpec((tq, HEAD_DIM, SUBLANE, 2 * LANE), lambda i, j: (i, 0, 0, 0),
                               pipeline_mode=pl.Buffered(1)),
                  pl.BlockSpec((d, tn), lambda i, j: (0, j)),
                  ospec],
        out_specs=ospec,
        out_shape=jax.ShapeDtypeStruct((batch, seq, d), jnp.float32),
        scratch_shapes=[pltpu.VMEM((batch, tm, d), jnp.bfloat16)],
        compiler_params=_params("parallel", "arbitrary"),
        name="rwkv_out_proj",
    )(y4, w_perm, residual3)


def _cols_nh(w, segs=1):
    lead = w.shape[:-1]
    return w.reshape(lead + (segs, -1, HEAD_DIM)).swapaxes(-1, -2).reshape(w.shape)


def _transpose_nh_kernel(w_ref, o_ref, t_ref):
    tk = w_ref.shape[0]
    n = HEAD_DIM
    heads = w_ref.shape[1] // n
    slabs = tk // LANE
    for pair in range(heads // 2):
        wt = w_ref[:, pair * LANE:(pair + 1) * LANE].T
        for half in range(2):
            for slab in range(slabs):
                t_ref[slab, pl.ds((2 * pair + half) * PITCH, n), :] = (
                    wt[half * n:(half + 1) * n, slab * LANE:(slab + 1) * LANE])
    rows_per_store = 2 * SUBLANE

    def per_channel(c, carry):
        for slab in range(slabs):
            for g in range(heads // rows_per_store):
                parts = [t_ref[slab, pl.ds((g * 2 + u) * SUBLANE * PITCH + c, SUBLANE, stride=PITCH), :]
                         for u in range(2)]
                row0 = pl.multiple_of(c * heads + g * rows_per_store, rows_per_store)
                o_ref[pl.ds(row0, rows_per_store), slab * LANE:(slab + 1) * LANE] = (
                    jnp.concatenate(parts, axis=0).astype(o_ref.dtype))
        return carry

    lax.fori_loop(0, n, per_channel, 0, unroll=4)


def _transpose_nh(w, col0, segs, tk=512):
    k = w.shape[0]
    d = 64 * HEAD_DIM
    assert col0 % LANE == 0
    return pl.pallas_call(
        _transpose_nh_kernel,
        grid=(segs, k // tk),
        in_specs=[pl.BlockSpec((pl.Element(tk), pl.Element(d)),
                               lambda s, i: (pl.multiple_of(i * tk, tk), pl.multiple_of(col0 + s * d, LANE)))],
        out_specs=pl.BlockSpec((d, tk), lambda s, i: (s, i)),
        out_shape=jax.ShapeDtypeStruct((segs * d, k), jnp.bfloat16),
        scratch_shapes=[pltpu.VMEM((tk // LANE, 64 * PITCH, LANE), jnp.float32)],
        compiler_params=_params("parallel", "parallel"),
        name="weight_transpose",
    )(w)


def _param_lanes(p, batch):
    pt = jnp.tile(p.reshape(-1, HEAD_DIM).T, (1, batch))
    return jnp.broadcast_to(pt[:, None, :], (HEAD_DIM, SUBLANE, pt.shape[1]))


def _rwkv_layer(h2d, batch, seq, norm_w, w_in, mu, w0, w2, a0, a2, k_k, k_a, r_k, gn_w,
                gn_b, w_out, side_casts=()):
    d = h2d.shape[1]
    bf = jnp.bfloat16
    x_cols = slice(3 * d, 3 * d + 2 * LORA)
    x, p3, g4 = _inproj_lanes(h2d.reshape(batch, seq, d), norm_w, w_in[:, x_cols].astype(bf),
                              _cols_nh(mu[:3 * d], 3),
                              _transpose_nh(w_in, 0, 3),
                              _transpose_nh(w_in, 3 * d + 2 * LORA, 1))
    dec4, a4 = _lora_lanes(x, mu[x_cols],
                           _cols_nh(w2).astype(bf), _cols_nh(a2).astype(bf),
                           _cols_nh(w0), _cols_nh(a0))
    head_params = [_param_lanes(x, batch) for x in (k_k, k_a, r_k.reshape(-1), gn_w, gn_b)]
    w_out_perm = w_out.reshape(-1, HEAD_DIM, d).swapaxes(0, 1).reshape(d, d)
    y4, casted = _scan(p3, g4, dec4, a4, head_params, (w_out_perm,) + tuple(side_casts))
    out = _outproj_lanes(y4, casted[0], h2d.reshape(batch, seq, d))
    return out.reshape(batch * seq, d), casted[1:]


def _attn_kernel(sinks_ref, q_ref, kc_ref, vc_ref, g_ref, o_ref, kp_ref, vp_ref):
    @pl.when((pl.program_id(0) == 0) & (pl.program_id(1) == 0))
    def _():
        kp_ref[...] = jnp.zeros_like(kp_ref)
        vp_ref[...] = jnp.zeros_like(vp_ref)

    bf = jnp.bfloat16
    n = HEAD_DIM
    pairs = GROUP // 2
    nq = pairs * BLOCK
    not_first = pl.program_id(1) > 0
    kj = lax.broadcasted_iota(jnp.int32, (2 * BLOCK, nq), 0)
    qi = lax.broadcasted_iota(jnp.int32, (2 * BLOCK, nq), 1) & (BLOCK - 1)
    delta = BLOCK + qi - kj
    mask = (delta >= 0) & (delta < BLOCK) & ((kj >= BLOCK) | not_first)
    sink_slot = kj == 0
    low_kv = lax.broadcasted_iota(jnp.int32, (2 * BLOCK, LANE), 1) < n
    key0_kv = lax.broadcasted_iota(jnp.int32, (2 * BLOCK, LANE), 0) == 0
    for hk in range(KV_HEADS):
        cols = slice((hk // 2) * LANE, (hk // 2 + 1) * LANE)
        kt = jnp.concatenate([kp_ref[:, cols], kc_ref[:, cols]], axis=0)
        vt = jnp.concatenate([vp_ref[:, cols], vc_ref[:, cols]], axis=0)
        kr = pltpu.roll(kt, n, axis=1)
        vr = pltpu.roll(vt, n, axis=1)
        if hk % 2 == 0:
            k_low, k_high = jnp.where(low_kv, kt, 0.0), jnp.where(low_kv, 0.0, kr)
            v_both = jnp.where(low_kv, vt, vr)
        else:
            k_low, k_high = jnp.where(low_kv, kr, 0.0), jnp.where(low_kv, 0.0, kt)
            v_both = jnp.where(low_kv, vr, vt)
        v_t = jnp.where(key0_kv, 0.0, v_both).T.astype(bf)
        qs = jnp.concatenate([q_ref[:, (hk * pairs + p) * LANE:(hk * pairs + p + 1) * LANE]
                              for p in range(pairs)], axis=0)
        qs = (qs * (n ** -0.5)).astype(bf)
        outs = []
        for parity, k_sel in ((0, k_low), (1, k_high)):
            s = lax.dot_general(k_sel.astype(bf), qs, (((1,), (1,)), ((), ())),
                                preferred_element_type=jnp.float32)
            sink = jnp.concatenate(
                [jnp.full((1, BLOCK), sinks_ref[hk * GROUP + 2 * p + parity], jnp.float32)
                 for p in range(pairs)], axis=1)
            s = jnp.where(mask, s, jnp.where(sink_slot, sink, MASK_VALUE))
            m = jnp.max(s, axis=0, keepdims=True)
            e = jnp.exp(s - m)
            inv = 1.0 / jnp.sum(e, axis=0, keepdims=True)
            outs.append(jnp.dot(v_t, (e * inv).astype(bf), preferred_element_type=jnp.float32))
        for p in range(pairs):
            col = (hk * pairs + p) * LANE
            qcols = slice(p * BLOCK, (p + 1) * BLOCK)
            o = jnp.concatenate([outs[0][:n, qcols], outs[1][n:, qcols]], axis=0).T
            gate = g_ref[:, col:col + LANE]
            o_ref[:, col:col + LANE] = (o * (gate * _sigmoid(gate))).astype(o_ref.dtype)
    kp_ref[...] = kc_ref[...]
    vp_ref[...] = vc_ref[...]


def _attention(p, batch, seq, d, sinks):
    nb = seq // BLOCK
    kvw = KV_HEADS * HEAD_DIM
    row = lambda b, n: b * nb + n
    kblk, vblk = d // kvw, d // kvw + 1
    in_specs = [pl.BlockSpec(memory_space=pltpu.SMEM),
                pl.BlockSpec((BLOCK, d), lambda b, n: (row(b, n), 0)),
                pl.BlockSpec((BLOCK, kvw), lambda b, n: (row(b, n), kblk)),
                pl.BlockSpec((BLOCK, kvw), lambda b, n: (row(b, n), vblk))]
    in_specs += [pl.BlockSpec((pl.Element(BLOCK), pl.Element(d)),
                              lambda b, n: (pl.multiple_of(row(b, n) * BLOCK, BLOCK), d + 2 * kvw))]
    return pl.pallas_call(
        _attn_kernel,
        grid=(batch, nb),
        in_specs=in_specs,
        out_specs=pl.BlockSpec((BLOCK, d), lambda b, n: (row(b, n), 0)),
        out_shape=jax.ShapeDtypeStruct((batch * seq, d), jnp.bfloat16),
        scratch_shapes=[pltpu.VMEM((BLOCK, kvw), jnp.float32), pltpu.VMEM((BLOCK, kvw), jnp.float32)],
        compiler_params=_params("arbitrary", "arbitrary"),
        name="swa_attention",
    )(sinks, p, p, p, p)


def _attn_layer(h2d, batch, seq, norm_w, w_in_bf, sinks, w_out_bf):
    d = h2d.shape[1]
    hn = _rms_norm(h2d, norm_w, jnp.bfloat16)
    p = _matmul(hn, w_in_bf, tn=1024, name="attn_in_proj")
    o = _attention(p, batch, seq, d, sinks)
    return _matmul(o, w_out_bf, residual=h2d, name="attn_out_proj")


def kernel(x, norm_w, final_norm_w, rwkv_w_in, rwkv_mu, rwkv_w0, rwkv_w2, rwkv_a0, rwkv_a2,
           rwkv_k_k, rwkv_k_a, rwkv_r_k, rwkv_gn_w, rwkv_gn_b, rwkv_w_out,
           attn_w_in, attn_sinks, attn_w_out):
    batch, seq, d = x.shape
    h = x.reshape(batch * seq, d)
    depth = norm_w.shape[0]
    bf_weights = {}
    for i in range(depth):
        j = i // 2
        if i % 2 == 0:
            ahead = (attn_w_in[j], attn_w_out[j]) if i + 1 < depth else ()
            h, casted = _rwkv_layer(h, batch, seq, norm_w[i], rwkv_w_in[j], rwkv_mu[j],
                                    rwkv_w0[j], rwkv_w2[j], rwkv_a0[j], rwkv_a2[j], rwkv_k_k[j],
                                    rwkv_k_a[j], rwkv_r_k[j], rwkv_gn_w[j], rwkv_gn_b[j],
                                    rwkv_w_out[j], ahead)
            if ahead:
                bf_weights[i + 1] = casted
        else:
            w_in_bf, w_out_bf = bf_weights.get(i) or (attn_w_in[j].astype(jnp.bfloat16),
                                                      attn_w_out[j].astype(jnp.bfloat16))
            h = _attn_layer(h, batch, seq, norm_w[i], w_in_bf, attn_sinks[j], w_out_bf)
    return _rms_norm(h, final_norm_w, jnp.float32).reshape(batch, seq, d)
```
